```python
import math
import jax
import jax.numpy as jnp
from jax import lax
import numpy as np

D_MODEL = 1024
BATCH = 8
SEQ = 2048
DEPTH = 4
DEC_BATCH = 128
DEC_SEQ = 8
PAST_LEN = 2048
PAGE_SIZE = 128

HEAD_DIM = 64
N_GROUPS = 4
GROUP_WIDTH = D_MODEL // N_GROUPS
GROUP_HEADS = GROUP_WIDTH // HEAD_DIM
GLA_HEADS = GROUP_HEADS
GLA_DK = HEAD_DIM // 2
GLA_DV = HEAD_DIM
GLA_RANK = 16
GLA_TAU = 16.0
RET_HEADS = GROUP_HEADS
RET_DK = HEAD_DIM
RET_DV = HEAD_DIM
LA_CHUNK = 64
NSA_HEADS = GROUP_HEADS
NSA_DH = HEAD_DIM
CMP_LEN = 32
CMP_STRIDE = 16
CMP_HIDDEN = 2 * NSA_DH
SEL_BLOCK = 64
N_SEL = 8
WINDOW = 512
MOBA_HEADS = GROUP_HEADS
MOBA_BLOCK = 256
MOBA_TOPK = 3
D_FF = 2816
Q_ROWS = 128
ALPHA = (2 * DEPTH) ** 0.25
BETA = (8 * DEPTH) ** -0.25
LN_EPS = 1e-5
NEG = -1e30
BIG = 1e30
TINY = 1e-30

PROJ_SIZES = (
    GLA_HEADS * GLA_DK, GLA_HEADS * GLA_DK, GLA_HEADS * GLA_DV, GLA_RANK, GLA_HEADS * GLA_DV,
    RET_HEADS * RET_DK, RET_HEADS * RET_DK, RET_HEADS * RET_DV, RET_HEADS * RET_DV,
    NSA_HEADS * NSA_DH, 6 * NSA_DH, NSA_HEADS * 3,
    MOBA_HEADS * HEAD_DIM, MOBA_HEADS * HEAD_DIM, MOBA_HEADS * HEAD_DIM,
)
D_IN = sum(PROJ_SIZES)

kernel_name = 'hybrid_gla_retnet_nsa_moba_decode_step'


def layer_norm(x, g, b):
    xf = x.astype(jnp.float32)
    mu = xf.mean(-1, keepdims=True)
    var = jnp.mean(jnp.square(xf - mu), -1, keepdims=True)
    return ((xf - mu) * lax.rsqrt(var + LN_EPS) * g + b).astype(x.dtype)


def head_norm(o, g):
    of = o.astype(jnp.float32)
    mu = of.mean(-1, keepdims=True)
    var = jnp.mean(jnp.square(of - mu), -1, keepdims=True)
    y = ((of - mu) * lax.rsqrt(var + LN_EPS)).reshape(o.shape[0], o.shape[1], -1)
    return (y * g).astype(o.dtype)


def swiglu(x, w_up, w_down):
    u, gt = jnp.split(x @ w_up, 2, axis=-1)
    return (jax.nn.silu(gt) * u) @ w_down


def masked_softmax(logits, mask):
    lf = jnp.where(mask, logits.astype(jnp.float32), NEG)
    e = jnp.exp(lf - lf.max(-1, keepdims=True)) * mask
    return e / jnp.maximum(e.sum(-1, keepdims=True), TINY)


def alibi_slopes():
    n = NSA_HEADS + MOBA_HEADS
    return jnp.power(2.0, -8.0 * jnp.arange(1, n + 1, dtype=jnp.float32) / n)


def query_block(batch, seq):
    return math.gcd(seq, max(1, Q_ROWS // batch))


def gated_linear_attention(q, k, v, log_a, s0):
    B, S, H, DK = q.shape
    DV = v.shape[-1]
    c = math.gcd(S, LA_CHUNK)
    n = S // c

    def chunks(t):
        return jnp.transpose(t.astype(jnp.float32).reshape(B, n, c, H, t.shape[-1]), (1, 0, 3, 2, 4))

    causal = jnp.tril(jnp.ones((c, c), bool))[:, :, None]

    def step(state, inp):
        qi, ki, vi, gi = inp
        b = jnp.cumsum(gi, axis=2)
        o_inter = jnp.einsum('bhtd,bhde->bhte', qi * jnp.exp(b), state)
        rel = jnp.where(causal, b[:, :, :, None, :] - b[:, :, None, :, :], -jnp.inf)
        scores = jnp.einsum('bhtd,bhsd,bhtsd->bhts', qi, ki, jnp.exp(rel))
        o = o_inter + jnp.einsum('bhts,bhse->bhte', scores, vi)
        b_end = b[:, :, -1:, :]
        state = (jnp.exp(b_end[:, :, 0, :, None]) * state
                 + jnp.einsum('bhsd,bhse->bhde', ki * jnp.exp(b_end - b), vi))
        return state, o

    s_final, oc = lax.scan(step, s0.astype(jnp.float32), (chunks(q), chunks(k), chunks(v), chunks(log_a)))
    o = jnp.transpose(oc, (1, 0, 3, 2, 4)).reshape(B, S, H, DV)
    return o.astype(v.dtype), s_final.astype(s0.dtype)


def cmp_to_sel_matrix(n_cmp, n_sel):
    ratio, span = SEL_BLOCK // CMP_STRIDE, CMP_LEN // CMP_STRIDE
    j, m, n = np.meshgrid(np.arange(n_sel), np.arange(ratio), np.arange(span), indexing='ij')
    i = ratio * j + m - n
    ok = (i >= 0) & (i < n_cmp)
    mat = np.zeros((n_cmp, n_sel), np.float32)
    np.add.at(mat, (i[ok], j[ok]), 1.0)
    return jnp.asarray(mat)


def nsa_mixer(q, gates, kv_all, win_all, q0, w0, cmp_w1, cmp_w2, cmp_pos):
    B, S, H, DH = q.shape
    T = kv_all.shape[1]
    scale = DH ** -0.5
    sl = alibi_slopes()[0::2][None, :, None, None]
    n_cmp = (T - CMP_LEN) // CMP_STRIDE + 1
    idx = np.arange(n_cmp)[:, None] * CMP_STRIDE + np.arange(CMP_LEN)[None, :]
    blocks = kv_all[:, idx, :2] + jnp.transpose(cmp_pos, (1, 0, 2))
    flat = jnp.transpose(blocks, (0, 1, 3, 2, 4)).reshape(B, n_cmp, 2, CMP_LEN * DH)
    hid = jax.nn.gelu(jnp.einsum('bnce,ceh->bnch', flat, cmp_w1))
    kv_cmp = jnp.einsum('bnch,chd->bncd', hid, cmp_w2)
    k_cmp, v_cmp = kv_cmp[:, :, 0], kv_cmp[:, :, 1]
    cmp_end = jnp.asarray(idx[:, -1], jnp.int32)
    n_sel = -(-T // SEL_BLOCK)
    sel_kv = jnp.pad(kv_all[:, :, 2:], ((0, 0), (0, n_sel * SEL_BLOCK - T), (0, 0), (0, 0)))
    sel_kv = sel_kv.reshape(B, n_sel, SEL_BLOCK, 2, DH)
    imp_map = cmp_to_sel_matrix(n_cmp, n_sel)
    n_top = min(N_SEL, n_sel)
    win_pad = jnp.pad(win_all, ((0, 0), (WINDOW, 0), (0, 0), (0, 0)))
    qb = query_block(B, S)
    bi = jnp.arange(B)[:, None, None]

    def block(i):
        qs = i * qb
        qi = lax.dynamic_slice_in_dim(q, qs, qb, axis=1)
        gi = lax.dynamic_slice_in_dim(gates, qs, qb, axis=1)
        pos = q0 + qs + jnp.arange(qb, dtype=jnp.int32)
        dist_c = pos[:, None] - cmp_end[None, :]
        lc = jnp.einsum('bqhd,bnd->bhqn', qi, k_cmp) * scale - sl * dist_c.astype(jnp.float32)
        pc = masked_softmax(lc, dist_c >= 0)
        o_cmp = jnp.einsum('bhqn,bnd->bqhd', pc.astype(q.dtype), v_cmp)
        imp = jnp.einsum('bhqn,nj->bqj', pc, imp_map)
        cur = pos // SEL_BLOCK
        blk = jnp.arange(n_sel)[None, :]
        forced = (blk == 0) | (blk == cur[:, None]) | (blk == cur[:, None] - 1)
        imp = jnp.where(blk <= cur[:, None], jnp.where(forced, BIG, imp), NEG)
        _, sel = lax.top_k(imp, n_top)
        kv_s = sel_kv[bi, sel]
        ks = kv_s[..., 0, :].reshape(B, qb, n_top * SEL_BLOCK, DH)
        vs = kv_s[..., 1, :].reshape(B, qb, n_top * SEL_BLOCK, DH)
        spos = sel[..., None] * SEL_BLOCK + jnp.arange(SEL_BLOCK, dtype=jnp.int32)
        dist_s = (pos[None, :, None, None] - spos).reshape(B, 1, qb, -1)
        ls = jnp.einsum('bqhd,bqkd->bhqk', qi, ks) * scale - sl * dist_s.astype(jnp.float32)
        ps = masked_softmax(ls, dist_s >= 0)
        o_sel = jnp.einsum('bhqk,bqkd->bqhd', ps.astype(q.dtype), vs)
        kv_w = lax.dynamic_slice_in_dim(win_pad, q0 + qs - w0, WINDOW + qb, axis=1)
        wpos = q0 + qs - WINDOW + jnp.arange(WINDOW + qb, dtype=jnp.int32)
        dist_w = pos[:, None] - wpos[None, :]
        mw = (dist_w >= 0) & (dist_w < WINDOW) & (wpos[None, :] >= w0)
        lw = jnp.einsum('bqhd,bkd->bhqk', qi, kv_w[:, :, 0]) * scale - sl * dist_w.astype(jnp.float32)
        pw = masked_softmax(lw, mw)
        o_win = jnp.einsum('bhqk,bkd->bqhd', pw.astype(q.dtype), kv_w[:, :, 1])
        return gi[..., 0:1] * o_cmp + gi[..., 1:2] * o_sel + gi[..., 2:3] * o_win

    out = lax.map(block, jnp.arange(S // qb))
    return jnp.transpose(out, (1, 0, 2, 3, 4)).reshape(B, S, H * DH)


def moba_mixer(q, kv_all, q0):
    B, S, H, DH = q.shape
    T = kv_all.shape[1]
    scale = DH ** -0.5
    sl = alibi_slopes()[1::2][None, :, None, None]
    nb = -(-T // MOBA_BLOCK)
    qb = query_block(B, S)
    kv_pad = jnp.pad(kv_all, ((0, 0), (0, nb * MOBA_BLOCK + qb - T), (0, 0), (0, 0), (0, 0)))
    kv_blk = jnp.transpose(kv_pad[:, :nb * MOBA_BLOCK].reshape(B, nb, MOBA_BLOCK, 2, H, DH), (0, 4, 1, 2, 3, 5))
    k_mean = kv_blk[..., 0, :].astype(jnp.float32).mean(axis=3)
    n_top = min(MOBA_TOPK, nb)
    bi = jnp.arange(B)[:, None, None, None]
    hi = jnp.arange(H)[None, :, None, None]

    def block(i):
        qs = i * qb
        qi = lax.dynamic_slice_in_dim(q, qs, qb, axis=1)
        pos = q0 + qs + jnp.arange(qb, dtype=jnp.int32)
        own = pos // MOBA_BLOCK
        gate = jnp.einsum('bqhd,bhnd->bhqn', qi.astype(jnp.float32), k_mean)
        fully_past = jnp.arange(nb)[None, :] < own[:, None]
        _, sel = lax.top_k(jnp.where(fully_past, gate, NEG), n_top)
        kv_s = kv_blk[bi, hi, sel]
        ks = kv_s[..., 0, :].reshape(B, H, qb, n_top * MOBA_BLOCK, DH)
        vs = kv_s[..., 1, :].reshape(B, H, qb, n_top * MOBA_BLOCK, DH)
        spos = sel[..., None] * MOBA_BLOCK + jnp.arange(MOBA_BLOCK, dtype=jnp.int32)
        dist_s = (pos[None, None, :, None, None] - spos).reshape(B, H, qb, -1)
        ms = jnp.broadcast_to((sel < own[None, None, :, None])[..., None], spos.shape).reshape(B, H, qb, -1)
        ls = jnp.einsum('bqhd,bhqkd->bhqk', qi, ks) * scale - sl * dist_s.astype(jnp.float32)
        start = ((q0 + qs) // MOBA_BLOCK) * MOBA_BLOCK
        kv_o = lax.dynamic_slice_in_dim(kv_pad, start, MOBA_BLOCK + qb, axis=1)
        opos = start + jnp.arange(MOBA_BLOCK + qb, dtype=jnp.int32)
        dist_o = pos[:, None] - opos[None, :]
        mo = (dist_o >= 0) & (opos[None, :] >= own[:, None] * MOBA_BLOCK)
        lo = jnp.einsum('bqhd,bkhd->bhqk', qi, kv_o[:, :, 0]) * scale - sl * dist_o.astype(jnp.float32)
        mask = jnp.concatenate([ms, jnp.broadcast_to(mo, (B, H, qb, mo.shape[-1]))], axis=-1)
        p = masked_softmax(jnp.concatenate([ls, lo], axis=-1), mask).astype(q.dtype)
        n_s = n_top * MOBA_BLOCK
        return (jnp.einsum('bhqk,bhqkd->bqhd', p[..., :n_s], vs)
                + jnp.einsum('bhqk,bkhd->bqhd', p[..., n_s:], kv_o[:, :, 1]))

    out = lax.map(block, jnp.arange(S // qb))
    return jnp.transpose(out, (1, 0, 2, 3, 4)).reshape(B, S, H * DH)


def token_mixers(h, q0, past, w_in, gla_w_a2, gla_b_a, gla_norm, ret_norm, cmp_w1, cmp_w2, cmp_pos, w_out):
    nsa_past, moba_past, win_past, gla_s0, ret_s0 = past
    B, S, _ = h.shape
    parts = jnp.split(h @ w_in, np.cumsum(PROJ_SIZES)[:-1].tolist(), axis=-1)
    gq, gk, gv, ga, gr, rq, rk, rv, rg, nq, nkv, ng, mq, mk, mv = parts

    def heads(t, n):
        return t.reshape(B, S, n, -1)

    log_a = jax.nn.log_sigmoid((ga @ gla_w_a2 + gla_b_a).astype(jnp.float32)) / GLA_TAU
    o, gla_new = gated_linear_attention(heads(gq, GLA_HEADS) * GLA_DK ** -0.5, heads(gk, GLA_HEADS),
                                        heads(gv, GLA_HEADS), heads(log_a, GLA_HEADS), gla_s0)
    o_gla = head_norm(o, gla_norm) * jax.nn.silu(gr)
    log_gamma = jnp.log1p(-jnp.power(2.0, -5.0 - jnp.arange(RET_HEADS, dtype=jnp.float32)))
    log_g = jnp.broadcast_to(log_gamma[:, None], (B, S, RET_HEADS, RET_DK))
    o, ret_new = gated_linear_attention(heads(rq, RET_HEADS), heads(rk, RET_HEADS) * RET_DK ** -0.5,
                                        heads(rv, RET_HEADS), log_g, ret_s0)
    o_ret = head_norm(o, ret_norm) * jax.nn.silu(rg)
    nkv = nkv.reshape(B, S, 6, NSA_DH)
    nsa_rows = nkv[:, :, :4]
    win_all = jnp.concatenate([win_past, nkv[:, :, 4:]], axis=1)
    w0 = q0 - win_past.shape[1]
    o_nsa = nsa_mixer(heads(nq, NSA_HEADS), jax.nn.sigmoid(heads(ng, NSA_HEADS)),
                      jnp.concatenate([nsa_past, nsa_rows], axis=1), win_all, q0, w0, cmp_w1, cmp_w2, cmp_pos)
    keep = win_past.shape[1] if win_past.shape[1] > 0 else min(WINDOW, S)
    win_new = win_all[:, win_all.shape[1] - keep:]
    moba_rows = jnp.stack([heads(mk, MOBA_HEADS), heads(mv, MOBA_HEADS)], axis=2)
    o_moba = moba_mixer(heads(mq, MOBA_HEADS), jnp.concatenate([moba_past, moba_rows], axis=1), q0)
    y = jnp.concatenate([o_gla.astype(h.dtype), o_ret.astype(h.dtype), o_nsa.astype(h.dtype),
                         o_moba.astype(h.dtype)], axis=-1) @ w_out
    return y, (nsa_rows, moba_rows, win_new, gla_new, ret_new)


def decoder_layer(x, q0, past, w_in, gla_w_a2, gla_b_a, gla_norm, ret_norm, cmp_w1, cmp_w2, cmp_pos,
                  w_out, ffn_w_up, ffn_w_down, ln_g, ln_b):
    x = layer_norm(ALPHA * x + 0.5 * swiglu(x, ffn_w_up[0], ffn_w_down[0]), ln_g[0], ln_b[0])
    mix, new_state = token_mixers(x, q0, past, w_in, gla_w_a2, gla_b_a, gla_norm, ret_norm,
                                  cmp_w1, cmp_w2, cmp_pos, w_out)
    x = layer_norm(ALPHA * x + mix, ln_g[1], ln_b[1])
    x = layer_norm(ALPHA * x + 0.5 * swiglu(x, ffn_w_up[1], ffn_w_down[1]), ln_g[2], ln_b[2])
    return x, new_state


def setup_inputs(seed: int = 0) -> dict:
    key = jax.random.key(seed)
    ks = jax.random.split(key, 21)
    n_pages = PAST_LEN // PAGE_SIZE
    n_used = DEC_BATCH * n_pages
    n_phys = n_used + -(-n_used // 4)
    w_buf = min(WINDOW, PAST_LEN)

    def nrm(k, shape, s=1.0):
        return s * jax.random.normal(k, shape, jnp.float32)

    return {
        'x_prompt': nrm(ks[0], (BATCH, SEQ, D_MODEL)),
        'x_sample': nrm(ks[1], (DEC_BATCH, DEC_SEQ, D_MODEL)),
        'cache_nsa': nrm(ks[2], (n_phys, DEPTH, PAGE_SIZE, 4, NSA_DH)),
        'cache_moba': nrm(ks[3], (n_phys, DEPTH, PAGE_SIZE, 2, MOBA_HEADS, HEAD_DIM)),
        'cache_nsa_win': nrm(ks[4], (DEPTH, DEC_BATCH, w_buf, 2, NSA_DH)),
        'state_gla': nrm(ks[5], (DEPTH, DEC_BATCH, GLA_HEADS, GLA_DK, GLA_DV), 0.5),
        'state_ret': nrm(ks[6], (DEPTH, DEC_BATCH, RET_HEADS, RET_DK, RET_DV), 0.5),
        'page_table': jax.random.permutation(ks[7], n_phys)[:n_used].astype(jnp.int32).reshape(DEC_BATCH, n_pages),
        'w_in': nrm(ks[8], (DEPTH, D_MODEL, D_IN), D_MODEL ** -0.5),
        'gla_w_a2': nrm(ks[9], (DEPTH, GLA_RANK, GLA_HEADS * GLA_DK), GLA_RANK ** -0.5),
        'gla_b_a': nrm(ks[10], (DEPTH, GLA_HEADS * GLA_DK), 0.1),
        'gla_norm': 1.0 + nrm(ks[11], (DEPTH, GROUP_WIDTH), 0.02),
        'ret_norm': 1.0 + nrm(ks[12], (DEPTH, GROUP_WIDTH), 0.02),
        'nsa_cmp_w1': nrm(ks[13], (DEPTH, 2, CMP_LEN * NSA_DH, CMP_HIDDEN), (CMP_LEN * NSA_DH) ** -0.5),
        'nsa_cmp_w2': nrm(ks[14], (DEPTH, 2, CMP_HIDDEN, NSA_DH), CMP_HIDDEN ** -0.5),
        'nsa_cmp_pos': nrm(ks[15], (DEPTH, 2, CMP_LEN, NSA_DH), 0.1),
        'w_out': nrm(ks[16], (DEPTH, D_MODEL, D_MODEL), BETA * D_MODEL ** -0.5),
        'ffn_w_up': nrm(ks[17], (DEPTH, 2, D_MODEL, 2 * D_FF), D_MODEL ** -0.5),
        'ffn_w_down': nrm(ks[18], (DEPTH, 2, D_FF, D_MODEL), BETA * D_FF ** -0.5),
        'ln_g': 1.0 + nrm(ks[19], (DEPTH, 3, D_MODEL), 0.02),
        'ln_b': nrm(ks[20], (DEPTH, 3, D_MODEL), 0.02),
    }


def reference(x_prompt, x_sample, cache_nsa, cache_moba, cache_nsa_win, state_gla, state_ret, page_table,
              w_in, gla_w_a2, gla_b_a, gla_norm, ret_norm, nsa_cmp_w1, nsa_cmp_w2, nsa_cmp_pos,
              w_out, ffn_w_up, ffn_w_down, ln_g, ln_b):
    dt = x_prompt.dtype
    bp = x_prompt.shape[0]
    bs = x_sample.shape[0]
    past_len = page_table.shape[1] * PAGE_SIZE
    yp, ys = x_prompt, x_sample
    nsa_p, moba_p, win_p, gla_p, ret_p = [], [], [], [], []
    nsa_s, moba_s, win_s, gla_s, ret_s = [], [], [], [], []
    for l in range(DEPTH):
        lw = (w_in[l], gla_w_a2[l], gla_b_a[l], gla_norm[l], ret_norm[l], nsa_cmp_w1[l], nsa_cmp_w2[l],
              nsa_cmp_pos[l], w_out[l], ffn_w_up[l], ffn_w_down[l], ln_g[l], ln_b[l])
        past_p = (jnp.zeros((bp, 0, 4, NSA_DH), dt),
                  jnp.zeros((bp, 0, 2, MOBA_HEADS, HEAD_DIM), dt),
                  jnp.zeros((bp, 0, 2, NSA_DH), dt),
                  jnp.zeros((bp, GLA_HEADS, GLA_DK, GLA_DV), dt),
                  jnp.zeros((bp, RET_HEADS, RET_DK, RET_DV), dt))
        yp, st = decoder_layer(yp, 0, past_p, *lw)
        nsa_p.append(st[0]); moba_p.append(st[1]); win_p.append(st[2]); gla_p.append(st[3]); ret_p.append(st[4])
        past_s = (cache_nsa[page_table, l].reshape(bs, past_len, 4, NSA_DH),
                  cache_moba[page_table, l].reshape(bs, past_len, 2, MOBA_HEADS, HEAD_DIM),
                  cache_nsa_win[l], state_gla[l], state_ret[l])
        ys, st = decoder_layer(ys, past_len, past_s, *lw)
        nsa_s.append(st[0]); moba_s.append(st[1]); win_s.append(st[2]); gla_s.append(st[3]); ret_s.append(st[4])
    nsa_rows_p = jnp.stack(nsa_p, axis=1)
    moba_rows_p = jnp.stack(moba_p, axis=1)
    nsa_win_p = jnp.stack(win_p, axis=0)
    gla_state_p = jnp.stack(gla_p, axis=0)
    ret_state_p = jnp.stack(ret_p, axis=0)
    nsa_rows_s = jnp.stack(nsa_s, axis=1)
    moba_rows_s = jnp.stack(moba_s, axis=1)
    nsa_win_s = jnp.stack(win_s, axis=0)
    gla_state_s = jnp.stack(gla_s, axis=0)
    ret_state_s = jnp.stack(ret_s, axis=0)
    return (yp, ys, nsa_rows_p, moba_rows_p, nsa_win_p, gla_state_p, ret_state_p,
            nsa_rows_s, moba_rows_s, nsa_win_s, gla_state_s, ret_state_s)
```

```python
import functools
import math

import numpy as np
import jax
import jax.numpy as jnp
from jax import lax
from jax.experimental import pallas as pl
from jax.experimental.pallas import tpu as pltpu

F32 = jnp.float32
BF16 = jnp.bfloat16
HI = lax.Precision.HIGHEST

D_MODEL = 1024
DEPTH = 4
PAGE_SIZE = 128
HEAD_DIM = 64
N_HEADS = 4
GROUP_WIDTH = 256
GLA_DK = 32
GLA_RANK = 16
GLA_TAU = 16.0
CMP_LEN = 32
CMP_STRIDE = 16
CMP_HIDDEN = 128
SEL_BLOCK = 64
N_SEL = 8
WINDOW = 512
MOBA_BLOCK = 256
MOBA_TOPK = 3
D_FF = 2816
ALPHA = (2 * DEPTH) ** 0.25
LN_EPS = 1e-5
NEG = -1e30
BIG = 1e30
TINY = 1e-30

LANES = 128
VMEM_LIMIT = 56 * 1024 * 1024

GLA_W = 896
RET_W = 1024
NSA_W = 768
MOBA_W = 768
PROJ_SPLITS = (GLA_W, RET_W, NSA_W, MOBA_W)


def _slopes():
    n = 2 * N_HEADS
    s = [2.0 ** (-8.0 * i / n) for i in range(1, n + 1)]
    return s[0::2], s[1::2]


NSA_SLOPES, MOBA_SLOPES = _slopes()


def _bdot(a, b):
    return jnp.dot(a.astype(BF16), b.astype(BF16), preferred_element_type=F32)


def _bdot_nt(a, b):
    return lax.dot_general(a.astype(BF16), b.astype(BF16), (((1,), (1,)), ((), ())),
                           preferred_element_type=F32)


def _bdot_tn(a, b):
    return lax.dot_general(a.astype(BF16), b.astype(BF16), (((0,), (0,)), ((), ())),
                           preferred_element_type=F32)


def _hdot(a, b):
    return jnp.dot(a, b, precision=HI, preferred_element_type=F32)


def _hdot_nt(a, b):
    return lax.dot_general(a, b, (((1,), (1,)), ((), ())), precision=HI, preferred_element_type=F32)


def _hdot_tn(a, b):
    return lax.dot_general(a, b, (((0,), (0,)), ((), ())), precision=HI, preferred_element_type=F32)


def _layer_norm(y, g, b):
    mu = jnp.mean(y, axis=-1, keepdims=True)
    d = y - mu
    var = jnp.mean(d * d, axis=-1, keepdims=True)
    return d * lax.rsqrt(var + LN_EPS) * g + b


def _cparams(sem):
    return pltpu.CompilerParams(dimension_semantics=sem, vmem_limit_bytes=VMEM_LIMIT)


def _ffn_ln_body(x_ref, wu_ref, wg_ref, wd_ref, g_ref, b_ref, o_ref, acc_ref):
    j = pl.program_id(1)

    @pl.when(j == 0)
    def _():
        acc_ref[...] = jnp.zeros_like(acc_ref)

    xb = x_ref[...].astype(BF16)
    u = jnp.dot(xb, wu_ref[...], preferred_element_type=F32)
    gt = jnp.dot(xb, wg_ref[...], preferred_element_type=F32)
    a = (gt * jax.nn.sigmoid(gt) * u).astype(BF16)
    acc_ref[...] += jnp.dot(a, wd_ref[...], preferred_element_type=F32)

    @pl.when(j == pl.num_programs(1) - 1)
    def _():
        y = ALPHA * x_ref[...] + 0.5 * acc_ref[...]
        o_ref[...] = _layer_norm(y, g_ref[...], b_ref[...])


def ffn_ln(x, w_up, w_down, g, b, *, tm=512, tf=1408):
    m, d = x.shape
    f = w_down.shape[0]
    tm = min(tm, m)
    nf = f // tf
    return pl.pallas_call(
        _ffn_ln_body,
        grid=(m // tm, nf),
        in_specs=[
            pl.BlockSpec((tm, d), lambda i, j: (i, 0)),
            pl.BlockSpec((d, tf), lambda i, j: (0, j)),
            pl.BlockSpec((d, tf), lambda i, j: (0, j + nf)),
            pl.BlockSpec((tf, d), lambda i, j: (j, 0)),
            pl.BlockSpec((1, d), lambda i, j: (0, 0)),
            pl.BlockSpec((1, d), lambda i, j: (0, 0)),
        ],
        out_specs=pl.BlockSpec((tm, d), lambda i, j: (i, 0)),
        out_shape=jax.ShapeDtypeStruct((m, d), F32),
        scratch_shapes=[pltpu.VMEM((tm, d), F32)],
        compiler_params=_cparams(("parallel", "arbitrary")),
        name="ffn_ln",
    )(x, w_up, w_up, w_down, g, b)


def _proj_body(x_ref, w_ref, *o_refs):
    xb = x_ref[...].astype(BF16)
    off = 0
    for o_ref in o_refs:
        wdt = o_ref.shape[1]
        o_ref[...] = jnp.dot(xb, w_ref[:, off:off + wdt], preferred_element_type=F32)
        off += wdt


def proj(x, w, *, tm=512):
    m, d = x.shape
    n = w.shape[1]
    tm = min(tm, m)
    return pl.pallas_call(
        _proj_body,
        grid=(m // tm,),
        in_specs=[pl.BlockSpec((tm, d), lambda i: (i, 0)),
                  pl.BlockSpec((d, n), lambda i: (0, 0))],
        out_specs=[pl.BlockSpec((tm, wdt), lambda i: (i, 0)) for wdt in PROJ_SPLITS],
        out_shape=[jax.ShapeDtypeStruct((m, wdt), F32) for wdt in PROJ_SPLITS],
        compiler_params=_cparams(("parallel",)),
        name="proj",
    )(x, w)


def _out_ln_body(x_ref, o0_ref, o1_ref, o2_ref, o3_ref, w_ref, g_ref, b_ref, y_ref):
    mix = None
    for gi, o_ref in enumerate((o0_ref, o1_ref, o2_ref, o3_ref)):
        part = jnp.dot(o_ref[...].astype(BF16), w_ref[gi * GROUP_WIDTH:(gi + 1) * GROUP_WIDTH, :],
                       preferred_element_type=F32)
        mix = part if mix is None else mix + part
    y_ref[...] = _layer_norm(ALPHA * x_ref[...] + mix, g_ref[...], b_ref[...])


def out_ln(x, outs, w_out, g, b, *, tm=512):
    m, d = x.shape
    tm = min(tm, m)
    row = lambda i: (i, 0)
    return pl.pallas_call(
        _out_ln_body,
        grid=(m // tm,),
        in_specs=[pl.BlockSpec((tm, d), row)]
        + [pl.BlockSpec((tm, GROUP_WIDTH), row) for _ in range(4)]
        + [pl.BlockSpec((d, d), lambda i: (0, 0)),
           pl.BlockSpec((1, d), lambda i: (0, 0)),
           pl.BlockSpec((1, d), lambda i: (0, 0))],
        out_specs=pl.BlockSpec((tm, d), row),
        out_shape=jax.ShapeDtypeStruct((m, d), F32),
        compiler_params=_cparams(("parallel",)),
        name="out_ln",
    )(x, *outs, w_out, g, b)


def _head_group_norm(o, jn, gain):
    mu = _hdot(o, jn)
    d = o - mu
    var = _hdot(d * d, jn)
    return d * lax.rsqrt(var + LN_EPS) * gain


def _lin_prompt_body(x_ref, wa_ref, ba_ref, la_ref, gain_ref, tri_ref, jm_ref, jn_ref,
                     o_ref, st_ref, s_scr, *, kind, chunk, kd, dk):
    c = chunk
    seq = x_ref.shape[0]
    s_scr[...] = jnp.zeros_like(s_scr)
    tri = tri_ref[...]
    jm = jm_ref[...]
    jn = jn_ref[...]
    gain = gain_ref[...]
    ones_c = jnp.ones((c, GROUP_WIDTH), F32)
    si = lax.broadcasted_iota(jnp.int32, (c, c, kd), 0)
    ti = lax.broadcasted_iota(jnp.int32, (c, c, kd), 1)
    causal = ti >= si

    def step(i, carry):
        r0 = pl.multiple_of(i * c, c)
        rows = pl.ds(r0, c)
        if kind == "gla":
            q = x_ref[rows, 0:128] * (GLA_DK ** -0.5)
            k = x_ref[rows, 128:256]
            v = x_ref[rows, 256:512]
            gate = x_ref[rows, 512:768]
            pre = _hdot(x_ref[rows, 768:896], wa_ref[...]) + ba_ref[...]
            la = jax.nn.log_sigmoid(pre) / GLA_TAU
        else:
            q = x_ref[rows, 0:256]
            k = x_ref[rows, 256:512] * (HEAD_DIM ** -0.5)
            v = x_ref[rows, 512:768]
            gate = x_ref[rows, 768:1024]
            la = jnp.broadcast_to(la_ref[...], (c, kd))
        bt = _hdot(tri, la)
        dlt = jnp.minimum(bt[None, :, :] - bt[:, None, :], 0.0)
        w = jnp.where(causal, q[None, :, :] * k[:, None, :] * jnp.exp(dlt), 0.0)
        z = _bdot(w.reshape(c * c, kd), jm).reshape(c, c, GROUP_WIDTH)
        o = jnp.sum(z * v[:, None, :], axis=0)
        s_old = s_scr[...]
        o = o + _bdot(q * jnp.exp(bt), s_old)
        bend = bt[c - 1:c, :]
        kt = k * jnp.exp(bend - bt)
        upd = jnp.where(jm > 0, _bdot_tn(kt, v), 0.0)
        dec = jnp.exp(_hdot_tn(la, ones_c))
        s_scr[...] = dec * s_old + upd
        y = _head_group_norm(o, jn, gain) * (gate * jax.nn.sigmoid(gate))
        o_ref[rows, :] = y
        return carry

    lax.fori_loop(0, seq // c, step, 0)
    s_fin = s_scr[...]
    for h in range(N_HEADS):
        st_ref[0, h] = s_fin[h * dk:(h + 1) * dk, h * HEAD_DIM:(h + 1) * HEAD_DIM]


def _blockdiag(kd, dk):
    r = np.arange(kd)[:, None] // dk
    cidx = np.arange(GROUP_WIDTH)[None, :] // HEAD_DIM
    return (r == cidx).astype(np.float32)


def lin_prompt(x, batch, seq, kind, wa, ba, la_row, gain, *, chunk=32):
    kd, dk = (128, GLA_DK) if kind == "gla" else (256, HEAD_DIM)
    width = x.shape[1]
    tri = jnp.asarray(np.tril(np.ones((chunk, chunk), np.float32)))
    jm = jnp.asarray(_blockdiag(kd, dk))
    jn = jnp.asarray(_blockdiag(GROUP_WIDTH, HEAD_DIM) / HEAD_DIM)
    full = lambda *shape: pl.BlockSpec(shape, lambda b: (0,) * len(shape))
    return pl.pallas_call(
        functools.partial(_lin_prompt_body, kind=kind, chunk=chunk, kd=kd, dk=dk),
        grid=(batch,),
        in_specs=[pl.BlockSpec((seq, width), lambda b: (b, 0)),
                  full(128, 128), full(1, 128), full(1, kd), full(1, GROUP_WIDTH),
                  full(chunk, chunk), full(kd, GROUP_WIDTH), full(GROUP_WIDTH, GROUP_WIDTH)],
        out_specs=[pl.BlockSpec((seq, GROUP_WIDTH), lambda b: (b, 0)),
                   pl.BlockSpec((1, N_HEADS, dk, HEAD_DIM), lambda b: (b, 0, 0, 0))],
        out_shape=[jax.ShapeDtypeStruct((batch * seq, GROUP_WIDTH), F32),
                   jax.ShapeDtypeStruct((batch, N_HEADS, dk, HEAD_DIM), F32)],
        scratch_shapes=[pltpu.VMEM((kd, GROUP_WIDTH), F32)],
        compiler_params=_cparams(("parallel",)),
        name="lin_prompt_" + kind,
    )(x, wa, ba, la_row, gain, tri, jm, jn)


def _lin_decode_body(x_ref, s0_ref, wa_ref, ba_ref, la_ref, gain_ref, o_ref, s1_ref, a_scr, o_scr,
                     *, kind, kd, dk):
    steps = x_ref.shape[0]
    if kind == "gla":
        qo, ko, vo, go = 0, 128, 256, 512
        for t in range(steps):
            pre = _hdot(wa_ref[...], x_ref[t, 768:896, :]) + ba_ref[...]
            a_scr[t] = jnp.exp(jax.nn.log_sigmoid(pre) / GLA_TAU)
        qscale, kscale = GLA_DK ** -0.5, 1.0
    else:
        qo, ko, vo, go = 0, 256, 512, 768
        for t in range(steps):
            a_scr[t] = jnp.exp(jnp.broadcast_to(la_ref[...], a_scr.shape[1:]))
        qscale, kscale = 1.0, HEAD_DIM ** -0.5
    o_scr[...] = jnp.zeros_like(o_scr)

    def body(j8, carry):
        r0 = pl.multiple_of(j8 * 8, 8)
        h = r0 // dk
        vrows = pl.ds(pl.multiple_of(vo + h * HEAD_DIM, HEAD_DIM), HEAD_DIM)
        orows = pl.ds(pl.multiple_of(h * HEAD_DIM, HEAD_DIM), HEAD_DIM)
        a8 = [a_scr[t, pl.ds(r0, 8), :] for t in range(steps)]
        q8 = [x_ref[t, pl.ds(qo + r0, 8), :] * qscale for t in range(steps)]
        k8 = [x_ref[t, pl.ds(ko + r0, 8), :] * kscale for t in range(steps)]
        for jj in range(8):
            sj = s0_ref[r0 + jj]
            for t in range(steps):
                vt = x_ref[t, vrows, :]
                sj = a8[t][jj:jj + 1, :] * sj + k8[t][jj:jj + 1, :] * vt
                o_scr[t, orows, :] += q8[t][jj:jj + 1, :] * sj
            s1_ref[r0 + jj] = sj
        return carry

    lax.fori_loop(0, kd // 8, body, 0)
    for t in range(steps):
        for h in range(N_HEADS):
            blk = slice(h * HEAD_DIM, (h + 1) * HEAD_DIM)
            o = o_scr[t, blk, :]
            mu = jnp.mean(o, axis=0, keepdims=True)
            d = o - mu
            var = jnp.mean(d * d, axis=0, keepdims=True)
            gate = x_ref[t, go + h * HEAD_DIM:go + (h + 1) * HEAD_DIM, :]
            o_ref[t, blk, :] = d * lax.rsqrt(var + LN_EPS) * gain_ref[blk, :] * (gate * jax.nn.sigmoid(gate))


def lin_decode(xt, s0, kind, wa_t, ba_col, la_col, gain_col):
    kd, dk = (128, GLA_DK) if kind == "gla" else (256, HEAD_DIM)
    steps, _, nb = xt.shape
    return pl.pallas_call(
        functools.partial(_lin_decode_body, kind=kind, kd=kd, dk=dk),
        out_shape=[jax.ShapeDtypeStruct((steps, GROUP_WIDTH, nb), F32),
                   jax.ShapeDtypeStruct((kd, HEAD_DIM, nb), F32)],
        scratch_shapes=[pltpu.VMEM((steps, kd, nb), F32), pltpu.VMEM((steps, GROUP_WIDTH, nb), F32)],
        compiler_params=pltpu.CompilerParams(vmem_limit_bytes=VMEM_LIMIT),
        name="lin_decode_" + kind,
    )(xt, s0, wa_t, ba_col, la_col, gain_col)


def _cmp_mlp(xs_scr, pos_ref, w1_ref, w2_ref, n_rows):
    hid = jnp.zeros((n_rows, 2 * CMP_HIDDEN), F32)
    for r in range(CMP_LEN):
        xr = xs_scr[pl.ds(r, n_rows, stride=CMP_STRIDE), :] + pos_ref[r:r + 1, :]
        hid = hid + _bdot(xr, w1_ref[r])
    return _bdot(jax.nn.gelu(hid), w2_ref[...])


def _rank_select(vals, idx, n_rows, n_top, axis):
    cnt = jnp.zeros(vals.shape, F32)
    for j in range(n_rows):
        vj = vals[j:j + 1, :] if axis == 0 else vals[:, j:j + 1]
        before = (vj > vals) | ((vj == vals) & (j < idx))
        cnt = cnt + before.astype(F32)
    return cnt < n_top


TQ = 256
TK = 256


def _flash_t(q_h, kv_ref, koff, voff, kt_lo, kt_hi, slope, mask_fn, tq):
    def body(kt, carry):
        m, l, acc = carry
        rows = pl.ds(pl.multiple_of(kt * TK, TK), TK)
        k = kv_ref[rows, koff:koff + HEAD_DIM]
        v = kv_ref[rows, voff:voff + HEAD_DIM]
        kpos = kt * TK + lax.broadcasted_iota(jnp.int32, (TK, tq), 0)
        mask = mask_fn(kt, kpos)
        s = _bdot_nt(k, q_h) + slope * kpos.astype(F32)
        s = jnp.where(mask, s, NEG)
        m_new = jnp.maximum(m, jnp.max(s, axis=0, keepdims=True))
        p = jnp.where(mask, jnp.exp(s - m_new), 0.0)
        corr = jnp.exp(m - m_new)
        l = corr * l + jnp.sum(p, axis=0, keepdims=True)
        acc = corr * acc + _bdot_tn(v, p)
        return m_new, l, acc

    init = (jnp.full((1, tq), NEG, F32), jnp.zeros((1, tq), F32), jnp.zeros((HEAD_DIM, tq), F32))
    _, l, acc = lax.fori_loop(kt_lo, kt_hi, body, init)
    return acc / jnp.maximum(l, TINY)


def _nsa_prompt_body(q_ref, kv_ref, w1_ref, pos_ref, w2_ref, imp_ref, o_ref,
                     xs_scr, kvc_scr, sel_scr, *, seq):
    qi = pl.program_id(1)
    tq = q_ref.shape[0]
    ncp = seq // CMP_STRIDE
    n_cmp = ncp - 1
    n_sel = seq // SEL_BLOCK

    @pl.when(qi == 0)
    def _():
        xs_scr[0:seq, :] = kv_ref[:, 256:384]
        xs_scr[seq:seq + CMP_STRIDE, :] = jnp.zeros((CMP_STRIDE, LANES), F32)
        kvc_scr[...] = _cmp_mlp(xs_scr, pos_ref, w1_ref, w2_ref, ncp)

    qpos = qi * tq + lax.broadcasted_iota(jnp.int32, (1, tq), 1)
    gates_t = jax.nn.sigmoid(q_ref[:, 640:768]).T
    q_heads = [q_ref[:, h * HEAD_DIM:(h + 1) * HEAD_DIM] * (HEAD_DIM ** -0.5) for h in range(N_HEADS)]

    k_cmp = kvc_scr[:, 0:HEAD_DIM]
    v_cmp = kvc_scr[:, HEAD_DIM:2 * HEAD_DIM]
    n_io = lax.broadcasted_iota(jnp.int32, (ncp, tq), 0)
    cend = n_io * CMP_STRIDE + (CMP_LEN - 1)
    cmask = (cend <= qpos) & (n_io < n_cmp)
    cmaskf = cmask.astype(F32)
    cendf = cend.astype(F32)
    o_cmp = []
    pc_sum = jnp.zeros((ncp, tq), F32)
    for h in range(N_HEADS):
        s = _bdot_nt(k_cmp, q_heads[h]) + NSA_SLOPES[h] * cendf
        s = jnp.where(cmask, s, NEG)
        e = jnp.exp(s - jnp.max(s, axis=0, keepdims=True)) * cmaskf
        pc = e / jnp.maximum(jnp.sum(e, axis=0, keepdims=True), TINY)
        o_cmp.append(_bdot_tn(v_cmp, pc))
        pc_sum = pc_sum + pc

    imp = _hdot(imp_ref[...], pc_sum)
    blk = lax.broadcasted_iota(jnp.int32, (n_sel, tq), 0)
    cur = qpos // SEL_BLOCK
    forced = (blk == 0) | (blk == cur) | (blk == cur - 1)
    vals = jnp.where(blk <= cur, jnp.where(forced, BIG, imp), NEG)
    sel_scr[...] = _rank_select(vals, blk, n_sel, min(N_SEL, n_sel), 0).astype(F32)

    def sel_mask(kt, kpos):
        b0 = kt * (TK // SEL_BLOCK)
        parts = [jnp.broadcast_to(sel_scr[pl.ds(b0 + i, 1), :], (SEL_BLOCK, tq))
                 for i in range(TK // SEL_BLOCK)]
        return (jnp.concatenate(parts, axis=0) > 0.0) & (kpos <= qpos)

    def win_mask(kt, kpos):
        dist = qpos - kpos
        return (dist >= 0) & (dist < WINDOW)

    w_lo = jnp.maximum(qi - WINDOW // TK, 0)
    outs = []
    for h in range(N_HEADS):
        o_sel = _flash_t(q_heads[h], kv_ref, 384, 448, 0, qi + 1, NSA_SLOPES[h], sel_mask, tq)
        o_win = _flash_t(q_heads[h], kv_ref, 512, 576, w_lo, qi + 1, NSA_SLOPES[h], win_mask, tq)
        g = gates_t[3 * h:3 * h + 3, :]
        outs.append(g[0:1, :] * o_cmp[h] + g[1:2, :] * o_sel + g[2:3, :] * o_win)
    o_ref[...] = jnp.concatenate(outs, axis=0).T


def _cmp_to_sel(n_cmp, n_sel):
    ratio, span = SEL_BLOCK // CMP_STRIDE, CMP_LEN // CMP_STRIDE
    j, m, n = np.meshgrid(np.arange(n_sel), np.arange(ratio), np.arange(span), indexing="ij")
    i = ratio * j + m - n
    ok = (i >= 0) & (i < n_cmp)
    mat = np.zeros((n_cmp, n_sel), np.float32)
    np.add.at(mat, (i[ok], j[ok]), 1.0)
    return mat


def nsa_prompt(x, batch, seq, w1, pos, w2):
    nq = seq // TQ
    ncp = seq // CMP_STRIDE
    n_sel = seq // SEL_BLOCK
    imp_t = np.zeros((n_sel, ncp), np.float32)
    imp_t[:, :ncp - 1] = _cmp_to_sel(ncp - 1, n_sel).T
    full = lambda *shape: pl.BlockSpec(shape, lambda b, i: (0,) * len(shape))
    return pl.pallas_call(
        functools.partial(_nsa_prompt_body, seq=seq),
        grid=(batch, nq),
        in_specs=[pl.BlockSpec((TQ, NSA_W), lambda b, i: (b * nq + i, 0)),
                  pl.BlockSpec((seq, NSA_W), lambda b, i: (b, 0)),
                  full(CMP_LEN, LANES, 2 * CMP_HIDDEN), full(CMP_LEN, LANES),
                  full(2 * CMP_HIDDEN, LANES), full(n_sel, ncp)],
        out_specs=pl.BlockSpec((TQ, GROUP_WIDTH), lambda b, i: (b * nq + i, 0)),
        out_shape=jax.ShapeDtypeStruct((batch * seq, GROUP_WIDTH), F32),
        scratch_shapes=[pltpu.VMEM((seq + CMP_STRIDE, LANES), F32),
                        pltpu.VMEM((ncp, LANES), F32),
                        pltpu.VMEM((n_sel, TQ), F32)],
        compiler_params=_cparams(("parallel", "arbitrary")),
        name="nsa_prompt",
    )(x, x, w1, pos, w2, jnp.asarray(imp_t))


def _moba_prompt_body(q_ref, kv_ref, o_ref, km_scr, sel_scr, *, seq):
    qi = pl.program_id(1)
    tq = q_ref.shape[0]
    nb = seq // MOBA_BLOCK

    @pl.when(qi == 0)
    def _():
        for n in range(nb):
            km_scr[n:n + 1, :] = jnp.mean(kv_ref[n * MOBA_BLOCK:(n + 1) * MOBA_BLOCK, 256:512],
                                          axis=0, keepdims=True)

    qpos = qi * tq + lax.broadcasted_iota(jnp.int32, (1, tq), 1)
    own = qpos // MOBA_BLOCK
    blk = lax.broadcasted_iota(jnp.int32, (nb, tq), 0)
    past = blk < own
    outs = []
    for h in range(N_HEADS):
        hs = slice(h * HEAD_DIM, (h + 1) * HEAD_DIM)
        q_raw = q_ref[:, hs]
        gate = _hdot_nt(km_scr[:, hs], q_raw)
        vals = jnp.where(past, gate, NEG)
        chosen = _rank_select(vals, blk, nb, min(MOBA_TOPK, nb), 0) & past
        sel_scr[...] = chosen.astype(F32)

        def mask_fn(kt, kpos):
            picked = jnp.broadcast_to(sel_scr[pl.ds(kt, 1), :], (TK, tq)) > 0.0
            in_own = (kpos // MOBA_BLOCK) == own
            return picked | (in_own & (kpos <= qpos))

        outs.append(_flash_t(q_raw * (HEAD_DIM ** -0.5), kv_ref, 256 + h * HEAD_DIM, 512 + h * HEAD_DIM,
                             0, qi + 1, MOBA_SLOPES[h], mask_fn, tq))
    o_ref[...] = jnp.concatenate(outs, axis=0).T


def moba_prompt(x, batch, seq):
    nq = seq // TQ
    nb = seq // MOBA_BLOCK
    return pl.pallas_call(
        functools.partial(_moba_prompt_body, seq=seq),
        grid=(batch, nq),
        in_specs=[pl.BlockSpec((TQ, MOBA_W), lambda b, i: (b * nq + i, 0)),
                  pl.BlockSpec((seq, MOBA_W), lambda b, i: (b, 0))],
        out_specs=pl.BlockSpec((TQ, GROUP_WIDTH), lambda b, i: (b * nq + i, 0)),
        out_shape=jax.ShapeDtypeStruct((batch * seq, GROUP_WIDTH), F32),
        scratch_shapes=[pltpu.VMEM((nb, GROUP_WIDTH), F32), pltpu.VMEM((nb, TQ), F32)],
        compiler_params=_cparams(("parallel", "arbitrary")),
        name="moba_prompt",
    )(x, x)


def _softmax_rows(parts, masks):
    masked = [jnp.where(mk, s, NEG) for s, mk in zip(parts, masks)]
    m = masked[0].max(axis=1, keepdims=True)
    for s in masked[1:]:
        m = jnp.maximum(m, s.max(axis=1, keepdims=True))
    es = [jnp.where(mk, jnp.exp(s - m), 0.0) for s, mk in zip(masked, masks)]
    den = es[0].sum(axis=1, keepdims=True)
    for e in es[1:]:
        den = den + e.sum(axis=1, keepdims=True)
    inv = 1.0 / jnp.maximum(den, TINY)
    return [e * inv for e in es]


def _stack_heads(x, off):
    return jnp.concatenate([x[:, off + h * HEAD_DIM:off + (h + 1) * HEAD_DIM] for h in range(N_HEADS)], axis=0)


def _pad_rows(x, n):
    return jnp.concatenate([x, jnp.zeros((n - x.shape[0], x.shape[1]), x.dtype)], axis=0)


def _nsa_decode_body(*refs, n_pages, past_len):
    x_ref = refs[1]
    page_refs = refs[2:2 + n_pages]
    (win_ref, w1_ref, pos_ref, w2_ref, imp_ref, esel_ref, o_ref, xs_scr) = refs[2 + n_pages:]
    steps = x_ref.shape[0]
    rows = N_HEADS * steps
    ncp = past_len // CMP_STRIDE
    n_cmp = ncp - 1
    x = x_ref[...]
    tpos = past_len + lax.broadcasted_iota(jnp.int32, (steps, 1), 0)
    qpos = jnp.concatenate([tpos] * N_HEADS, axis=0)
    slope = jnp.concatenate([jnp.full((steps, 1), s, F32) for s in NSA_SLOPES], axis=0)
    qs = _stack_heads(x, 0) * (HEAD_DIM ** -0.5)
    gates = jax.nn.sigmoid(x[:, 640:768])
    gcol = [jnp.concatenate([gates[:, 3 * h + j:3 * h + j + 1] for h in range(N_HEADS)], axis=0)
            for j in range(3)]
    new_io = lax.broadcasted_iota(jnp.int32, (rows, LANES), 1)
    new_pos = past_len + new_io
    new_ok = (new_io < steps) & (new_pos <= qpos)

    for p in range(n_pages):
        xs_scr[p * PAGE_SIZE:(p + 1) * PAGE_SIZE, :] = page_refs[p][0, 0, 0:2].reshape(LANES, PAGE_SIZE).T
    xs_scr[past_len:past_len + CMP_STRIDE, :] = jnp.zeros((CMP_STRIDE, LANES), F32)
    kvc = _cmp_mlp(xs_scr, pos_ref, w1_ref, w2_ref, ncp)
    k_cmp, v_cmp = kvc[:, 0:HEAD_DIM], kvc[:, HEAD_DIM:2 * HEAD_DIM]
    n_io = lax.broadcasted_iota(jnp.int32, (rows, ncp), 1)
    cend = n_io * CMP_STRIDE + (CMP_LEN - 1)
    (pc,) = _softmax_rows([_bdot_nt(qs, k_cmp) + slope * cend.astype(F32)],
                          [(cend <= qpos) & (n_io < n_cmp)])
    o_cmp = _bdot(pc, v_cmp)
    pc_sum = pc[0:steps]
    for h in range(1, N_HEADS):
        pc_sum = pc_sum + pc[h * steps:(h + 1) * steps]

    n_sel = -(-(past_len + steps) // SEL_BLOCK)
    imp = _hdot(pc_sum, imp_ref[...])
    blk = lax.broadcasted_iota(jnp.int32, (steps, LANES), 1)
    cur = tpos // SEL_BLOCK
    forced = (blk == 0) | (blk == cur) | (blk == cur - 1)
    vals = jnp.where((blk <= cur) & (blk < n_sel), jnp.where(forced, BIG, imp), NEG)
    chosen = _rank_select(vals, blk, n_sel, min(N_SEL, n_sel), 1).astype(F32)
    key_sel = _bdot(chosen, esel_ref[...])
    key_sel = jnp.concatenate([key_sel] * N_HEADS, axis=0) > 0.5

    s_past = jnp.concatenate([_bdot(qs, page_refs[p][0, 0, 2]) for p in range(n_pages)], axis=1)
    kpos = lax.broadcasted_iota(jnp.int32, (rows, past_len), 1)
    k_new = _pad_rows(x[:, 384:448], LANES)
    v_new = _pad_rows(x[:, 448:512], LANES)
    s_new = _bdot_nt(qs, k_new) + slope * new_pos.astype(F32)
    p_past, p_new = _softmax_rows([s_past + slope * kpos.astype(F32), s_new],
                                  [key_sel[:, 0:past_len] & (kpos <= qpos),
                                   key_sel[:, past_len:past_len + LANES] & new_ok])
    o_sel = _bdot(p_new, v_new)
    for p in range(n_pages):
        o_sel = o_sel + _bdot_nt(p_past[:, p * PAGE_SIZE:(p + 1) * PAGE_SIZE], page_refs[p][0, 0, 3])

    n_win = win_ref.shape[-1]
    wpos = (past_len - n_win) + lax.broadcasted_iota(jnp.int32, (rows, n_win), 1)
    kw_new = _pad_rows(x[:, 512:576], LANES)
    vw_new = _pad_rows(x[:, 576:640], LANES)
    s_w = _bdot(qs, win_ref[0, 0, 0]) + slope * wpos.astype(F32)
    s_wn = _bdot_nt(qs, kw_new) + slope * new_pos.astype(F32)
    dist = qpos - wpos
    p_w, p_wn = _softmax_rows([s_w, s_wn],
                              [(dist >= 0) & (dist < WINDOW), new_ok & (qpos - new_pos < WINDOW)])
    o_win = _bdot_nt(p_w, win_ref[0, 0, 1]) + _bdot(p_wn, vw_new)

    out = gcol[0] * o_cmp + gcol[1] * o_sel + gcol[2] * o_win
    o_ref[...] = jnp.concatenate([out[h * steps:(h + 1) * steps] for h in range(N_HEADS)], axis=1)


def nsa_decode(x, page_ids, cache_t, layer, win_t, w1, pos, w2, *, steps, past_len):
    nseq = x.shape[0] // steps
    n_pages = past_len // PAGE_SIZE
    ncp = past_len // CMP_STRIDE
    n_sel = -(-(past_len + steps) // SEL_BLOCK)
    imp = np.zeros((ncp, LANES), np.float32)
    imp[:ncp - 1, :n_sel] = _cmp_to_sel(ncp - 1, n_sel)
    kblk = np.concatenate([np.arange(past_len) // SEL_BLOCK,
                           (past_len + np.arange(LANES)) // SEL_BLOCK])
    esel = (np.arange(LANES)[:, None] == kblk[None, :]).astype(np.float32)
    n_win = win_t.shape[-1]

    def page_spec(p):
        return pl.BlockSpec((1, 1, 4, HEAD_DIM, PAGE_SIZE),
                            lambda b, pt, p=p: (pt[b * n_pages + p], layer, 0, 0, 0))

    full = lambda *shape: pl.BlockSpec(shape, lambda b, pt: (0,) * len(shape))
    return pl.pallas_call(
        functools.partial(_nsa_decode_body, n_pages=n_pages, past_len=past_len),
        grid_spec=pltpu.PrefetchScalarGridSpec(
            num_scalar_prefetch=1,
            grid=(nseq,),
            in_specs=[pl.BlockSpec((steps, NSA_W), lambda b, pt: (b, 0))]
            + [page_spec(p) for p in range(n_pages)]
            + [pl.BlockSpec((1, 1, 2, HEAD_DIM, n_win), lambda b, pt: (layer, b, 0, 0, 0)),
               full(CMP_LEN, LANES, 2 * CMP_HIDDEN), full(CMP_LEN, LANES),
               full(2 * CMP_HIDDEN, LANES), full(ncp, LANES), full(LANES, past_len + LANES)],
            out_specs=pl.BlockSpec((steps, GROUP_WIDTH), lambda b, pt: (b, 0)),
            scratch_shapes=[pltpu.VMEM((past_len + CMP_STRIDE, LANES), F32)],
        ),
        out_shape=jax.ShapeDtypeStruct((nseq * steps, GROUP_WIDTH), F32),
        compiler_params=_cparams(("parallel",)),
        name="nsa_decode",
    )(page_ids, x, *([cache_t] * n_pages), win_t, w1, pos, w2, jnp.asarray(imp), jnp.asarray(esel))


def _moba_decode_body(*refs, n_pages, past_len):
    x_ref = refs[1]
    page_refs = refs[2:2 + n_pages]
    emean_ref, eblk_ref, o_ref = refs[2 + n_pages:]
    steps = x_ref.shape[0]
    rows = N_HEADS * steps
    x = x_ref[...]
    tpos = past_len + lax.broadcasted_iota(jnp.int32, (steps, 1), 0)
    qpos = jnp.concatenate([tpos] * N_HEADS, axis=0)
    own = qpos // MOBA_BLOCK
    slope = jnp.concatenate([jnp.full((steps, 1), s, F32) for s in MOBA_SLOPES], axis=0)
    head_of_row = lax.broadcasted_iota(jnp.int32, (rows, GROUP_WIDTH), 0) // steps
    head_of_col = lax.broadcasted_iota(jnp.int32, (rows, GROUP_WIDTH), 1) // HEAD_DIM
    diag = head_of_row == head_of_col
    q_bd = jnp.where(diag, jnp.concatenate([x[:, 0:GROUP_WIDTH]] * N_HEADS, axis=0), 0.0)

    kts = [page_refs[p][0, 0, 0].reshape(GROUP_WIDTH, PAGE_SIZE) for p in range(n_pages)]
    kmean_t = _bdot(kts[0], emean_ref[0])
    for p in range(1, n_pages):
        kmean_t = kmean_t + _bdot(kts[p], emean_ref[p])
    gate = _hdot(q_bd, kmean_t)
    nb = -(-(past_len + steps) // MOBA_BLOCK)
    blk = lax.broadcasted_iota(jnp.int32, (rows, LANES), 1)
    past = (blk < own) & (blk < nb)
    vals = jnp.where(past, gate, NEG)
    chosen = (_rank_select(vals, blk, nb, min(MOBA_TOPK, nb), 1) & past).astype(F32)
    key_sel = _bdot(chosen, eblk_ref[...]) > 0.5

    q_sc = q_bd * (HEAD_DIM ** -0.5)
    s_past = jnp.concatenate([_bdot(q_sc, kts[p]) for p in range(n_pages)], axis=1)
    kpos = lax.broadcasted_iota(jnp.int32, (rows, past_len), 1)
    new_io = lax.broadcasted_iota(jnp.int32, (rows, LANES), 1)
    new_pos = past_len + new_io
    k_new = _pad_rows(x[:, 256:512], LANES)
    v_new = _pad_rows(x[:, 512:768], LANES)
    s_new = _bdot_nt(q_sc, k_new) + slope * new_pos.astype(F32)
    in_own_past = (kpos // MOBA_BLOCK) == own
    in_own_new = (new_pos // MOBA_BLOCK) == own
    p_past, p_new = _softmax_rows(
        [s_past + slope * kpos.astype(F32), s_new],
        [key_sel | in_own_past, (new_io < steps) & (new_pos <= qpos) & in_own_new])
    o_all = _bdot(p_new, v_new)
    for p in range(n_pages):
        vt = page_refs[p][0, 0, 1].reshape(GROUP_WIDTH, PAGE_SIZE)
        o_all = o_all + _bdot_nt(p_past[:, p * PAGE_SIZE:(p + 1) * PAGE_SIZE], vt)
    o_all = jnp.where(diag, o_all, 0.0)
    out = o_all[0:steps]
    for h in range(1, N_HEADS):
        out = out + o_all[h * steps:(h + 1) * steps]
    o_ref[...] = out


def moba_decode(x, page_ids, cache_t, layer, *, steps, past_len):
    nseq = x.shape[0] // steps
    n_pages = past_len // PAGE_SIZE
    pages_per_blk = MOBA_BLOCK // PAGE_SIZE
    emean = np.zeros((n_pages, PAGE_SIZE, LANES), np.float32)
    for p in range(n_pages):
        emean[p, :, p // pages_per_blk] = 1.0 / MOBA_BLOCK
    eblk = (np.arange(LANES)[:, None] == (np.arange(past_len) // MOBA_BLOCK)[None, :]).astype(np.float32)

    def page_spec(p):
        return pl.BlockSpec((1, 1, 2, N_HEADS, HEAD_DIM, PAGE_SIZE),
                            lambda b, pt, p=p: (pt[b * n_pages + p], layer, 0, 0, 0, 0))

    full = lambda *shape: pl.BlockSpec(shape, lambda b, pt: (0,) * len(shape))
    return pl.pallas_call(
        functools.partial(_moba_decode_body, n_pages=n_pages, past_len=past_len),
        grid_spec=pltpu.PrefetchScalarGridSpec(
            num_scalar_prefetch=1,
            grid=(nseq,),
            in_specs=[pl.BlockSpec((steps, MOBA_W), lambda b, pt: (b, 0))]
            + [page_spec(p) for p in range(n_pages)]
            + [full(n_pages, PAGE_SIZE, LANES), full(LANES, past_len)],
            out_specs=pl.BlockSpec((steps, GROUP_WIDTH), lambda b, pt: (b, 0)),
        ),
        out_shape=jax.ShapeDtypeStruct((nseq * steps, GROUP_WIDTH), F32),
        compiler_params=_cparams(("parallel",)),
        name="moba_decode",
    )(page_ids, x, *([cache_t] * n_pages), jnp.asarray(emean), jnp.asarray(eblk))


def _regroup_w_in(w):
    d = w.shape[0]
    z = lambda n: jnp.zeros((d, n), w.dtype)
    gq, gk, gv, ga, gr = w[:, 0:128], w[:, 128:256], w[:, 256:512], w[:, 512:528], w[:, 528:784]
    ret = w[:, 784:1808]
    nq, nkv, ng = w[:, 1808:2064], w[:, 2064:2448], w[:, 2448:2460]
    moba = w[:, 2460:3228]
    return jnp.concatenate([gq, gk, gv, gr, ga, z(112), ret, nq, nkv, ng, z(116), moba], axis=1).astype(BF16)


def _cmp_weights(w1, pos, w2):
    w1r = w1.reshape(2, CMP_LEN, HEAD_DIM, CMP_HIDDEN)
    zero = jnp.zeros((CMP_LEN, HEAD_DIM, CMP_HIDDEN), w1.dtype)
    w1bd = jnp.concatenate([jnp.concatenate([w1r[0], zero], axis=2),
                            jnp.concatenate([zero, w1r[1]], axis=2)], axis=1)
    zero2 = jnp.zeros((CMP_HIDDEN, HEAD_DIM), w2.dtype)
    w2bd = jnp.concatenate([jnp.concatenate([w2[0], zero2], axis=1),
                            jnp.concatenate([zero2, w2[1]], axis=1)], axis=0)
    posf = jnp.concatenate([pos[0], pos[1]], axis=1)
    return w1bd.astype(BF16), posf, w2bd.astype(BF16)


def kernel(x_prompt, x_sample, cache_nsa, cache_moba, cache_nsa_win, state_gla, state_ret, page_table, w_in, gla_w_a2, gla_b_a, gla_norm, ret_norm, nsa_cmp_w1, nsa_cmp_w2, nsa_cmp_pos, w_out, ffn_w_up, ffn_w_down, ln_g, ln_b):
    bp, sp, d = x_prompt.shape
    bs, ss, _ = x_sample.shape
    past_len = page_table.shape[1] * PAGE_SIZE
    n_win = cache_nsa_win.shape[2]

    nsa_t = jnp.transpose(cache_nsa, (0, 1, 3, 4, 2))
    moba_t = jnp.transpose(cache_moba, (0, 1, 3, 4, 5, 2))
    win_t = jnp.transpose(cache_nsa_win, (0, 1, 3, 4, 2))
    gla_t = jnp.transpose(state_gla, (0, 2, 3, 4, 1)).reshape(DEPTH, N_HEADS * GLA_DK, HEAD_DIM, bs)
    ret_t = jnp.transpose(state_ret, (0, 2, 3, 4, 1)).reshape(DEPTH, N_HEADS * HEAD_DIM, HEAD_DIM, bs)
    page_ids = page_table.reshape(-1)

    log_gamma = np.log1p(-np.power(2.0, -5.0 - np.arange(N_HEADS, dtype=np.float64))).astype(np.float32)
    la_ret = jnp.asarray(np.repeat(log_gamma, HEAD_DIM))
    zeros128 = jnp.zeros((1, 128), F32)

    xp = x_prompt.reshape(bp * sp, d)
    xs = x_sample.reshape(bs * ss, d)
    outs = {k: [] for k in ("nsa_p", "moba_p", "win_p", "gla_p", "ret_p",
                            "nsa_s", "moba_s", "win_s", "gla_s", "ret_s")}
    for l in range(DEPTH):
        wu = ffn_w_up[l].astype(BF16)
        wd = ffn_w_down[l].astype(BF16)
        wi = _regroup_w_in(w_in[l])
        wo = w_out[l].astype(BF16)
        g = ln_g[l].reshape(3, 1, d)
        b = ln_b[l].reshape(3, 1, d)
        wa = jnp.zeros((128, 128), F32).at[0:GLA_RANK, :].set(gla_w_a2[l])
        ba = gla_b_a[l].reshape(1, 128)
        gn = gla_norm[l].reshape(1, GROUP_WIDTH)
        rn = ret_norm[l].reshape(1, GROUP_WIDTH)
        w1bd, posf, w2bd = _cmp_weights(nsa_cmp_w1[l], nsa_cmp_pos[l], nsa_cmp_w2[l])

        xp = ffn_ln(xp, wu[0], wd[0], g[0], b[0])
        pg, pr, pn, pm = proj(xp, wi)
        o_gla, st_gla = lin_prompt(pg, bp, sp, "gla", wa, ba, zeros128, gn)
        o_ret, st_ret = lin_prompt(pr, bp, sp, "ret", wa, ba, la_ret.reshape(1, 256), rn)
        o_nsa = nsa_prompt(pn, bp, sp, w1bd, posf, w2bd)
        o_moba = moba_prompt(pm, bp, sp)
        xp = out_ln(xp, (o_gla, o_ret, o_nsa, o_moba), wo, g[1], b[1])
        xp = ffn_ln(xp, wu[1], wd[1], g[2], b[2])
        pn3 = pn.reshape(bp, sp, NSA_W)
        outs["nsa_p"].append(pn3[:, :, 256:512].reshape(bp, sp, 4, HEAD_DIM))
        outs["moba_p"].append(pm.reshape(bp, sp, MOBA_W)[:, :, 256:768].reshape(bp, sp, 2, N_HEADS, HEAD_DIM))
        keep = min(WINDOW, sp)
        outs["win_p"].append(pn3[:, sp - keep:, 512:640].reshape(bp, keep, 2, HEAD_DIM))
        outs["gla_p"].append(st_gla)
        outs["ret_p"].append(st_ret)

        xs = ffn_ln(xs, wu[0], wd[0], g[0], b[0])
        pg, pr, pn, pm = proj(xs, wi)
        to_lanes = lambda a: jnp.transpose(a.reshape(bs, ss, a.shape[1]), (1, 2, 0))
        og_t, sg_t = lin_decode(to_lanes(pg), gla_t[l], "gla", wa.T,
                                ba.reshape(128, 1), jnp.zeros((128, 1), F32), gn.reshape(GROUP_WIDTH, 1))
        or_t, sr_t = lin_decode(to_lanes(pr), ret_t[l], "ret", jnp.zeros((128, 128), F32),
                                jnp.zeros((128, 1), F32), la_ret.reshape(256, 1), rn.reshape(GROUP_WIDTH, 1))
        from_lanes = lambda a: jnp.transpose(a, (2, 0, 1)).reshape(bs * ss, GROUP_WIDTH)
        o_nsa = nsa_decode(pn, page_ids, nsa_t, l, win_t, w1bd, posf, w2bd, steps=ss, past_len=past_len)
        o_moba = moba_decode(pm, page_ids, moba_t, l, steps=ss, past_len=past_len)
        xs = out_ln(xs, (from_lanes(og_t), from_lanes(or_t), o_nsa, o_moba), wo, g[1], b[1])
        xs = ffn_ln(xs, wu[1], wd[1], g[2], b[2])
        pn3 = pn.reshape(bs, ss, NSA_W)
        outs["nsa_s"].append(pn3[:, :, 256:512].reshape(bs, ss, 4, HEAD_DIM))
        outs["moba_s"].append(pm.reshape(bs, ss, MOBA_W)[:, :, 256:768].reshape(bs, ss, 2, N_HEADS, HEAD_DIM))
        new_win = pn3[:, :, 512:640].reshape(bs, ss, 2, HEAD_DIM)
        outs["win_s"].append(jnp.concatenate([cache_nsa_win[l], new_win], axis=1)[:, ss:] if n_win > 0
                             else new_win[:, ss - min(WINDOW, ss):])
        outs["gla_s"].append(jnp.transpose(sg_t.reshape(N_HEADS, GLA_DK, HEAD_DIM, bs), (3, 0, 1, 2)))
        outs["ret_s"].append(jnp.transpose(sr_t.reshape(N_HEADS, HEAD_DIM, HEAD_DIM, bs), (3, 0, 1, 2)))

    return (xp.reshape(bp, sp, d), xs.reshape(bs, ss, d),
            jnp.stack(outs["nsa_p"], axis=1), jnp.stack(outs["moba_p"], axis=1),
            jnp.stack(outs["win_p"], axis=0), jnp.stack(outs["gla_p"], axis=0), jnp.stack(outs["ret_p"], axis=0),
            jnp.stack(outs["nsa_s"], axis=1), jnp.stack(outs["moba_s"], axis=1),
            jnp.stack(outs["win_s"], axis=0), jnp.stack(outs["gla_s"], axis=0), jnp.stack(outs["ret_s"], axis=0))
```

```python
import functools
import math

import numpy as np
import jax
import jax.numpy as jnp
from jax import lax
from jax.experimental import pallas as pl
from jax.experimental.pallas import tpu as pltpu

F32 = jnp.float32
BF16 = jnp.bfloat16
HI = lax.Precision.HIGHEST

D_MODEL = 1024
DEPTH = 4
PAGE_SIZE = 128
HEAD_DIM = 64
N_HEADS = 4
GROUP_WIDTH = 256
GLA_DK = 32
GLA_RANK = 16
GLA_TAU = 16.0
CMP_LEN = 32
CMP_STRIDE = 16
CMP_HIDDEN = 128
SEL_BLOCK = 64
N_SEL = 8
WINDOW = 512
MOBA_BLOCK = 256
MOBA_TOPK = 3
D_FF = 2816
ALPHA = (2 * DEPTH) ** 0.25
LN_EPS = 1e-5
NEG = -1e30
BIG = 1e30
TINY = 1e-30

LANES = 128
VMEM_LIMIT = 56 * 1024 * 1024

GLA_W = 896
RET_W = 1024
NSA_W = 768
MOBA_W = 768
PROJ_SPLITS = (GLA_W, RET_W, NSA_W, MOBA_W)


def _slopes():
    n = 2 * N_HEADS
    s = [2.0 ** (-8.0 * i / n) for i in range(1, n + 1)]
    return s[0::2], s[1::2]


NSA_SLOPES, MOBA_SLOPES = _slopes()


def _bdot(a, b):
    return jnp.dot(a.astype(BF16), b.astype(BF16), preferred_element_type=F32)


def _bdot_nt(a, b):
    return lax.dot_general(a.astype(BF16), b.astype(BF16), (((1,), (1,)), ((), ())),
                           preferred_element_type=F32)


def _bdot_tn(a, b):
    return lax.dot_general(a.astype(BF16), b.astype(BF16), (((0,), (0,)), ((), ())),
                           preferred_element_type=F32)


def _hdot(a, b):
    return jnp.dot(a, b, precision=HI, preferred_element_type=F32)


def _hdot_nt(a, b):
    return lax.dot_general(a, b, (((1,), (1,)), ((), ())), precision=HI, preferred_element_type=F32)


def _hdot_tn(a, b):
    return lax.dot_general(a, b, (((0,), (0,)), ((), ())), precision=HI, preferred_element_type=F32)


def _layer_norm(y, g, b):
    mu = jnp.mean(y, axis=-1, keepdims=True)
    d = y - mu
    var = jnp.mean(d * d, axis=-1, keepdims=True)
    return d * lax.rsqrt(var + LN_EPS) * g + b


def _cparams(sem):
    return pltpu.CompilerParams(dimension_semantics=sem, vmem_limit_bytes=VMEM_LIMIT)


def _ffn_ln_body(x_ref, wu_ref, wg_ref, wd_ref, g_ref, b_ref, o_ref, acc_ref):
    j = pl.program_id(1)

    @pl.when(j == 0)
    def _():
        acc_ref[...] = jnp.zeros_like(acc_ref)

    xb = x_ref[...].astype(BF16)
    u = jnp.dot(xb, wu_ref[...], preferred_element_type=F32)
    gt = jnp.dot(xb, wg_ref[...], preferred_element_type=F32)
    a = (gt * jax.nn.sigmoid(gt) * u).astype(BF16)
    acc_ref[...] += jnp.dot(a, wd_ref[...], preferred_element_type=F32)

    @pl.when(j == pl.num_programs(1) - 1)
    def _():
        y = ALPHA * x_ref[...] + 0.5 * acc_ref[...]
        o_ref[...] = _layer_norm(y, g_ref[...], b_ref[...])


def ffn_ln(x, w_up, w_down, g, b, *, tm=512, tf=1408):
    m, d = x.shape
    f = w_down.shape[0]
    tm = min(tm, m)
    nf = f // tf
    return pl.pallas_call(
        _ffn_ln_body,
        grid=(m // tm, nf),
        in_specs=[
            pl.BlockSpec((tm, d), lambda i, j: (i, 0)),
            pl.BlockSpec((d, tf), lambda i, j: (0, j)),
            pl.BlockSpec((d, tf), lambda i, j: (0, j + nf)),
            pl.BlockSpec((tf, d), lambda i, j: (j, 0)),
            pl.BlockSpec((1, d), lambda i, j: (0, 0)),
            pl.BlockSpec((1, d), lambda i, j: (0, 0)),
        ],
        out_specs=pl.BlockSpec((tm, d), lambda i, j: (i, 0)),
        out_shape=jax.ShapeDtypeStruct((m, d), F32),
        scratch_shapes=[pltpu.VMEM((tm, d), F32)],
        compiler_params=_cparams(("parallel", "arbitrary")),
        name="ffn_ln",
    )(x, w_up, w_up, w_down, g, b)


def _proj_body(x_ref, w_ref, *o_refs):
    xb = x_ref[...].astype(BF16)
    off = 0
    for o_ref in o_refs:
        wdt = o_ref.shape[1]
        o_ref[...] = jnp.dot(xb, w_ref[:, off:off + wdt], preferred_element_type=F32)
        off += wdt


def proj(x, w, *, tm=512):
    m, d = x.shape
    n = w.shape[1]
    tm = min(tm, m)
    return pl.pallas_call(
        _proj_body,
        grid=(m // tm,),
        in_specs=[pl.BlockSpec((tm, d), lambda i: (i, 0)),
                  pl.BlockSpec((d, n), lambda i: (0, 0))],
        out_specs=[pl.BlockSpec((tm, wdt), lambda i: (i, 0)) for wdt in PROJ_SPLITS],
        out_shape=[jax.ShapeDtypeStruct((m, wdt), F32) for wdt in PROJ_SPLITS],
        compiler_params=_cparams(("parallel",)),
        name="proj",
    )(x, w)


def _out_ln_body(x_ref, o0_ref, o1_ref, o2_ref, o3_ref, w_ref, g_ref, b_ref, y_ref):
    mix = None
    for gi, o_ref in enumerate((o0_ref, o1_ref, o2_ref, o3_ref)):
        part = jnp.dot(o_ref[...].astype(BF16), w_ref[gi * GROUP_WIDTH:(gi + 1) * GROUP_WIDTH, :],
                       preferred_element_type=F32)
        mix = part if mix is None else mix + part
    y_ref[...] = _layer_norm(ALPHA * x_ref[...] + mix, g_ref[...], b_ref[...])


def out_ln(x, outs, w_out, g, b, *, tm=512):
    m, d = x.shape
    tm = min(tm, m)
    row = lambda i: (i, 0)
    return pl.pallas_call(
        _out_ln_body,
        grid=(m // tm,),
        in_specs=[pl.BlockSpec((tm, d), row)]
        + [pl.BlockSpec((tm, GROUP_WIDTH), row) for _ in range(4)]
        + [pl.BlockSpec((d, d), lambda i: (0, 0)),
           pl.BlockSpec((1, d), lambda i: (0, 0)),
           pl.BlockSpec((1, d), lambda i: (0, 0))],
        out_specs=pl.BlockSpec((tm, d), row),
        out_shape=jax.ShapeDtypeStruct((m, d), F32),
        compiler_params=_cparams(("parallel",)),
        name="out_ln",
    )(x, *outs, w_out, g, b)


def _head_group_norm(o, jn, gain):
    mu = _hdot(o, jn)
    d = o - mu
    var = _hdot(d * d, jn)
    return d * lax.rsqrt(var + LN_EPS) * gain


GLA_CHUNK = 16
GLA_ROWS = 128
RET_CHUNK = 128


def _gla_prompt_body(x_ref, wa_ref, ba_ref, gain_ref, tri_ref, tot_ref, emask_ref, jm_ref, jn_ref,
                     o_ref, st_ref, s_scr):
    c, r = GLA_CHUNK, GLA_ROWS
    g = r // c
    kd = 128
    seq = x_ref.shape[0]
    s_scr[...] = jnp.zeros_like(s_scr)
    jm = jm_ref[...]
    si = lax.broadcasted_iota(jnp.int32, (g, c, c, kd), 1)
    ti = lax.broadcasted_iota(jnp.int32, (g, c, c, kd), 2)
    causal = ti >= si

    def step(i, carry):
        rows = pl.ds(pl.multiple_of(i * r, r), r)
        q = x_ref[rows, 0:128] * (GLA_DK ** -0.5)
        k = x_ref[rows, 128:256]
        v = x_ref[rows, 256:512]
        gate = x_ref[rows, 512:768]
        la = jax.nn.log_sigmoid(_hdot(x_ref[rows, 768:896], wa_ref[...]) + ba_ref[...]) / GLA_TAU
        bt = _hdot(tri_ref[...], la)
        btot = _hdot(tot_ref[...], la)
        bt4 = bt.reshape(g, c, kd)
        q4 = q.reshape(g, c, kd)
        k4 = k.reshape(g, c, kd)
        dlt = jnp.minimum(bt4[:, None, :, :] - bt4[:, :, None, :], 0.0)
        w = jnp.where(causal, q4[:, None, :, :] * k4[:, :, None, :] * jnp.exp(dlt), 0.0)
        z = _bdot(w.reshape(g * c * c, kd), jm).reshape(g, c, c, GROUP_WIDTH)
        o = jnp.sum(z * v.reshape(g, c, GROUP_WIDTH)[:, :, None, :], axis=1).reshape(r, GROUP_WIDTH)
        qd = q * jnp.exp(bt)
        kdn = k * jnp.exp(btot - bt)
        edec = jnp.exp(_hdot_tn(la, emask_ref[...]))
        s_cur = s_scr[...]
        inter = []
        for j in range(g):
            rs = slice(j * c, (j + 1) * c)
            inter.append(_bdot(qd[rs], s_cur))
            upd = jnp.where(jm > 0, _bdot_tn(kdn[rs], v[rs]), 0.0)
            s_cur = jnp.broadcast_to(edec[:, j:j + 1], (kd, GROUP_WIDTH)) * s_cur + upd
        s_scr[...] = s_cur
        o = o + jnp.concatenate(inter, axis=0)
        o_ref[rows, :] = _head_group_norm(o, jn_ref[...], gain_ref[...]) * (gate * jax.nn.sigmoid(gate))
        return carry

    lax.fori_loop(0, seq // r, step, 0)
    s_fin = s_scr[...]
    for h in range(N_HEADS):
        st_ref[0, h] = s_fin[h * GLA_DK:(h + 1) * GLA_DK, h * HEAD_DIM:(h + 1) * HEAD_DIM]


def _ret_prompt_body(x_ref, gain_ref, dstack_ref, qdec_ref, kdec_ref, sdec_ref, hm_ref, jm_ref, jn_ref,
                     o_ref, st_ref, s_scr):
    c = RET_CHUNK
    seq = x_ref.shape[0]
    s_scr[...] = jnp.zeros_like(s_scr)

    def step(i, carry):
        rows = pl.ds(pl.multiple_of(i * c, c), c)
        q = x_ref[rows, 0:256]
        k = x_ref[rows, 256:512] * (HEAD_DIM ** -0.5)
        v = x_ref[rows, 512:768]
        gate = x_ref[rows, 768:1024]
        hm = hm_ref[...]
        qs = jnp.concatenate([q] * N_HEADS, axis=0) * hm
        sc = _bdot_nt(qs, k) * dstack_ref[...]
        of = _bdot(sc, v) * hm
        o = of[0:c]
        for h in range(1, N_HEADS):
            o = o + of[h * c:(h + 1) * c]
        s_old = s_scr[...]
        o = o + _bdot(q * qdec_ref[...], s_old)
        upd = jnp.where(jm_ref[...] > 0, _bdot_tn(k * kdec_ref[...], v), 0.0)
        s_scr[...] = sdec_ref[...] * s_old + upd
        o_ref[rows, :] = _head_group_norm(o, jn_ref[...], gain_ref[...]) * (gate * jax.nn.sigmoid(gate))
        return carry

    lax.fori_loop(0, seq // c, step, 0)
    s_fin = s_scr[...]
    for h in range(N_HEADS):
        st_ref[0, h] = s_fin[h * HEAD_DIM:(h + 1) * HEAD_DIM, h * HEAD_DIM:(h + 1) * HEAD_DIM]


def _blockdiag(kd, dk):
    r = np.arange(kd)[:, None] // dk
    cidx = np.arange(GROUP_WIDTH)[None, :] // HEAD_DIM
    return (r == cidx).astype(np.float32)


def _lin_prompt_call(body, name, x, batch, seq, kd, dk, consts):
    full = lambda a: pl.BlockSpec(a.shape, lambda b: (0,) * a.ndim)
    return pl.pallas_call(
        body,
        grid=(batch,),
        in_specs=[pl.BlockSpec((seq, x.shape[1]), lambda b: (b, 0))] + [full(a) for a in consts],
        out_specs=[pl.BlockSpec((seq, GROUP_WIDTH), lambda b: (b, 0)),
                   pl.BlockSpec((1, N_HEADS, dk, HEAD_DIM), lambda b: (b, 0, 0, 0))],
        out_shape=[jax.ShapeDtypeStruct((batch * seq, GROUP_WIDTH), F32),
                   jax.ShapeDtypeStruct((batch, N_HEADS, dk, HEAD_DIM), F32)],
        scratch_shapes=[pltpu.VMEM((kd, GROUP_WIDTH), F32)],
        compiler_params=_cparams(("parallel",)),
        name=name,
    )(x, *consts)


def gla_prompt(x, batch, seq, wa, ba, gain):
    r, c = GLA_ROWS, GLA_CHUNK
    same = (np.arange(r)[:, None] // c) == (np.arange(r)[None, :] // c)
    tri = (same & (np.arange(r)[:, None] >= np.arange(r)[None, :])).astype(np.float32)
    emask = ((np.arange(r)[:, None] // c) == np.arange(LANES)[None, :]).astype(np.float32)
    consts = [wa, ba, gain, jnp.asarray(tri), jnp.asarray(same.astype(np.float32)), jnp.asarray(emask),
              jnp.asarray(_blockdiag(128, GLA_DK)), jnp.asarray(_blockdiag(GROUP_WIDTH, HEAD_DIM) / HEAD_DIM)]
    return _lin_prompt_call(_gla_prompt_body, "gla_prompt", x, batch, seq, 128, GLA_DK, consts)


def ret_prompt(x, batch, seq, gain):
    c = RET_CHUNK
    log_gamma = np.log1p(-np.power(2.0, -5.0 - np.arange(N_HEADS, dtype=np.float64)))
    t = np.arange(c)
    diff = t[:, None] - t[None, :]
    dstack = np.concatenate([np.where(diff >= 0, np.exp(lg * np.maximum(diff, 0)), 0.0) for lg in log_gamma], axis=0)
    per_lane = np.repeat(log_gamma, HEAD_DIM)
    qdec = np.exp(per_lane[None, :] * (t[:, None] + 1))
    kdec = np.exp(per_lane[None, :] * (c - 1 - t[:, None]))
    sdec = np.broadcast_to(np.exp(per_lane * c)[:, None], (GROUP_WIDTH, GROUP_WIDTH))
    hm = np.concatenate([np.broadcast_to(np.arange(GROUP_WIDTH)[None, :] // HEAD_DIM == h, (c, GROUP_WIDTH))
                         for h in range(N_HEADS)], axis=0)
    as_f32 = lambda a: jnp.asarray(np.asarray(a, np.float32))
    consts = [gain, as_f32(dstack), as_f32(qdec), as_f32(kdec), as_f32(sdec), as_f32(hm),
              jnp.asarray(_blockdiag(GROUP_WIDTH, HEAD_DIM)),
              jnp.asarray(_blockdiag(GROUP_WIDTH, HEAD_DIM) / HEAD_DIM)]
    return _lin_prompt_call(_ret_prompt_body, "ret_prompt", x, batch, seq, GROUP_WIDTH, HEAD_DIM, consts)


def _lin_decode_body(x_ref, s0_ref, wa_ref, ba_ref, la_ref, gain_ref, o_ref, s1_ref, a_scr, o_scr,
                     *, kind, kd, dk):
    steps = x_ref.shape[0]
    if kind == "gla":
        qo, ko, vo, go = 0, 128, 256, 512
        for t in range(steps):
            pre = _hdot(wa_ref[...], x_ref[t, 768:896, :]) + ba_ref[...]
            a_scr[t] = jnp.exp(jax.nn.log_sigmoid(pre) / GLA_TAU)
        qscale, kscale = GLA_DK ** -0.5, 1.0
    else:
        qo, ko, vo, go = 0, 256, 512, 768
        for t in range(steps):
            a_scr[t] = jnp.exp(jnp.broadcast_to(la_ref[...], a_scr.shape[1:]))
        qscale, kscale = 1.0, HEAD_DIM ** -0.5
    o_scr[...] = jnp.zeros_like(o_scr)

    def body(j8, carry):
        r0 = pl.multiple_of(j8 * 8, 8)
        h = r0 // dk
        vrows = pl.ds(pl.multiple_of(vo + h * HEAD_DIM, HEAD_DIM), HEAD_DIM)
        orows = pl.ds(pl.multiple_of(h * HEAD_DIM, HEAD_DIM), HEAD_DIM)
        a8 = [a_scr[t, pl.ds(r0, 8), :] for t in range(steps)]
        q8 = [x_ref[t, pl.ds(qo + r0, 8), :] * qscale for t in range(steps)]
        k8 = [x_ref[t, pl.ds(ko + r0, 8), :] * kscale for t in range(steps)]
        for jj in range(8):
            sj = s0_ref[r0 + jj]
            for t in range(steps):
                vt = x_ref[t, vrows, :]
                sj = a8[t][jj:jj + 1, :] * sj + k8[t][jj:jj + 1, :] * vt
                o_scr[t, orows, :] += q8[t][jj:jj + 1, :] * sj
            s1_ref[r0 + jj] = sj
        return carry

    lax.fori_loop(0, kd // 8, body, 0)
    for t in range(steps):
        for h in range(N_HEADS):
            blk = slice(h * HEAD_DIM, (h + 1) * HEAD_DIM)
            o = o_scr[t, blk, :]
            mu = jnp.mean(o, axis=0, keepdims=True)
            d = o - mu
            var = jnp.mean(d * d, axis=0, keepdims=True)
            gate = x_ref[t, go + h * HEAD_DIM:go + (h + 1) * HEAD_DIM, :]
            o_ref[t, blk, :] = d * lax.rsqrt(var + LN_EPS) * gain_ref[blk, :] * (gate * jax.nn.sigmoid(gate))


def lin_decode(xt, s0, kind, wa_t, ba_col, la_col, gain_col):
    kd, dk = (128, GLA_DK) if kind == "gla" else (256, HEAD_DIM)
    steps, _, nb = xt.shape
    return pl.pallas_call(
        functools.partial(_lin_decode_body, kind=kind, kd=kd, dk=dk),
        out_shape=[jax.ShapeDtypeStruct((steps, GROUP_WIDTH, nb), F32),
                   jax.ShapeDtypeStruct((kd, HEAD_DIM, nb), F32)],
        scratch_shapes=[pltpu.VMEM((steps, kd, nb), F32), pltpu.VMEM((steps, GROUP_WIDTH, nb), F32)],
        compiler_params=pltpu.CompilerParams(vmem_limit_bytes=VMEM_LIMIT),
        name="lin_decode_" + kind,
    )(xt, s0, wa_t, ba_col, la_col, gain_col)


def _cmp_mlp(xs_scr, pos_ref, w1_ref, w2_ref, n_rows):
    hid = jnp.zeros((n_rows, 2 * CMP_HIDDEN), F32)
    for r in range(CMP_LEN):
        xr = xs_scr[pl.ds(r, n_rows, stride=CMP_STRIDE), :] + pos_ref[r:r + 1, :]
        hid = hid + _bdot(xr, w1_ref[r])
    return _bdot(jax.nn.gelu(hid), w2_ref[...])


def _rank_select(vals, idx, n_rows, n_top, axis):
    cnt = jnp.zeros(vals.shape, F32)
    for j in range(n_rows):
        vj = vals[j:j + 1, :] if axis == 0 else vals[:, j:j + 1]
        before = (vj > vals) | ((vj == vals) & (j < idx))
        cnt = cnt + before.astype(F32)
    return cnt < n_top


TQ = 256
TK = 256


POS_HI_LANE = HEAD_DIM
POS_LO_LANE = HEAD_DIM + 1


def _nt(a, b):
    return lax.dot_general(a, b, (((1,), (1,)), ((), ())), preferred_element_type=F32)


def _tn(a, b):
    return lax.dot_general(a, b, (((0,), (0,)), ((), ())), preferred_element_type=F32)


def _key_tile(x128, pos):
    lane = lax.broadcasted_iota(jnp.int32, x128.shape, 1)
    feat = jnp.where(lane == POS_HI_LANE, (pos // 64).astype(F32),
                     jnp.where(lane == POS_LO_LANE, (pos % 64).astype(F32), 0.0))
    return jnp.where(lane < HEAD_DIM, x128, feat).astype(BF16)


def _query_stack(q256, slopes):
    tq = q256.shape[0]
    lane = lax.broadcasted_iota(jnp.int32, (tq, LANES), 1)
    parts = []
    for h in range(N_HEADS):
        pair = q256[:, (h // 2) * LANES:(h // 2 + 1) * LANES]
        if h % 2:
            pair = pltpu.roll(pair, HEAD_DIM, 1)
        const = jnp.where(lane == POS_HI_LANE, 64.0 * slopes[h], jnp.where(lane == POS_LO_LANE, slopes[h], 0.0))
        parts.append(jnp.where(lane < HEAD_DIM, pair * (HEAD_DIM ** -0.5), const).astype(BF16))
    return jnp.concatenate(parts, axis=0)


def _tile_step(k_fn, v_fn, qst_scr, bias, m_scr, l_scr, acc_scr, first, tq, group):
    starts = range(0, N_HEADS * tq, group)
    s = jnp.concatenate([_nt(k_fn(c0 // tq), qst_scr[c0:c0 + group, :]) for c0 in starts], axis=1)
    if bias is not None:
        s = s + bias
    pv = lambda p: jnp.concatenate([_tn(v_fn(c0 // tq), p[:, c0:c0 + group]) for c0 in starts], axis=1)
    if first:
        m_new = jnp.max(s, axis=0, keepdims=True)
        p = jnp.exp(s - m_new)
        l_scr[...] = jnp.sum(p, axis=0, keepdims=True)
        acc_scr[...] = pv(p.astype(BF16))
    else:
        m_old = m_scr[...]
        m_new = jnp.maximum(m_old, jnp.max(s, axis=0, keepdims=True))
        p = jnp.exp(s - m_new)
        corr = jnp.exp(m_old - m_new)
        l_scr[...] = corr * l_scr[...] + jnp.sum(p, axis=0, keepdims=True)
        acc_scr[...] = corr * acc_scr[...] + pv(p.astype(BF16))
    m_scr[...] = m_new


def _tile4(x):
    return jnp.concatenate([x] * N_HEADS, axis=1)


def _tri_bias(tq, keep_lower):
    row = lax.broadcasted_iota(jnp.int32, (TK, tq), 0)
    col = lax.broadcasted_iota(jnp.int32, (TK, tq), 1)
    return _tile4(jnp.where((row <= col) if keep_lower else (row > col), 0.0, NEG))


def _nsa_prompt_body(q_ref, kv_ref, w1_ref, pos_ref, w2_ref, imp_ref, o_ref,
                     xs_scr, kc_scr, vc_scr, ks_scr, vs_scr, kw_scr, vw_scr, selb_scr,
                     qst_scr, ocmp_scr, m_scr, l_scr, acc_scr, *, seq):
    qi = pl.program_id(1)
    tq = q_ref.shape[0]
    ncp = seq // CMP_STRIDE
    n_cmp = ncp - 1
    n_sel = seq // SEL_BLOCK
    blocks_per_tile = TK // SEL_BLOCK

    @pl.when(qi == 0)
    def _():
        xs_scr[0:seq, :] = kv_ref[:, 256:384]
        xs_scr[seq:seq + CMP_STRIDE, :] = jnp.zeros((CMP_STRIDE, LANES), F32)
        kvc = _cmp_mlp(xs_scr, pos_ref, w1_ref, w2_ref, ncp)
        cend = lax.broadcasted_iota(jnp.int32, (ncp, LANES), 0) * CMP_STRIDE + (CMP_LEN - 1)
        kc_scr[...] = _key_tile(kvc, cend)
        vc_scr[...] = kvc[:, HEAD_DIM:2 * HEAD_DIM].astype(BF16)
        kpos = lax.broadcasted_iota(jnp.int32, (seq, LANES), 0)
        ks_scr[...] = _key_tile(kv_ref[:, 384:512], kpos)
        vs_scr[...] = kv_ref[:, 448:512].astype(BF16)
        kw_scr[...] = _key_tile(kv_ref[:, 512:640], kpos)
        vw_scr[...] = kv_ref[:, 576:640].astype(BF16)

    qpos = qi * tq + lax.broadcasted_iota(jnp.int32, (1, tq), 1)
    gates_t = jax.nn.sigmoid(q_ref[:, 640:768]).T
    qst_scr[...] = _query_stack(q_ref[:, 0:GROUP_WIDTH], NSA_SLOPES)

    n_io = lax.broadcasted_iota(jnp.int32, (ncp, tq), 0)
    cmask = _tile4(((n_io * CMP_STRIDE + (CMP_LEN - 1)) <= qpos) & (n_io < n_cmp))
    s = jnp.where(cmask, _nt(kc_scr[...], qst_scr[...]), NEG)
    e = jnp.where(cmask, jnp.exp(s - jnp.max(s, axis=0, keepdims=True)), 0.0)
    pc = e / jnp.maximum(jnp.sum(e, axis=0, keepdims=True), TINY)
    ocmp_scr[...] = _tn(vc_scr[...], pc.astype(BF16))
    pc_sum = pc[:, 0:tq]
    for h in range(1, N_HEADS):
        pc_sum = pc_sum + pc[:, h * tq:(h + 1) * tq]

    imp = _hdot(imp_ref[...], pc_sum)
    blk = lax.broadcasted_iota(jnp.int32, (n_sel, tq), 0)
    cur = qpos // SEL_BLOCK
    forced = (blk == 0) | (blk == cur) | (blk == cur - 1)
    vals = jnp.where(blk <= cur, jnp.where(forced, BIG, imp), NEG)
    chosen = _rank_select(vals, blk, n_sel, min(N_SEL, n_sel), 0)
    selb_scr[...] = jnp.where(chosen, 0.0, NEG)

    def sel_bias(kt):
        rows = [jnp.broadcast_to(selb_scr[pl.ds(kt * blocks_per_tile + i, 1), :], (SEL_BLOCK, tq))
                for i in range(blocks_per_tile)]
        return _tile4(jnp.concatenate(rows, axis=0))

    def tile_rows(kt):
        return pl.ds(pl.multiple_of(kt * TK, TK), TK)

    def step(k_scr, v_scr, rows, bias, first):
        _tile_step(lambda h: k_scr[rows, :], lambda h: v_scr[rows, :], qst_scr, bias,
                   m_scr, l_scr, acc_scr, first, tq, N_HEADS * tq)

    diag = tile_rows(qi)
    step(ks_scr, vs_scr, diag, sel_bias(qi) + _tri_bias(tq, True), True)

    def sel_body(kt, carry):
        step(ks_scr, vs_scr, tile_rows(kt), sel_bias(kt), False)
        return carry

    lax.fori_loop(0, qi, sel_body, 0)
    gate_row = lambda j: jnp.concatenate([gates_t[3 * h + j:3 * h + j + 1, :] for h in range(N_HEADS)], axis=1)
    out = gate_row(0) * ocmp_scr[...] + gate_row(1) * (acc_scr[...] / l_scr[...])

    step(kw_scr, vw_scr, diag, _tri_bias(tq, True), True)

    @pl.when(qi >= 1)
    def _():
        step(kw_scr, vw_scr, tile_rows(qi - 1), None, False)

    @pl.when(qi >= WINDOW // TK)
    def _():
        step(kw_scr, vw_scr, tile_rows(qi - WINDOW // TK), _tri_bias(tq, False), False)

    out = out + gate_row(2) * (acc_scr[...] / l_scr[...])
    o_ref[...] = jnp.concatenate([out[:, h * tq:(h + 1) * tq] for h in range(N_HEADS)], axis=0).T


def _cmp_to_sel(n_cmp, n_sel):
    ratio, span = SEL_BLOCK // CMP_STRIDE, CMP_LEN // CMP_STRIDE
    j, m, n = np.meshgrid(np.arange(n_sel), np.arange(ratio), np.arange(span), indexing="ij")
    i = ratio * j + m - n
    ok = (i >= 0) & (i < n_cmp)
    mat = np.zeros((n_cmp, n_sel), np.float32)
    np.add.at(mat, (i[ok], j[ok]), 1.0)
    return mat


def nsa_prompt(x, batch, seq, w1, pos, w2):
    nq = seq // TQ
    ncp = seq // CMP_STRIDE
    n_sel = seq // SEL_BLOCK
    imp_t = np.zeros((n_sel, ncp), np.float32)
    imp_t[:, :ncp - 1] = _cmp_to_sel(ncp - 1, n_sel).T
    full = lambda *shape: pl.BlockSpec(shape, lambda b, i: (0,) * len(shape))
    return pl.pallas_call(
        functools.partial(_nsa_prompt_body, seq=seq),
        grid=(batch, nq),
        in_specs=[pl.BlockSpec((TQ, NSA_W), lambda b, i: (b * nq + i, 0)),
                  pl.BlockSpec((seq, NSA_W), lambda b, i: (b, 0)),
                  full(CMP_LEN, LANES, 2 * CMP_HIDDEN), full(CMP_LEN, LANES),
                  full(2 * CMP_HIDDEN, LANES), full(n_sel, ncp)],
        out_specs=pl.BlockSpec((TQ, GROUP_WIDTH), lambda b, i: (b * nq + i, 0)),
        out_shape=jax.ShapeDtypeStruct((batch * seq, GROUP_WIDTH), F32),
        scratch_shapes=[pltpu.VMEM((seq + CMP_STRIDE, LANES), F32),
                        pltpu.VMEM((ncp, LANES), BF16), pltpu.VMEM((ncp, HEAD_DIM), BF16),
                        pltpu.VMEM((seq, LANES), BF16), pltpu.VMEM((seq, HEAD_DIM), BF16),
                        pltpu.VMEM((seq, LANES), BF16), pltpu.VMEM((seq, HEAD_DIM), BF16),
                        pltpu.VMEM((n_sel, TQ), F32),
                        pltpu.VMEM((N_HEADS * TQ, LANES), BF16), pltpu.VMEM((HEAD_DIM, N_HEADS * TQ), F32),
                        pltpu.VMEM((1, N_HEADS * TQ), F32), pltpu.VMEM((1, N_HEADS * TQ), F32),
                        pltpu.VMEM((HEAD_DIM, N_HEADS * TQ), F32)],
        compiler_params=_cparams(("parallel", "arbitrary")),
        name="nsa_prompt",
    )(x, x, w1, pos, w2, jnp.asarray(imp_t))


def _moba_prompt_body(q_ref, kv_ref, o_ref, km_scr, ka_scr, va_scr, selb_scr, qst_scr,
                      m_scr, l_scr, acc_scr, *, seq):
    qi = pl.program_id(1)
    tq = q_ref.shape[0]
    nb = seq // MOBA_BLOCK

    @pl.when(qi == 0)
    def _():
        for n in range(nb):
            km_scr[n:n + 1, :] = jnp.mean(kv_ref[n * MOBA_BLOCK:(n + 1) * MOBA_BLOCK, 256:512],
                                          axis=0, keepdims=True)

        kpos = lax.broadcasted_iota(jnp.int32, (seq, LANES), 0)
        for h in range(N_HEADS):
            pair = kv_ref[:, 256 + (h // 2) * LANES:256 + (h // 2 + 1) * LANES]
            if h % 2:
                pair = pltpu.roll(pair, HEAD_DIM, 1)
            ka_scr[h] = _key_tile(pair, kpos)
            va_scr[h] = kv_ref[:, 512 + h * HEAD_DIM:512 + (h + 1) * HEAD_DIM].astype(BF16)

    qpos = qi * tq + lax.broadcasted_iota(jnp.int32, (1, tq), 1)
    own = qpos // MOBA_BLOCK
    blk = lax.broadcasted_iota(jnp.int32, (nb, tq), 0)
    past = blk < own
    for h in range(N_HEADS):
        hs = slice(h * HEAD_DIM, (h + 1) * HEAD_DIM)
        gate = _hdot_nt(km_scr[:, hs], q_ref[:, hs])
        chosen = _rank_select(jnp.where(past, gate, NEG), blk, nb, min(MOBA_TOPK, nb), 0) & past
        selb_scr[h] = jnp.where(chosen, 0.0, NEG)

    qst_scr[...] = _query_stack(q_ref[:, 0:GROUP_WIDTH], MOBA_SLOPES)

    def step(rows, bias, first):
        _tile_step(lambda h: ka_scr[h, rows, :], lambda h: va_scr[h, rows, :], qst_scr, bias,
                   m_scr, l_scr, acc_scr, first, tq, tq)

    step(pl.ds(pl.multiple_of(qi * TK, TK), TK), _tri_bias(tq, True), True)

    def body(kt, carry):
        bias = jnp.concatenate([selb_scr[h, pl.ds(kt, 1), :] for h in range(N_HEADS)], axis=1)
        step(pl.ds(pl.multiple_of(kt * TK, TK), TK), bias, False)
        return carry

    lax.fori_loop(0, qi, body, 0)
    out = acc_scr[...] / l_scr[...]
    o_ref[...] = jnp.concatenate([out[:, h * tq:(h + 1) * tq] for h in range(N_HEADS)], axis=0).T


def moba_prompt(x, batch, seq):
    nq = seq // TQ
    nb = seq // MOBA_BLOCK
    return pl.pallas_call(
        functools.partial(_moba_prompt_body, seq=seq),
        grid=(batch, nq),
        in_specs=[pl.BlockSpec((TQ, MOBA_W), lambda b, i: (b * nq + i, 0)),
                  pl.BlockSpec((seq, MOBA_W), lambda b, i: (b, 0))],
        out_specs=pl.BlockSpec((TQ, GROUP_WIDTH), lambda b, i: (b * nq + i, 0)),
        out_shape=jax.ShapeDtypeStruct((batch * seq, GROUP_WIDTH), F32),
        scratch_shapes=[pltpu.VMEM((nb, GROUP_WIDTH), F32),
                        pltpu.VMEM((N_HEADS, seq, LANES), BF16), pltpu.VMEM((N_HEADS, seq, HEAD_DIM), BF16),
                        pltpu.VMEM((N_HEADS, nb, TQ), F32),
                        pltpu.VMEM((N_HEADS * TQ, LANES), BF16),
                        pltpu.VMEM((1, N_HEADS * TQ), F32), pltpu.VMEM((1, N_HEADS * TQ), F32),
                        pltpu.VMEM((HEAD_DIM, N_HEADS * TQ), F32)],
        compiler_params=_cparams(("parallel", "arbitrary")),
        name="moba_prompt",
    )(x, x)


def _softmax_rows(parts, masks):
    masked = [jnp.where(mk, s, NEG) for s, mk in zip(parts, masks)]
    m = masked[0].max(axis=1, keepdims=True)
    for s in masked[1:]:
        m = jnp.maximum(m, s.max(axis=1, keepdims=True))
    es = [jnp.where(mk, jnp.exp(s - m), 0.0) for s, mk in zip(masked, masks)]
    den = es[0].sum(axis=1, keepdims=True)
    for e in es[1:]:
        den = den + e.sum(axis=1, keepdims=True)
    inv = 1.0 / jnp.maximum(den, TINY)
    return [e * inv for e in es]


def _stack_heads(x, off):
    return jnp.concatenate([x[:, off + h * HEAD_DIM:off + (h + 1) * HEAD_DIM] for h in range(N_HEADS)], axis=0)


def _pad_rows(x, n):
    return jnp.concatenate([x, jnp.zeros((n - x.shape[0], x.shape[1]), x.dtype)], axis=0)


def _nsa_decode_body(*refs, n_pages, past_len):
    x_ref = refs[1]
    page_refs = refs[2:2 + n_pages]
    (win_ref, w1_ref, pos_ref, w2_ref, imp_ref, esel_ref, o_ref, xs_scr) = refs[2 + n_pages:]
    steps = x_ref.shape[0]
    rows = N_HEADS * steps
    ncp = past_len // CMP_STRIDE
    n_cmp = ncp - 1
    x = x_ref[...]
    tpos = past_len + lax.broadcasted_iota(jnp.int32, (steps, 1), 0)
    qpos = jnp.concatenate([tpos] * N_HEADS, axis=0)
    slope = jnp.concatenate([jnp.full((steps, 1), s, F32) for s in NSA_SLOPES], axis=0)
    qs = _stack_heads(x, 0) * (HEAD_DIM ** -0.5)
    gates = jax.nn.sigmoid(x[:, 640:768])
    gcol = [jnp.concatenate([gates[:, 3 * h + j:3 * h + j + 1] for h in range(N_HEADS)], axis=0)
            for j in range(3)]
    new_io = lax.broadcasted_iota(jnp.int32, (rows, LANES), 1)
    new_pos = past_len + new_io
    new_ok = (new_io < steps) & (new_pos <= qpos)

    for p in range(n_pages):
        xs_scr[p * PAGE_SIZE:(p + 1) * PAGE_SIZE, :] = page_refs[p][0, 0, 0:2].reshape(LANES, PAGE_SIZE).T
    xs_scr[past_len:past_len + CMP_STRIDE, :] = jnp.zeros((CMP_STRIDE, LANES), F32)
    kvc = _cmp_mlp(xs_scr, pos_ref, w1_ref, w2_ref, ncp)
    k_cmp, v_cmp = kvc[:, 0:HEAD_DIM], kvc[:, HEAD_DIM:2 * HEAD_DIM]
    n_io = lax.broadcasted_iota(jnp.int32, (rows, ncp), 1)
    cend = n_io * CMP_STRIDE + (CMP_LEN - 1)
    (pc,) = _softmax_rows([_bdot_nt(qs, k_cmp) + slope * cend.astype(F32)],
                          [(cend <= qpos) & (n_io < n_cmp)])
    o_cmp = _bdot(pc, v_cmp)
    pc_sum = pc[0:steps]
    for h in range(1, N_HEADS):
        pc_sum = pc_sum + pc[h * steps:(h + 1) * steps]

    n_sel = -(-(past_len + steps) // SEL_BLOCK)
    imp = _hdot(pc_sum, imp_ref[...])
    blk = lax.broadcasted_iota(jnp.int32, (steps, LANES), 1)
    cur = tpos // SEL_BLOCK
    forced = (blk == 0) | (blk == cur) | (blk == cur - 1)
    vals = jnp.where((blk <= cur) & (blk < n_sel), jnp.where(forced, BIG, imp), NEG)
    chosen = _rank_select(vals, blk, n_sel, min(N_SEL, n_sel), 1).astype(F32)
    key_sel = _bdot(chosen, esel_ref[...])
    key_sel = jnp.concatenate([key_sel] * N_HEADS, axis=0) > 0.5

    s_past = jnp.concatenate([_bdot(qs, page_refs[p][0, 0, 2]) for p in range(n_pages)], axis=1)
    kpos = lax.broadcasted_iota(jnp.int32, (rows, past_len), 1)
    k_new = _pad_rows(x[:, 384:448], LANES)
    v_new = _pad_rows(x[:, 448:512], LANES)
    s_new = _bdot_nt(qs, k_new) + slope * new_pos.astype(F32)
    p_past, p_new = _softmax_rows([s_past + slope * kpos.astype(F32), s_new],
                                  [key_sel[:, 0:past_len] & (kpos <= qpos),
                                   key_sel[:, past_len:past_len + LANES] & new_ok])
    o_sel = _bdot(p_new, v_new)
    for p in range(n_pages):
        o_sel = o_sel + _bdot_nt(p_past[:, p * PAGE_SIZE:(p + 1) * PAGE_SIZE], page_refs[p][0, 0, 3])

    n_win = win_ref.shape[-1]
    wpos = (past_len - n_win) + lax.broadcasted_iota(jnp.int32, (rows, n_win), 1)
    kw_new = _pad_rows(x[:, 512:576], LANES)
    vw_new = _pad_rows(x[:, 576:640], LANES)
    s_w = _bdot(qs, win_ref[0, 0, 0]) + slope * wpos.astype(F32)
    s_wn = _bdot_nt(qs, kw_new) + slope * new_pos.astype(F32)
    dist = qpos - wpos
    p_w, p_wn = _softmax_rows([s_w, s_wn],
                              [(dist >= 0) & (dist < WINDOW), new_ok & (qpos - new_pos < WINDOW)])
    o_win = _bdot_nt(p_w, win_ref[0, 0, 1]) + _bdot(p_wn, vw_new)

    out = gcol[0] * o_cmp + gcol[1] * o_sel + gcol[2] * o_win
    o_ref[...] = jnp.concatenate([out[h * steps:(h + 1) * steps] for h in range(N_HEADS)], axis=1)


def nsa_decode(x, page_ids, cache_t, layer, win_t, w1, pos, w2, *, steps, past_len):
    nseq = x.shape[0] // steps
    n_pages = past_len // PAGE_SIZE
    ncp = past_len // CMP_STRIDE
    n_sel = -(-(past_len + steps) // SEL_BLOCK)
    imp = np.zeros((ncp, LANES), np.float32)
    imp[:ncp - 1, :n_sel] = _cmp_to_sel(ncp - 1, n_sel)
    kblk = np.concatenate([np.arange(past_len) // SEL_BLOCK,
                           (past_len + np.arange(LANES)) // SEL_BLOCK])
    esel = (np.arange(LANES)[:, None] == kblk[None, :]).astype(np.float32)
    n_win = win_t.shape[-1]

    def page_spec(p):
        return pl.BlockSpec((1, 1, 4, HEAD_DIM, PAGE_SIZE),
                            lambda b, pt, p=p: (pt[b * n_pages + p], layer, 0, 0, 0))

    full = lambda *shape: pl.BlockSpec(shape, lambda b, pt: (0,) * len(shape))
    return pl.pallas_call(
        functools.partial(_nsa_decode_body, n_pages=n_pages, past_len=past_len),
        grid_spec=pltpu.PrefetchScalarGridSpec(
            num_scalar_prefetch=1,
            grid=(nseq,),
            in_specs=[pl.BlockSpec((steps, NSA_W), lambda b, pt: (b, 0))]
            + [page_spec(p) for p in range(n_pages)]
            + [pl.BlockSpec((1, 1, 2, HEAD_DIM, n_win), lambda b, pt: (layer, b, 0, 0, 0)),
               full(CMP_LEN, LANES, 2 * CMP_HIDDEN), full(CMP_LEN, LANES),
               full(2 * CMP_HIDDEN, LANES), full(ncp, LANES), full(LANES, past_len + LANES)],
            out_specs=pl.BlockSpec((steps, GROUP_WIDTH), lambda b, pt: (b, 0)),
            scratch_shapes=[pltpu.VMEM((past_len + CMP_STRIDE, LANES), F32)],
        ),
        out_shape=jax.ShapeDtypeStruct((nseq * steps, GROUP_WIDTH), F32),
        compiler_params=_cparams(("parallel",)),
        name="nsa_decode",
    )(page_ids, x, *([cache_t] * n_pages), win_t, w1, pos, w2, jnp.asarray(imp), jnp.asarray(esel))


def _moba_decode_body(*refs, n_pages, past_len):
    x_ref = refs[1]
    page_refs = refs[2:2 + n_pages]
    emean_ref, eblk_ref, o_ref = refs[2 + n_pages:]
    steps = x_ref.shape[0]
    rows = N_HEADS * steps
    x = x_ref[...]
    tpos = past_len + lax.broadcasted_iota(jnp.int32, (steps, 1), 0)
    qpos = jnp.concatenate([tpos] * N_HEADS, axis=0)
    own = qpos // MOBA_BLOCK
    slope = jnp.concatenate([jnp.full((steps, 1), s, F32) for s in MOBA_SLOPES], axis=0)
    head_of_row = lax.broadcasted_iota(jnp.int32, (rows, GROUP_WIDTH), 0) // steps
    head_of_col = lax.broadcasted_iota(jnp.int32, (rows, GROUP_WIDTH), 1) // HEAD_DIM
    diag = head_of_row == head_of_col
    q_bd = jnp.where(diag, jnp.concatenate([x[:, 0:GROUP_WIDTH]] * N_HEADS, axis=0), 0.0)

    kts = [page_refs[p][0, 0, 0].reshape(GROUP_WIDTH, PAGE_SIZE) for p in range(n_pages)]
    kmean_t = _bdot(kts[0], emean_ref[0])
    for p in range(1, n_pages):
        kmean_t = kmean_t + _bdot(kts[p], emean_ref[p])
    gate = _hdot(q_bd, kmean_t)
    nb = -(-(past_len + steps) // MOBA_BLOCK)
    blk = lax.broadcasted_iota(jnp.int32, (rows, LANES), 1)
    past = (blk < own) & (blk < nb)
    vals = jnp.where(past, gate, NEG)
    chosen = (_rank_select(vals, blk, nb, min(MOBA_TOPK, nb), 1) & past).astype(F32)
    key_sel = _bdot(chosen, eblk_ref[...]) > 0.5

    q_sc = q_bd * (HEAD_DIM ** -0.5)
    s_past = jnp.concatenate([_bdot(q_sc, kts[p]) for p in range(n_pages)], axis=1)
    kpos = lax.broadcasted_iota(jnp.int32, (rows, past_len), 1)
    new_io = lax.broadcasted_iota(jnp.int32, (rows, LANES), 1)
    new_pos = past_len + new_io
    k_new = _pad_rows(x[:, 256:512], LANES)
    v_new = _pad_rows(x[:, 512:768], LANES)
    s_new = _bdot_nt(q_sc, k_new) + slope * new_pos.astype(F32)
    in_own_past = (kpos // MOBA_BLOCK) == own
    in_own_new = (new_pos // MOBA_BLOCK) == own
    p_past, p_new = _softmax_rows(
        [s_past + slope * kpos.astype(F32), s_new],
        [key_sel | in_own_past, (new_io < steps) & (new_pos <= qpos) & in_own_new])
    o_all = _bdot(p_new, v_new)
    for p in range(n_pages):
        vt = page_refs[p][0, 0, 1].reshape(GROUP_WIDTH, PAGE_SIZE)
        o_all = o_all + _bdot_nt(p_past[:, p * PAGE_SIZE:(p + 1) * PAGE_SIZE], vt)
    o_all = jnp.where(diag, o_all, 0.0)
    out = o_all[0:steps]
    for h in range(1, N_HEADS):
        out = out + o_all[h * steps:(h + 1) * steps]
    o_ref[...] = out


def moba_decode(x, page_ids, cache_t, layer, *, steps, past_len):
    nseq = x.shape[0] // steps
    n_pages = past_len // PAGE_SIZE
    pages_per_blk = MOBA_BLOCK // PAGE_SIZE
    emean = np.zeros((n_pages, PAGE_SIZE, LANES), np.float32)
    for p in range(n_pages):
        emean[p, :, p // pages_per_blk] = 1.0 / MOBA_BLOCK
    eblk = (np.arange(LANES)[:, None] == (np.arange(past_len) // MOBA_BLOCK)[None, :]).astype(np.float32)

    def page_spec(p):
        return pl.BlockSpec((1, 1, 2, N_HEADS, HEAD_DIM, PAGE_SIZE),
                            lambda b, pt, p=p: (pt[b * n_pages + p], layer, 0, 0, 0, 0))

    full = lambda *shape: pl.BlockSpec(shape, lambda b, pt: (0,) * len(shape))
    return pl.pallas_call(
        functools.partial(_moba_decode_body, n_pages=n_pages, past_len=past_len),
        grid_spec=pltpu.PrefetchScalarGridSpec(
            num_scalar_prefetch=1,
            grid=(nseq,),
            in_specs=[pl.BlockSpec((steps, MOBA_W), lambda b, pt: (b, 0))]
            + [page_spec(p) for p in range(n_pages)]
            + [full(n_pages, PAGE_SIZE, LANES), full(LANES, past_len)],
            out_specs=pl.BlockSpec((steps, GROUP_WIDTH), lambda b, pt: (b, 0)),
        ),
        out_shape=jax.ShapeDtypeStruct((nseq * steps, GROUP_WIDTH), F32),
        compiler_params=_cparams(("parallel",)),
        name="moba_decode",
    )(page_ids, x, *([cache_t] * n_pages), jnp.asarray(emean), jnp.asarray(eblk))


def _regroup_w_in(w):
    d = w.shape[0]
    z = lambda n: jnp.zeros((d, n), w.dtype)
    gq, gk, gv, ga, gr = w[:, 0:128], w[:, 128:256], w[:, 256:512], w[:, 512:528], w[:, 528:784]
    ret = w[:, 784:1808]
    nq, nkv, ng = w[:, 1808:2064], w[:, 2064:2448], w[:, 2448:2460]
    moba = w[:, 2460:3228]
    return jnp.concatenate([gq, gk, gv, gr, ga, z(112), ret, nq, nkv, ng, z(116), moba], axis=1).astype(BF16)


def _cmp_weights(w1, pos, w2):
    w1r = w1.reshape(2, CMP_LEN, HEAD_DIM, CMP_HIDDEN)
    zero = jnp.zeros((CMP_LEN, HEAD_DIM, CMP_HIDDEN), w1.dtype)
    w1bd = jnp.concatenate([jnp.concatenate([w1r[0], zero], axis=2),
                            jnp.concatenate([zero, w1r[1]], axis=2)], axis=1)
    zero2 = jnp.zeros((CMP_HIDDEN, HEAD_DIM), w2.dtype)
    w2bd = jnp.concatenate([jnp.concatenate([w2[0], zero2], axis=1),
                            jnp.concatenate([zero2, w2[1]], axis=1)], axis=0)
    posf = jnp.concatenate([pos[0], pos[1]], axis=1)
    return w1bd.astype(BF16), posf, w2bd.astype(BF16)


def kernel(x_prompt, x_sample, cache_nsa, cache_moba, cache_nsa_win, state_gla, state_ret, page_table, w_in, gla_w_a2, gla_b_a, gla_norm, ret_norm, nsa_cmp_w1, nsa_cmp_w2, nsa_cmp_pos, w_out, ffn_w_up, ffn_w_down, ln_g, ln_b):
    bp, sp, d = x_prompt.shape
    bs, ss, _ = x_sample.shape
    past_len = page_table.shape[1] * PAGE_SIZE
    n_win = cache_nsa_win.shape[2]

    nsa_t = jnp.transpose(cache_nsa, (0, 1, 3, 4, 2))
    moba_t = jnp.transpose(cache_moba, (0, 1, 3, 4, 5, 2))
    win_t = jnp.transpose(cache_nsa_win, (0, 1, 3, 4, 2))
    gla_t = jnp.transpose(state_gla, (0, 2, 3, 4, 1)).reshape(DEPTH, N_HEADS * GLA_DK, HEAD_DIM, bs)
    ret_t = jnp.transpose(state_ret, (0, 2, 3, 4, 1)).reshape(DEPTH, N_HEADS * HEAD_DIM, HEAD_DIM, bs)
    page_ids = page_table.reshape(-1)

    log_gamma = np.log1p(-np.power(2.0, -5.0 - np.arange(N_HEADS, dtype=np.float64))).astype(np.float32)
    la_ret = jnp.asarray(np.repeat(log_gamma, HEAD_DIM))
    zeros128 = jnp.zeros((1, 128), F32)

    xp = x_prompt.reshape(bp * sp, d)
    xs = x_sample.reshape(bs * ss, d)
    outs = {k: [] for k in ("nsa_p", "moba_p", "win_p", "gla_p", "ret_p",
                            "nsa_s", "moba_s", "win_s", "gla_s", "ret_s")}
    for l in range(DEPTH):
        wu = ffn_w_up[l].astype(BF16)
        wd = ffn_w_down[l].astype(BF16)
        wi = _regroup_w_in(w_in[l])
        wo = w_out[l].astype(BF16)
        g = ln_g[l].reshape(3, 1, d)
        b = ln_b[l].reshape(3, 1, d)
        wa = jnp.zeros((128, 128), F32).at[0:GLA_RANK, :].set(gla_w_a2[l])
        ba = gla_b_a[l].reshape(1, 128)
        gn = gla_norm[l].reshape(1, GROUP_WIDTH)
        rn = ret_norm[l].reshape(1, GROUP_WIDTH)
        w1bd, posf, w2bd = _cmp_weights(nsa_cmp_w1[l], nsa_cmp_pos[l], nsa_cmp_w2[l])

        xp = ffn_ln(xp, wu[0], wd[0], g[0], b[0])
        pg, pr, pn, pm = proj(xp, wi)
        o_gla, st_gla = gla_prompt(pg, bp, sp, wa, ba, gn)
        o_ret, st_ret = ret_prompt(pr, bp, sp, rn)
        o_nsa = nsa_prompt(pn, bp, sp, w1bd, posf, w2bd)
        o_moba = moba_prompt(pm, bp, sp)
        xp = out_ln(xp, (o_gla, o_ret, o_nsa, o_moba), wo, g[1], b[1])
        xp = ffn_ln(xp, wu[1], wd[1], g[2], b[2])
        pn3 = pn.reshape(bp, sp, NSA_W)
        outs["nsa_p"].append(pn3[:, :, 256:512].reshape(bp, sp, 4, HEAD_DIM))
        outs["moba_p"].append(pm.reshape(bp, sp, MOBA_W)[:, :, 256:768].reshape(bp, sp, 2, N_HEADS, HEAD_DIM))
        keep = min(WINDOW, sp)
        outs["win_p"].append(pn3[:, sp - keep:, 512:640].reshape(bp, keep, 2, HEAD_DIM))
        outs["gla_p"].append(st_gla)
        outs["ret_p"].append(st_ret)

        xs = ffn_ln(xs, wu[0], wd[0], g[0], b[0])
        pg, pr, pn, pm = proj(xs, wi)
        to_lanes = lambda a: jnp.transpose(a.reshape(bs, ss, a.shape[1]), (1, 2, 0))
        og_t, sg_t = lin_decode(to_lanes(pg), gla_t[l], "gla", wa.T,
                                ba.reshape(128, 1), jnp.zeros((128, 1), F32), gn.reshape(GROUP_WIDTH, 1))
        or_t, sr_t = lin_decode(to_lanes(pr), ret_t[l], "ret", jnp.zeros((128, 128), F32),
                                jnp.zeros((128, 1), F32), la_ret.reshape(256, 1), rn.reshape(GROUP_WIDTH, 1))
        from_lanes = lambda a: jnp.transpose(a, (2, 0, 1)).reshape(bs * ss, GROUP_WIDTH)
        o_nsa = nsa_decode(pn, page_ids, nsa_t, l, win_t, w1bd, posf, w2bd, steps=ss, past_len=past_len)
        o_moba = moba_decode(pm, page_ids, moba_t, l, steps=ss, past_len=past_len)
        xs = out_ln(xs, (from_lanes(og_t), from_lanes(or_t), o_nsa, o_moba), wo, g[1], b[1])
        xs = ffn_ln(xs, wu[1], wd[1], g[2], b[2])
        pn3 = pn.reshape(bs, ss, NSA_W)
        outs["nsa_s"].append(pn3[:, :, 256:512].reshape(bs, ss, 4, HEAD_DIM))
        outs["moba_s"].append(pm.reshape(bs, ss, MOBA_W)[:, :, 256:768].reshape(bs, ss, 2, N_HEADS, HEAD_DIM))
        new_win = pn3[:, :, 512:640].reshape(bs, ss, 2, HEAD_DIM)
        outs["win_s"].append(jnp.concatenate([cache_nsa_win[l], new_win], axis=1)[:, ss:] if n_win > 0
                             else new_win[:, ss - min(WINDOW, ss):])
        outs["gla_s"].append(jnp.transpose(sg_t.reshape(N_HEADS, GLA_DK, HEAD_DIM, bs), (3, 0, 1, 2)))
        outs["ret_s"].append(jnp.transpose(sr_t.reshape(N_HEADS, HEAD_DIM, HEAD_DIM, bs), (3, 0, 1, 2)))

    return (xp.reshape(bp, sp, d), xs.reshape(bs, ss, d),
            jnp.stack(outs["nsa_p"], axis=1), jnp.stack(outs["moba_p"], axis=1),
            jnp.stack(outs["win_p"], axis=0), jnp.stack(outs["gla_p"], axis=0), jnp.stack(outs["ret_p"], axis=0),
            jnp.stack(outs["nsa_s"], axis=1), jnp.stack(outs["moba_s"], axis=1),
            jnp.stack(outs["win_s"], axis=0), jnp.stack(outs["gla_s"], axis=0), jnp.stack(outs["ret_s"], axis=0))
```

```python
import functools
import math

import numpy as np
import jax
import jax.numpy as jnp
from jax import lax
from jax.experimental import pallas as pl
from jax.experimental.pallas import tpu as pltpu

F32 = jnp.float32
BF16 = jnp.bfloat16
HI = lax.Precision.HIGHEST

D_MODEL = 1024
DEPTH = 4
PAGE_SIZE = 128
HEAD_DIM = 64
N_HEADS = 4
GROUP_WIDTH = 256
GLA_DK = 32
GLA_RANK = 16
GLA_TAU = 16.0
CMP_LEN = 32
CMP_STRIDE = 16
CMP_HIDDEN = 128
SEL_BLOCK = 64
N_SEL = 8
WINDOW = 512
MOBA_BLOCK = 256
MOBA_TOPK = 3
D_FF = 2816
ALPHA = (2 * DEPTH) ** 0.25
LN_EPS = 1e-5
NEG = -1e30
BIG = 1e30
TINY = 1e-30

LANES = 128
VMEM_LIMIT = 56 * 1024 * 1024

GLA_W = 896
RET_W = 1024
NSA_W = 768
MOBA_W = 768
PROJ_SPLITS = (GLA_W, RET_W, NSA_W, MOBA_W)


def _slopes():
    n = 2 * N_HEADS
    s = [2.0 ** (-8.0 * i / n) for i in range(1, n + 1)]
    return s[0::2], s[1::2]


NSA_SLOPES, MOBA_SLOPES = _slopes()


def _bdot(a, b):
    return jnp.dot(a.astype(BF16), b.astype(BF16), preferred_element_type=F32)


def _bdot_nt(a, b):
    return lax.dot_general(a.astype(BF16), b.astype(BF16), (((1,), (1,)), ((), ())),
                           preferred_element_type=F32)


def _bdot_tn(a, b):
    return lax.dot_general(a.astype(BF16), b.astype(BF16), (((0,), (0,)), ((), ())),
                           preferred_element_type=F32)


def _hdot(a, b):
    return jnp.dot(a, b, precision=HI, preferred_element_type=F32)


def _hdot_nt(a, b):
    return lax.dot_general(a, b, (((1,), (1,)), ((), ())), precision=HI, preferred_element_type=F32)


def _hdot_tn(a, b):
    return lax.dot_general(a, b, (((0,), (0,)), ((), ())), precision=HI, preferred_element_type=F32)


def _split3(a):
    hi = a.astype(BF16)
    r1 = a - hi.astype(F32)
    mid = r1.astype(BF16)
    lo = (r1 - mid.astype(F32)).astype(BF16)
    return hi, mid, lo


def _xdot(a, c, dims=(((1,), (0,)), ((), ()))):
    cb = c.astype(BF16)
    return sum(lax.dot_general(part, cb, dims, preferred_element_type=F32) for part in _split3(a))


def _xdot_left(c, a):
    cb = c.astype(BF16)
    return sum(jnp.dot(cb, part, preferred_element_type=F32) for part in _split3(a))


def _layer_norm(y, g, b):
    mu = jnp.mean(y, axis=-1, keepdims=True)
    d = y - mu
    var = jnp.mean(d * d, axis=-1, keepdims=True)
    return d * lax.rsqrt(var + LN_EPS) * g + b


def _cparams(sem):
    return pltpu.CompilerParams(dimension_semantics=sem, vmem_limit_bytes=VMEM_LIMIT)


def _ffn_ln_body(x_ref, wu_ref, wg_ref, wd_ref, g_ref, b_ref, o_ref, acc_ref):
    j = pl.program_id(1)

    @pl.when(j == 0)
    def _():
        acc_ref[...] = jnp.zeros_like(acc_ref)

    xb = x_ref[...].astype(BF16)
    u = jnp.dot(xb, wu_ref[...], preferred_element_type=F32)
    gt = jnp.dot(xb, wg_ref[...], preferred_element_type=F32)
    a = (gt * jax.nn.sigmoid(gt) * u).astype(BF16)
    acc_ref[...] += jnp.dot(a, wd_ref[...], preferred_element_type=F32)

    @pl.when(j == pl.num_programs(1) - 1)
    def _():
        y = ALPHA * x_ref[...] + 0.5 * acc_ref[...]
        o_ref[...] = _layer_norm(y, g_ref[...], b_ref[...])


def ffn_ln(x, w_up, w_down, g, b, *, tm=512, tf=1408):
    m, d = x.shape
    f = w_down.shape[0]
    tm = min(tm, m)
    nf = f // tf
    return pl.pallas_call(
        _ffn_ln_body,
        grid=(m // tm, nf),
        in_specs=[
            pl.BlockSpec((tm, d), lambda i, j: (i, 0)),
            pl.BlockSpec((d, tf), lambda i, j: (0, j)),
            pl.BlockSpec((d, tf), lambda i, j: (0, j + nf)),
            pl.BlockSpec((tf, d), lambda i, j: (j, 0)),
            pl.BlockSpec((1, d), lambda i, j: (0, 0)),
            pl.BlockSpec((1, d), lambda i, j: (0, 0)),
        ],
        out_specs=pl.BlockSpec((tm, d), lambda i, j: (i, 0)),
        out_shape=jax.ShapeDtypeStruct((m, d), F32),
        scratch_shapes=[pltpu.VMEM((tm, d), F32)],
        compiler_params=_cparams(("parallel", "arbitrary")),
        name="ffn_ln",
    )(x, w_up, w_up, w_down, g, b)


def _proj_body(x_ref, w_ref, *o_refs):
    xb = x_ref[...].astype(BF16)
    off = 0
    for o_ref in o_refs:
        wdt = o_ref.shape[1]
        o_ref[...] = jnp.dot(xb, w_ref[:, off:off + wdt], preferred_element_type=F32)
        off += wdt


def proj(x, w, *, tm=512):
    m, d = x.shape
    n = w.shape[1]
    tm = min(tm, m)
    return pl.pallas_call(
        _proj_body,
        grid=(m // tm,),
        in_specs=[pl.BlockSpec((tm, d), lambda i: (i, 0)),
                  pl.BlockSpec((d, n), lambda i: (0, 0))],
        out_specs=[pl.BlockSpec((tm, wdt), lambda i: (i, 0)) for wdt in PROJ_SPLITS],
        out_shape=[jax.ShapeDtypeStruct((m, wdt), F32) for wdt in PROJ_SPLITS],
        compiler_params=_cparams(("parallel",)),
        name="proj",
    )(x, w)


def _out_ln_body(x_ref, o0_ref, o1_ref, o2_ref, o3_ref, w_ref, g_ref, b_ref, y_ref):
    mix = None
    for gi, o_ref in enumerate((o0_ref, o1_ref, o2_ref, o3_ref)):
        part = jnp.dot(o_ref[...].astype(BF16), w_ref[gi * GROUP_WIDTH:(gi + 1) * GROUP_WIDTH, :],
                       preferred_element_type=F32)
        mix = part if mix is None else mix + part
    y_ref[...] = _layer_norm(ALPHA * x_ref[...] + mix, g_ref[...], b_ref[...])


def out_ln(x, outs, w_out, g, b, *, tm=512):
    m, d = x.shape
    tm = min(tm, m)
    row = lambda i: (i, 0)
    return pl.pallas_call(
        _out_ln_body,
        grid=(m // tm,),
        in_specs=[pl.BlockSpec((tm, d), row)]
        + [pl.BlockSpec((tm, GROUP_WIDTH), row) for _ in range(4)]
        + [pl.BlockSpec((d, d), lambda i: (0, 0)),
           pl.BlockSpec((1, d), lambda i: (0, 0)),
           pl.BlockSpec((1, d), lambda i: (0, 0))],
        out_specs=pl.BlockSpec((tm, d), row),
        out_shape=jax.ShapeDtypeStruct((m, d), F32),
        compiler_params=_cparams(("parallel",)),
        name="out_ln",
    )(x, *outs, w_out, g, b)


def _head_group_norm(o, jn, gain):
    mu = _xdot(o, jn)
    d = o - mu
    var = _xdot(d * d, jn)
    return d * lax.rsqrt(var + LN_EPS) * gain


GLA_CHUNK = 16
GLA_ROWS = 128
RET_CHUNK = 128


def _gla_prompt_body(x_ref, wa_ref, ba_ref, gain_ref, tri_ref, tot_ref, emask_ref, jm_ref, jn_ref,
                     o_ref, st_ref, s_scr):
    c, r = GLA_CHUNK, GLA_ROWS
    g = r // c
    kd = 128
    seq = x_ref.shape[0]
    s_scr[...] = jnp.zeros_like(s_scr)
    jm = jm_ref[...]
    si = lax.broadcasted_iota(jnp.int32, (g, c, c, kd), 1)
    ti = lax.broadcasted_iota(jnp.int32, (g, c, c, kd), 2)
    causal = ti >= si

    def step(i, carry):
        rows = pl.ds(pl.multiple_of(i * r, r), r)
        q = x_ref[rows, 0:128] * (GLA_DK ** -0.5)
        k = x_ref[rows, 128:256]
        v = x_ref[rows, 256:512]
        gate = x_ref[rows, 512:768]
        la = jax.nn.log_sigmoid(_hdot(x_ref[rows, 768:896], wa_ref[...]) + ba_ref[...]) / GLA_TAU
        bt = _hdot(tri_ref[...], la)
        btot = _hdot(tot_ref[...], la)
        bt4 = bt.reshape(g, c, kd)
        q4 = q.reshape(g, c, kd)
        k4 = k.reshape(g, c, kd)
        dlt = jnp.minimum(bt4[:, None, :, :] - bt4[:, :, None, :], 0.0)
        w = jnp.where(causal, q4[:, None, :, :] * k4[:, :, None, :] * jnp.exp(dlt), 0.0)
        z = _bdot(w.reshape(g * c * c, kd), jm).reshape(g, c, c, GROUP_WIDTH)
        o = jnp.sum(z * v.reshape(g, c, GROUP_WIDTH)[:, :, None, :], axis=1).reshape(r, GROUP_WIDTH)
        qd = q * jnp.exp(bt)
        kdn = k * jnp.exp(btot - bt)
        edec = jnp.exp(_hdot_tn(la, emask_ref[...]))
        s_cur = s_scr[...]
        inter = []
        for j in range(g):
            rs = slice(j * c, (j + 1) * c)
            inter.append(_bdot(qd[rs], s_cur))
            upd = jnp.where(jm > 0, _bdot_tn(kdn[rs], v[rs]), 0.0)
            s_cur = jnp.broadcast_to(edec[:, j:j + 1], (kd, GROUP_WIDTH)) * s_cur + upd
        s_scr[...] = s_cur
        o = o + jnp.concatenate(inter, axis=0)
        o_ref[rows, :] = _head_group_norm(o, jn_ref[...], gain_ref[...]) * (gate * jax.nn.sigmoid(gate))
        return carry

    lax.fori_loop(0, seq // r, step, 0)
    s_fin = s_scr[...]
    for h in range(N_HEADS):
        st_ref[0, h] = s_fin[h * GLA_DK:(h + 1) * GLA_DK, h * HEAD_DIM:(h + 1) * HEAD_DIM]


def _ret_prompt_body(x_ref, gain_ref, dstack_ref, qdec_ref, kdec_ref, sdec_ref, hm_ref, jm_ref, jn_ref,
                     o_ref, st_ref, s_scr):
    c = RET_CHUNK
    seq = x_ref.shape[0]
    s_scr[...] = jnp.zeros_like(s_scr)

    def step(i, carry):
        rows = pl.ds(pl.multiple_of(i * c, c), c)
        q = x_ref[rows, 0:256]
        k = x_ref[rows, 256:512] * (HEAD_DIM ** -0.5)
        v = x_ref[rows, 512:768]
        gate = x_ref[rows, 768:1024]
        hm = hm_ref[...]
        qs = jnp.concatenate([q] * N_HEADS, axis=0) * hm
        sc = _bdot_nt(qs, k) * dstack_ref[...]
        of = _bdot(sc, v) * hm
        o = of[0:c]
        for h in range(1, N_HEADS):
            o = o + of[h * c:(h + 1) * c]
        s_old = s_scr[...]
        o = o + _bdot(q * qdec_ref[...], s_old)
        upd = jnp.where(jm_ref[...] > 0, _bdot_tn(k * kdec_ref[...], v), 0.0)
        s_scr[...] = sdec_ref[...] * s_old + upd
        o_ref[rows, :] = _head_group_norm(o, jn_ref[...], gain_ref[...]) * (gate * jax.nn.sigmoid(gate))
        return carry

    lax.fori_loop(0, seq // c, step, 0)
    s_fin = s_scr[...]
    for h in range(N_HEADS):
        st_ref[0, h] = s_fin[h * HEAD_DIM:(h + 1) * HEAD_DIM, h * HEAD_DIM:(h + 1) * HEAD_DIM]


def _blockdiag(kd, dk):
    r = np.arange(kd)[:, None] // dk
    cidx = np.arange(GROUP_WIDTH)[None, :] // HEAD_DIM
    return (r == cidx).astype(np.float32)


def _lin_prompt_call(body, name, x, batch, seq, kd, dk, consts):
    full = lambda a: pl.BlockSpec(a.shape, lambda b: (0,) * a.ndim)
    return pl.pallas_call(
        body,
        grid=(batch,),
        in_specs=[pl.BlockSpec((seq, x.shape[1]), lambda b: (b, 0))] + [full(a) for a in consts],
        out_specs=[pl.BlockSpec((seq, GROUP_WIDTH), lambda b: (b, 0)),
                   pl.BlockSpec((1, N_HEADS, dk, HEAD_DIM), lambda b: (b, 0, 0, 0))],
        out_shape=[jax.ShapeDtypeStruct((batch * seq, GROUP_WIDTH), F32),
                   jax.ShapeDtypeStruct((batch, N_HEADS, dk, HEAD_DIM), F32)],
        scratch_shapes=[pltpu.VMEM((kd, GROUP_WIDTH), F32)],
        compiler_params=_cparams(("parallel",)),
        name=name,
    )(x, *consts)


def gla_prompt(x, batch, seq, wa, ba, gain):
    r, c = GLA_ROWS, GLA_CHUNK
    same = (np.arange(r)[:, None] // c) == (np.arange(r)[None, :] // c)
    tri = (same & (np.arange(r)[:, None] >= np.arange(r)[None, :])).astype(np.float32)
    emask = ((np.arange(r)[:, None] // c) == np.arange(LANES)[None, :]).astype(np.float32)
    consts = [wa, ba, gain, jnp.asarray(tri), jnp.asarray(same.astype(np.float32)), jnp.asarray(emask),
              jnp.asarray(_blockdiag(128, GLA_DK)), jnp.asarray(_blockdiag(GROUP_WIDTH, HEAD_DIM) / HEAD_DIM)]
    return _lin_prompt_call(_gla_prompt_body, "gla_prompt", x, batch, seq, 128, GLA_DK, consts)


def ret_prompt(x, batch, seq, gain):
    c = RET_CHUNK
    log_gamma = np.log1p(-np.power(2.0, -5.0 - np.arange(N_HEADS, dtype=np.float64)))
    t = np.arange(c)
    diff = t[:, None] - t[None, :]
    dstack = np.concatenate([np.where(diff >= 0, np.exp(lg * np.maximum(diff, 0)), 0.0) for lg in log_gamma], axis=0)
    per_lane = np.repeat(log_gamma, HEAD_DIM)
    qdec = np.exp(per_lane[None, :] * (t[:, None] + 1))
    kdec = np.exp(per_lane[None, :] * (c - 1 - t[:, None]))
    sdec = np.broadcast_to(np.exp(per_lane * c)[:, None], (GROUP_WIDTH, GROUP_WIDTH))
    hm = np.concatenate([np.broadcast_to(np.arange(GROUP_WIDTH)[None, :] // HEAD_DIM == h, (c, GROUP_WIDTH))
                         for h in range(N_HEADS)], axis=0)
    as_f32 = lambda a: jnp.asarray(np.asarray(a, np.float32))
    consts = [gain, as_f32(dstack), as_f32(qdec), as_f32(kdec), as_f32(sdec), as_f32(hm),
              jnp.asarray(_blockdiag(GROUP_WIDTH, HEAD_DIM)),
              jnp.asarray(_blockdiag(GROUP_WIDTH, HEAD_DIM) / HEAD_DIM)]
    return _lin_prompt_call(_ret_prompt_body, "ret_prompt", x, batch, seq, GROUP_WIDTH, HEAD_DIM, consts)


def _lin_decode_body(x_ref, s0_ref, wa_ref, ba_ref, la_ref, gain_ref, o_ref, s1_ref, a_scr, o_scr,
                     *, kind, kd, dk):
    steps = x_ref.shape[0]
    if kind == "gla":
        qo, ko, vo, go = 0, 128, 256, 512
        for t in range(steps):
            pre = _hdot(wa_ref[...], x_ref[t, 768:896, :]) + ba_ref[...]
            a_scr[t] = jnp.exp(jax.nn.log_sigmoid(pre) / GLA_TAU)
        qscale, kscale = GLA_DK ** -0.5, 1.0
    else:
        qo, ko, vo, go = 0, 256, 512, 768
        for t in range(steps):
            a_scr[t] = jnp.exp(jnp.broadcast_to(la_ref[...], a_scr.shape[1:]))
        qscale, kscale = 1.0, HEAD_DIM ** -0.5
    o_scr[...] = jnp.zeros_like(o_scr)

    def body(j8, carry):
        r0 = pl.multiple_of(j8 * 8, 8)
        h = r0 // dk
        vrows = pl.ds(pl.multiple_of(vo + h * HEAD_DIM, HEAD_DIM), HEAD_DIM)
        orows = pl.ds(pl.multiple_of(h * HEAD_DIM, HEAD_DIM), HEAD_DIM)
        a8 = [a_scr[t, pl.ds(r0, 8), :] for t in range(steps)]
        q8 = [x_ref[t, pl.ds(qo + r0, 8), :] * qscale for t in range(steps)]
        k8 = [x_ref[t, pl.ds(ko + r0, 8), :] * kscale for t in range(steps)]
        for jj in range(8):
            sj = s0_ref[r0 + jj]
            for t in range(steps):
                vt = x_ref[t, vrows, :]
                sj = a8[t][jj:jj + 1, :] * sj + k8[t][jj:jj + 1, :] * vt
                o_scr[t, orows, :] += q8[t][jj:jj + 1, :] * sj
            s1_ref[r0 + jj] = sj
        return carry

    lax.fori_loop(0, kd // 8, body, 0)
    for t in range(steps):
        for h in range(N_HEADS):
            blk = slice(h * HEAD_DIM, (h + 1) * HEAD_DIM)
            o = o_scr[t, blk, :]
            mu = jnp.mean(o, axis=0, keepdims=True)
            d = o - mu
            var = jnp.mean(d * d, axis=0, keepdims=True)
            gate = x_ref[t, go + h * HEAD_DIM:go + (h + 1) * HEAD_DIM, :]
            o_ref[t, blk, :] = d * lax.rsqrt(var + LN_EPS) * gain_ref[blk, :] * (gate * jax.nn.sigmoid(gate))


def lin_decode(xt, s0, kind, wa_t, ba_col, la_col, gain_col):
    kd, dk = (128, GLA_DK) if kind == "gla" else (256, HEAD_DIM)
    steps, _, nb = xt.shape
    return pl.pallas_call(
        functools.partial(_lin_decode_body, kind=kind, kd=kd, dk=dk),
        out_shape=[jax.ShapeDtypeStruct((steps, GROUP_WIDTH, nb), F32),
                   jax.ShapeDtypeStruct((kd, HEAD_DIM, nb), F32)],
        scratch_shapes=[pltpu.VMEM((steps, kd, nb), F32), pltpu.VMEM((steps, GROUP_WIDTH, nb), F32)],
        compiler_params=pltpu.CompilerParams(vmem_limit_bytes=VMEM_LIMIT),
        name="lin_decode_" + kind,
    )(xt, s0, wa_t, ba_col, la_col, gain_col)


def _cmp_mlp(xs_list, pos_ref, w1_ref, w2_ref, n_rows):
    xcat = jnp.concatenate(
        [jnp.concatenate([(xs[pl.ds(r, n_rows, stride=CMP_STRIDE), :] + pos_ref[r:r + 1, :]).astype(BF16)
                          for r in range(CMP_LEN)], axis=1) for xs in xs_list], axis=0)
    hid = jnp.dot(xcat, w1_ref[...].reshape(CMP_LEN * LANES, 2 * CMP_HIDDEN), preferred_element_type=F32)
    return _bdot(jax.nn.gelu(hid), w2_ref[...])


def _rank_select(vals, idx, n_rows, n_top, axis):
    cnt = jnp.zeros(vals.shape, F32)
    for j in range(n_rows):
        vj = vals[j:j + 1, :] if axis == 0 else vals[:, j:j + 1]
        before = (vj > vals) | ((vj == vals) & (j < idx))
        cnt = cnt + before.astype(F32)
    return cnt < n_top


DECODE_SEQS = 4
TQ = 256
TK = 256


POS_HI_LANE = HEAD_DIM
POS_LO_LANE = HEAD_DIM + 1


def _nt(a, b):
    return lax.dot_general(a, b, (((1,), (1,)), ((), ())), preferred_element_type=F32)


def _tn(a, b):
    return lax.dot_general(a, b, (((0,), (0,)), ((), ())), preferred_element_type=F32)


def _key_tile(x128, pos):
    lane = lax.broadcasted_iota(jnp.int32, x128.shape, 1)
    feat = jnp.where(lane == POS_HI_LANE, (pos // 64).astype(F32),
                     jnp.where(lane == POS_LO_LANE, (pos % 64).astype(F32), 0.0))
    return jnp.where(lane < HEAD_DIM, x128, feat).astype(BF16)


def _query_stack(q256, slopes):
    tq = q256.shape[0]
    lane = lax.broadcasted_iota(jnp.int32, (tq, LANES), 1)
    parts = []
    for h in range(N_HEADS):
        pair = q256[:, (h // 2) * LANES:(h // 2 + 1) * LANES]
        if h % 2:
            pair = pltpu.roll(pair, HEAD_DIM, 1)
        const = jnp.where(lane == POS_HI_LANE, 64.0 * slopes[h], jnp.where(lane == POS_LO_LANE, slopes[h], 0.0))
        parts.append(jnp.where(lane < HEAD_DIM, pair * (HEAD_DIM ** -0.5), const).astype(BF16))
    return jnp.concatenate(parts, axis=0)


def _tile_step(k_fn, v_fn, qst_scr, bias, m_scr, l_scr, acc_scr, first, tq, group):
    starts = list(range(0, N_HEADS * tq, group))
    scores = []
    for c0 in starts:
        s = _nt(k_fn(c0 // tq), qst_scr[c0:c0 + group, :])
        scores.append(s if bias is None else s + bias[:, c0:c0 + group])
    probs, stats = [], []
    for c0, s in zip(starts, scores):
        cols = slice(c0, c0 + group)
        if first:
            m_new = jnp.max(s, axis=0, keepdims=True)
            p = jnp.exp(s - m_new)
            stats.append((m_new, jnp.sum(p, axis=0, keepdims=True), None))
        else:
            m_old = m_scr[:, cols]
            m_new = jnp.maximum(m_old, jnp.max(s, axis=0, keepdims=True))
            p = jnp.exp(s - m_new)
            corr = jnp.exp(m_old - m_new)
            stats.append((m_new, corr * l_scr[:, cols] + jnp.sum(p, axis=0, keepdims=True), corr))
        probs.append(p.astype(BF16))
    for c0, p, (m_new, l_new, corr) in zip(starts, probs, stats):
        cols = slice(c0, c0 + group)
        pv = _tn(v_fn(c0 // tq), p)
        acc_scr[:, cols] = pv if first else corr * acc_scr[:, cols] + pv
        m_scr[:, cols] = m_new
        l_scr[:, cols] = l_new


def _tile4(x):
    return jnp.concatenate([x] * N_HEADS, axis=1)


def _tri_bias(tq, keep_lower):
    row = lax.broadcasted_iota(jnp.int32, (TK, tq), 0)
    col = lax.broadcasted_iota(jnp.int32, (TK, tq), 1)
    return _tile4(jnp.where((row <= col) if keep_lower else (row > col), 0.0, NEG))


def _nsa_prompt_body(q_ref, kv_ref, w1_ref, pos_ref, w2_ref, imp_ref, o_ref,
                     xs_scr, kc_scr, vc_scr, ks_scr, vs_scr, kw_scr, vw_scr, selb_scr,
                     qst_scr, ocmp_scr, m_scr, l_scr, acc_scr, *, seq):
    qi = pl.program_id(1)
    tq = q_ref.shape[0]
    ncp = seq // CMP_STRIDE
    n_cmp = ncp - 1
    n_sel = seq // SEL_BLOCK
    blocks_per_tile = TK // SEL_BLOCK

    @pl.when(qi == 0)
    def _():
        xs_scr[0:seq, :] = kv_ref[:, 256:384]
        xs_scr[seq:seq + CMP_STRIDE, :] = jnp.zeros((CMP_STRIDE, LANES), F32)
        kvc = _cmp_mlp([xs_scr], pos_ref, w1_ref, w2_ref, ncp)
        cend = lax.broadcasted_iota(jnp.int32, (ncp, LANES), 0) * CMP_STRIDE + (CMP_LEN - 1)
        kc_scr[...] = _key_tile(kvc, cend)
        vc_scr[...] = kvc[:, HEAD_DIM:2 * HEAD_DIM].astype(BF16)
        kpos = lax.broadcasted_iota(jnp.int32, (seq, LANES), 0)
        ks_scr[...] = _key_tile(kv_ref[:, 384:512], kpos)
        vs_scr[...] = kv_ref[:, 448:512].astype(BF16)
        kw_scr[...] = _key_tile(kv_ref[:, 512:640], kpos)
        vw_scr[...] = kv_ref[:, 576:640].astype(BF16)

    qpos = qi * tq + lax.broadcasted_iota(jnp.int32, (1, tq), 1)
    gates_t = jax.nn.sigmoid(q_ref[:, 640:768]).T
    qst_scr[...] = _query_stack(q_ref[:, 0:GROUP_WIDTH], NSA_SLOPES)

    n_io = lax.broadcasted_iota(jnp.int32, (ncp, tq), 0)
    cmask = _tile4(((n_io * CMP_STRIDE + (CMP_LEN - 1)) <= qpos) & (n_io < n_cmp))
    s = jnp.where(cmask, _nt(kc_scr[...], qst_scr[...]), NEG)
    e = jnp.where(cmask, jnp.exp(s - jnp.max(s, axis=0, keepdims=True)), 0.0)
    pc = e / jnp.maximum(jnp.sum(e, axis=0, keepdims=True), TINY)
    ocmp_scr[...] = _tn(vc_scr[...], pc.astype(BF16))
    pc_sum = pc[:, 0:tq]
    for h in range(1, N_HEADS):
        pc_sum = pc_sum + pc[:, h * tq:(h + 1) * tq]

    imp = _xdot_left(imp_ref[...], pc_sum)
    blk = lax.broadcasted_iota(jnp.int32, (n_sel, tq), 0)
    cur = qpos // SEL_BLOCK
    forced = (blk == 0) | (blk == cur) | (blk == cur - 1)
    vals = jnp.where(blk <= cur, jnp.where(forced, BIG, imp), NEG)
    chosen = _rank_select(vals, blk, n_sel, min(N_SEL, n_sel), 0)
    selb_scr[...] = jnp.where(chosen, 0.0, NEG)

    def sel_bias(kt):
        rows = [jnp.broadcast_to(selb_scr[pl.ds(kt * blocks_per_tile + i, 1), :], (SEL_BLOCK, tq))
                for i in range(blocks_per_tile)]
        return _tile4(jnp.concatenate(rows, axis=0))

    def tile_rows(kt):
        return pl.ds(pl.multiple_of(kt * TK, TK), TK)

    def step(k_scr, v_scr, rows, bias, first):
        _tile_step(lambda h: k_scr[rows, :], lambda h: v_scr[rows, :], qst_scr, bias,
                   m_scr, l_scr, acc_scr, first, tq, 2 * tq)

    diag = tile_rows(qi)
    step(ks_scr, vs_scr, diag, sel_bias(qi) + _tri_bias(tq, True), True)

    def sel_body(kt, carry):
        step(ks_scr, vs_scr, tile_rows(kt), sel_bias(kt), False)
        return carry

    lax.fori_loop(0, qi, sel_body, 0)
    gate_row = lambda j: jnp.concatenate([gates_t[3 * h + j:3 * h + j + 1, :] for h in range(N_HEADS)], axis=1)
    out = gate_row(0) * ocmp_scr[...] + gate_row(1) * (acc_scr[...] / l_scr[...])

    step(kw_scr, vw_scr, diag, _tri_bias(tq, True), True)

    @pl.when(qi >= 1)
    def _():
        step(kw_scr, vw_scr, tile_rows(qi - 1), None, False)

    @pl.when(qi >= WINDOW // TK)
    def _():
        step(kw_scr, vw_scr, tile_rows(qi - WINDOW // TK), _tri_bias(tq, False), False)

    out = out + gate_row(2) * (acc_scr[...] / l_scr[...])
    o_ref[...] = jnp.concatenate([out[:, h * tq:(h + 1) * tq] for h in range(N_HEADS)], axis=0).T


def _cmp_to_sel(n_cmp, n_sel):
    ratio, span = SEL_BLOCK // CMP_STRIDE, CMP_LEN // CMP_STRIDE
    j, m, n = np.meshgrid(np.arange(n_sel), np.arange(ratio), np.arange(span), indexing="ij")
    i = ratio * j + m - n
    ok = (i >= 0) & (i < n_cmp)
    mat = np.zeros((n_cmp, n_sel), np.float32)
    np.add.at(mat, (i[ok], j[ok]), 1.0)
    return mat


def nsa_prompt(x, batch, seq, w1, pos, w2):
    nq = seq // TQ
    ncp = seq // CMP_STRIDE
    n_sel = seq // SEL_BLOCK
    imp_t = np.zeros((n_sel, ncp), np.float32)
    imp_t[:, :ncp - 1] = _cmp_to_sel(ncp - 1, n_sel).T
    full = lambda *shape: pl.BlockSpec(shape, lambda b, i: (0,) * len(shape))
    return pl.pallas_call(
        functools.partial(_nsa_prompt_body, seq=seq),
        grid=(batch, nq),
        in_specs=[pl.BlockSpec((TQ, NSA_W), lambda b, i: (b * nq + i, 0)),
                  pl.BlockSpec((seq, NSA_W), lambda b, i: (b, 0)),
                  full(CMP_LEN, LANES, 2 * CMP_HIDDEN), full(CMP_LEN, LANES),
                  full(2 * CMP_HIDDEN, LANES), full(n_sel, ncp)],
        out_specs=pl.BlockSpec((TQ, GROUP_WIDTH), lambda b, i: (b * nq + i, 0)),
        out_shape=jax.ShapeDtypeStruct((batch * seq, GROUP_WIDTH), F32),
        scratch_shapes=[pltpu.VMEM((seq + CMP_STRIDE, LANES), F32),
                        pltpu.VMEM((ncp, LANES), BF16), pltpu.VMEM((ncp, HEAD_DIM), BF16),
                        pltpu.VMEM((seq, LANES), BF16), pltpu.VMEM((seq, HEAD_DIM), BF16),
                        pltpu.VMEM((seq, LANES), BF16), pltpu.VMEM((seq, HEAD_DIM), BF16),
                        pltpu.VMEM((n_sel, TQ), F32),
                        pltpu.VMEM((N_HEADS * TQ, LANES), BF16), pltpu.VMEM((HEAD_DIM, N_HEADS * TQ), F32),
                        pltpu.VMEM((1, N_HEADS * TQ), F32), pltpu.VMEM((1, N_HEADS * TQ), F32),
                        pltpu.VMEM((HEAD_DIM, N_HEADS * TQ), F32)],
        compiler_params=_cparams(("parallel", "arbitrary")),
        name="nsa_prompt",
    )(x, x, w1, pos, w2, jnp.asarray(imp_t))


def _moba_prompt_body(q_ref, kv_ref, o_ref, km_scr, ka_scr, va_scr, selb_scr, qst_scr,
                      m_scr, l_scr, acc_scr, *, seq):
    qi = pl.program_id(1)
    tq = q_ref.shape[0]
    nb = seq // MOBA_BLOCK

    @pl.when(qi == 0)
    def _():
        for n in range(nb):
            km_scr[n:n + 1, :] = jnp.mean(kv_ref[n * MOBA_BLOCK:(n + 1) * MOBA_BLOCK, 256:512],
                                          axis=0, keepdims=True)

        kpos = lax.broadcasted_iota(jnp.int32, (seq, LANES), 0)
        for h in range(N_HEADS):
            pair = kv_ref[:, 256 + (h // 2) * LANES:256 + (h // 2 + 1) * LANES]
            if h % 2:
                pair = pltpu.roll(pair, HEAD_DIM, 1)
            ka_scr[h] = _key_tile(pair, kpos)
            va_scr[h] = kv_ref[:, 512 + h * HEAD_DIM:512 + (h + 1) * HEAD_DIM].astype(BF16)

    qpos = qi * tq + lax.broadcasted_iota(jnp.int32, (1, tq), 1)
    own = qpos // MOBA_BLOCK
    blk = lax.broadcasted_iota(jnp.int32, (nb, tq), 0)
    past = blk < own
    for h in range(N_HEADS):
        hs = slice(h * HEAD_DIM, (h + 1) * HEAD_DIM)
        gate = _hdot_nt(km_scr[:, hs], q_ref[:, hs])
        chosen = _rank_select(jnp.where(past, gate, NEG), blk, nb, min(MOBA_TOPK, nb), 0) & past
        selb_scr[h] = jnp.where(chosen, 0.0, NEG)

    qst_scr[...] = _query_stack(q_ref[:, 0:GROUP_WIDTH], MOBA_SLOPES)

    def step(rows, bias, first):
        _tile_step(lambda h: ka_scr[h, rows, :], lambda h: va_scr[h, rows, :], qst_scr, bias,
                   m_scr, l_scr, acc_scr, first, tq, tq)

    step(pl.ds(pl.multiple_of(qi * TK, TK), TK), _tri_bias(tq, True), True)

    def body(kt, carry):
        bias = jnp.concatenate([selb_scr[h, pl.ds(kt, 1), :] for h in range(N_HEADS)], axis=1)
        step(pl.ds(pl.multiple_of(kt * TK, TK), TK), bias, False)
        return carry

    lax.fori_loop(0, qi, body, 0)
    out = acc_scr[...] / l_scr[...]
    o_ref[...] = jnp.concatenate([out[:, h * tq:(h + 1) * tq] for h in range(N_HEADS)], axis=0).T


def moba_prompt(x, batch, seq):
    nq = seq // TQ
    nb = seq // MOBA_BLOCK
    return pl.pallas_call(
        functools.partial(_moba_prompt_body, seq=seq),
        grid=(batch, nq),
        in_specs=[pl.BlockSpec((TQ, MOBA_W), lambda b, i: (b * nq + i, 0)),
                  pl.BlockSpec((seq, MOBA_W), lambda b, i: (b, 0))],
        out_specs=pl.BlockSpec((TQ, GROUP_WIDTH), lambda b, i: (b * nq + i, 0)),
        out_shape=jax.ShapeDtypeStruct((batch * seq, GROUP_WIDTH), F32),
        scratch_shapes=[pltpu.VMEM((nb, GROUP_WIDTH), F32),
                        pltpu.VMEM((N_HEADS, seq, LANES), BF16), pltpu.VMEM((N_HEADS, seq, HEAD_DIM), BF16),
                        pltpu.VMEM((N_HEADS, nb, TQ), F32),
                        pltpu.VMEM((N_HEADS * TQ, LANES), BF16),
                        pltpu.VMEM((1, N_HEADS * TQ), F32), pltpu.VMEM((1, N_HEADS * TQ), F32),
                        pltpu.VMEM((HEAD_DIM, N_HEADS * TQ), F32)],
        compiler_params=_cparams(("parallel", "arbitrary")),
        name="moba_prompt",
    )(x, x)


def _softmax_rows(parts, masks):
    masked = [jnp.where(mk, s, NEG) for s, mk in zip(parts, masks)]
    m = masked[0].max(axis=1, keepdims=True)
    for s in masked[1:]:
        m = jnp.maximum(m, s.max(axis=1, keepdims=True))
    es = [jnp.where(mk, jnp.exp(s - m), 0.0) for s, mk in zip(masked, masks)]
    den = es[0].sum(axis=1, keepdims=True)
    for e in es[1:]:
        den = den + e.sum(axis=1, keepdims=True)
    inv = 1.0 / jnp.maximum(den, TINY)
    return [e * inv for e in es]


def _stack_heads(x, off):
    return jnp.concatenate([x[:, off + h * HEAD_DIM:off + (h + 1) * HEAD_DIM] for h in range(N_HEADS)], axis=0)


def _round_robin(gens):
    outs = [None] * len(gens)
    live = list(range(len(gens)))
    while live:
        for i in list(live):
            try:
                next(gens[i])
            except StopIteration as stop:
                outs[i] = stop.value
                live.remove(i)
    return outs


def _pad_rows(x, n):
    return jnp.concatenate([x, jnp.zeros((n - x.shape[0], x.shape[1]), x.dtype)], axis=0)


def _nsa_decode_body(*refs, n_pages, past_len, ns):
    x_ref = refs[1]
    all_pages = refs[2:2 + ns * n_pages]
    (win_ref, w1_ref, pos_ref, w2_ref, imp_ref, esel_ref, o_ref, xs_scr) = refs[2 + ns * n_pages:]
    steps = x_ref.shape[0] // ns
    ncp = past_len // CMP_STRIDE
    for i in range(ns):
        for p in range(n_pages):
            xs_scr[i, p * PAGE_SIZE:(p + 1) * PAGE_SIZE, :] = (
                all_pages[i * n_pages + p][0, 0, 0:2].reshape(LANES, PAGE_SIZE).T)
        xs_scr[i, past_len:past_len + CMP_STRIDE, :] = jnp.zeros((CMP_STRIDE, LANES), F32)
    kvc_all = _cmp_mlp([xs_scr.at[i] for i in range(ns)], pos_ref, w1_ref, w2_ref, ncp)
    outs = _round_robin([_nsa_decode_seq(
        x_ref[i * steps:(i + 1) * steps, :], all_pages[i * n_pages:(i + 1) * n_pages], win_ref, i,
        kvc_all[i * ncp:(i + 1) * ncp], imp_ref, esel_ref, n_pages, past_len) for i in range(ns)])
    for i in range(ns):
        o_ref[i * steps:(i + 1) * steps, :] = outs[i]


def _nsa_decode_seq(x, page_refs, win_ref, wi, kvc, imp_ref, esel_ref, n_pages, past_len):
    steps = x.shape[0]
    rows = N_HEADS * steps
    ncp = past_len // CMP_STRIDE
    n_cmp = ncp - 1
    tpos = past_len + lax.broadcasted_iota(jnp.int32, (steps, 1), 0)
    qpos = jnp.concatenate([tpos] * N_HEADS, axis=0)
    slope = jnp.concatenate([jnp.full((steps, 1), s, F32) for s in NSA_SLOPES], axis=0)
    qs = _stack_heads(x, 0) * (HEAD_DIM ** -0.5)
    gates = jax.nn.sigmoid(x[:, 640:768])
    gcol = [jnp.concatenate([gates[:, 3 * h + j:3 * h + j + 1] for h in range(N_HEADS)], axis=0)
            for j in range(3)]
    new_io = lax.broadcasted_iota(jnp.int32, (rows, LANES), 1)
    new_pos = past_len + new_io
    new_ok = (new_io < steps) & (new_pos <= qpos)

    k_cmp, v_cmp = kvc[:, 0:HEAD_DIM], kvc[:, HEAD_DIM:2 * HEAD_DIM]
    n_io = lax.broadcasted_iota(jnp.int32, (rows, ncp), 1)
    cend = n_io * CMP_STRIDE + (CMP_LEN - 1)
    (pc,) = _softmax_rows([_bdot_nt(qs, k_cmp) + slope * cend.astype(F32)],
                          [(cend <= qpos) & (n_io < n_cmp)])
    o_cmp = _bdot(pc, v_cmp)
    yield
    pc_sum = pc[0:steps]
    for h in range(1, N_HEADS):
        pc_sum = pc_sum + pc[h * steps:(h + 1) * steps]

    n_sel = -(-(past_len + steps) // SEL_BLOCK)
    imp = _hdot(pc_sum, imp_ref[...])
    yield
    blk = lax.broadcasted_iota(jnp.int32, (steps, LANES), 1)
    cur = tpos // SEL_BLOCK
    forced = (blk == 0) | (blk == cur) | (blk == cur - 1)
    vals = jnp.where((blk <= cur) & (blk < n_sel), jnp.where(forced, BIG, imp), NEG)
    chosen = _rank_select(vals, blk, n_sel, min(N_SEL, n_sel), 1).astype(F32)
    key_sel = _bdot(chosen, esel_ref[...])
    key_sel = jnp.concatenate([key_sel] * N_HEADS, axis=0) > 0.5
    yield

    cat_pages = lambda c: jnp.concatenate([page_refs[p][0, 0, c].astype(BF16) for p in range(n_pages)], axis=1)
    s_past = _bdot(qs, cat_pages(2))
    kpos = lax.broadcasted_iota(jnp.int32, (rows, past_len), 1)
    k_new = _pad_rows(x[:, 384:448], LANES)
    v_new = _pad_rows(x[:, 448:512], LANES)
    s_new = _bdot_nt(qs, k_new) + slope * new_pos.astype(F32)
    yield
    p_past, p_new = _softmax_rows([s_past + slope * kpos.astype(F32), s_new],
                                  [key_sel[:, 0:past_len] & (kpos <= qpos),
                                   key_sel[:, past_len:past_len + LANES] & new_ok])
    o_sel = _bdot(p_new, v_new) + _bdot_nt(p_past, cat_pages(3))
    yield

    n_win = win_ref.shape[-1]
    wpos = (past_len - n_win) + lax.broadcasted_iota(jnp.int32, (rows, n_win), 1)
    kw_new = _pad_rows(x[:, 512:576], LANES)
    vw_new = _pad_rows(x[:, 576:640], LANES)
    s_w = _bdot(qs, win_ref[0, wi, 0]) + slope * wpos.astype(F32)
    s_wn = _bdot_nt(qs, kw_new) + slope * new_pos.astype(F32)
    yield
    dist = qpos - wpos
    p_w, p_wn = _softmax_rows([s_w, s_wn],
                              [(dist >= 0) & (dist < WINDOW), new_ok & (qpos - new_pos < WINDOW)])
    o_win = _bdot_nt(p_w, win_ref[0, wi, 1]) + _bdot(p_wn, vw_new)

    out = gcol[0] * o_cmp + gcol[1] * o_sel + gcol[2] * o_win
    return jnp.concatenate([out[h * steps:(h + 1) * steps] for h in range(N_HEADS)], axis=1)


def nsa_decode(x, page_ids, cache_t, layer, win_t, w1, pos, w2, *, steps, past_len):
    nseq = x.shape[0] // steps
    n_pages = past_len // PAGE_SIZE
    ncp = past_len // CMP_STRIDE
    n_sel = -(-(past_len + steps) // SEL_BLOCK)
    imp = np.zeros((ncp, LANES), np.float32)
    imp[:ncp - 1, :n_sel] = _cmp_to_sel(ncp - 1, n_sel)
    kblk = np.concatenate([np.arange(past_len) // SEL_BLOCK,
                           (past_len + np.arange(LANES)) // SEL_BLOCK])
    esel = (np.arange(LANES)[:, None] == kblk[None, :]).astype(np.float32)
    n_win = win_t.shape[-1]

    ns = math.gcd(nseq, DECODE_SEQS)

    def page_spec(j):
        return pl.BlockSpec((1, 1, 4, HEAD_DIM, PAGE_SIZE),
                            lambda b, pt, j=j: (pt[b * (ns * n_pages) + j], layer, 0, 0, 0))

    full = lambda *shape: pl.BlockSpec(shape, lambda b, pt: (0,) * len(shape))
    return pl.pallas_call(
        functools.partial(_nsa_decode_body, n_pages=n_pages, past_len=past_len, ns=ns),
        grid_spec=pltpu.PrefetchScalarGridSpec(
            num_scalar_prefetch=1,
            grid=(nseq // ns,),
            in_specs=[pl.BlockSpec((ns * steps, NSA_W), lambda b, pt: (b, 0))]
            + [page_spec(j) for j in range(ns * n_pages)]
            + [pl.BlockSpec((1, ns, 2, HEAD_DIM, n_win), lambda b, pt: (layer, b, 0, 0, 0)),
               full(CMP_LEN, LANES, 2 * CMP_HIDDEN), full(CMP_LEN, LANES),
               full(2 * CMP_HIDDEN, LANES), full(ncp, LANES), full(LANES, past_len + LANES)],
            out_specs=pl.BlockSpec((ns * steps, GROUP_WIDTH), lambda b, pt: (b, 0)),
            scratch_shapes=[pltpu.VMEM((ns, past_len + CMP_STRIDE, LANES), F32)],
        ),
        out_shape=jax.ShapeDtypeStruct((nseq * steps, GROUP_WIDTH), F32),
        compiler_params=_cparams(("parallel",)),
        name="nsa_decode",
    )(page_ids, x, *([cache_t] * (ns * n_pages)), win_t, w1, pos, w2, jnp.asarray(imp), jnp.asarray(esel))


def _moba_decode_body(*refs, n_pages, past_len, ns):
    x_ref = refs[1]
    all_pages = refs[2:2 + ns * n_pages]
    emean_ref, eblk_ref, o_ref = refs[2 + ns * n_pages:]
    steps = x_ref.shape[0] // ns
    outs = _round_robin([_moba_decode_seq(
        x_ref[i * steps:(i + 1) * steps, :], all_pages[i * n_pages:(i + 1) * n_pages],
        emean_ref, eblk_ref, n_pages, past_len) for i in range(ns)])
    for i in range(ns):
        o_ref[i * steps:(i + 1) * steps, :] = outs[i]


def _moba_decode_seq(x, page_refs, emean_ref, eblk_ref, n_pages, past_len):
    steps = x.shape[0]
    rows = N_HEADS * steps
    tpos = past_len + lax.broadcasted_iota(jnp.int32, (steps, 1), 0)
    qpos = jnp.concatenate([tpos] * N_HEADS, axis=0)
    own = qpos // MOBA_BLOCK
    slope = jnp.concatenate([jnp.full((steps, 1), s, F32) for s in MOBA_SLOPES], axis=0)
    head_of_row = lax.broadcasted_iota(jnp.int32, (rows, GROUP_WIDTH), 0) // steps
    head_of_col = lax.broadcasted_iota(jnp.int32, (rows, GROUP_WIDTH), 1) // HEAD_DIM
    diag = head_of_row == head_of_col
    q_bd = jnp.where(diag, jnp.concatenate([x[:, 0:GROUP_WIDTH]] * N_HEADS, axis=0), 0.0)

    cat_pages = lambda c: jnp.concatenate(
        [page_refs[p][0, 0, c].reshape(GROUP_WIDTH, PAGE_SIZE).astype(BF16) for p in range(n_pages)], axis=1)
    kt_all = cat_pages(0)
    kmean_t = jnp.dot(kt_all, emean_ref[...], preferred_element_type=F32)
    yield
    gate = _hdot(q_bd, kmean_t)
    yield
    nb = -(-(past_len + steps) // MOBA_BLOCK)
    blk = lax.broadcasted_iota(jnp.int32, (rows, LANES), 1)
    past = (blk < own) & (blk < nb)
    vals = jnp.where(past, gate, NEG)
    chosen = (_rank_select(vals, blk, nb, min(MOBA_TOPK, nb), 1) & past).astype(F32)
    key_sel = _bdot(chosen, eblk_ref[...]) > 0.5
    yield

    q_sc = q_bd * (HEAD_DIM ** -0.5)
    s_past = _bdot(q_sc, kt_all)
    kpos = lax.broadcasted_iota(jnp.int32, (rows, past_len), 1)
    new_io = lax.broadcasted_iota(jnp.int32, (rows, LANES), 1)
    new_pos = past_len + new_io
    k_new = _pad_rows(x[:, 256:512], LANES)
    v_new = _pad_rows(x[:, 512:768], LANES)
    s_new = _bdot_nt(q_sc, k_new) + slope * new_pos.astype(F32)
    yield
    in_own_past = (kpos // MOBA_BLOCK) == own
    in_own_new = (new_pos // MOBA_BLOCK) == own
    p_past, p_new = _softmax_rows(
        [s_past + slope * kpos.astype(F32), s_new],
        [key_sel | in_own_past, (new_io < steps) & (new_pos <= qpos) & in_own_new])
    o_all = _bdot(p_new, v_new) + _bdot_nt(p_past, cat_pages(1))
    o_all = jnp.where(diag, o_all, 0.0)
    out = o_all[0:steps]
    for h in range(1, N_HEADS):
        out = out + o_all[h * steps:(h + 1) * steps]
    return out


def moba_decode(x, page_ids, cache_t, layer, *, steps, past_len):
    nseq = x.shape[0] // steps
    n_pages = past_len // PAGE_SIZE
    pages_per_blk = MOBA_BLOCK // PAGE_SIZE
    emean = np.zeros((n_pages, PAGE_SIZE, LANES), np.float32)
    for p in range(n_pages):
        emean[p, :, p // pages_per_blk] = 1.0 / MOBA_BLOCK
    eblk = (np.arange(LANES)[:, None] == (np.arange(past_len) // MOBA_BLOCK)[None, :]).astype(np.float32)

    ns = math.gcd(nseq, DECODE_SEQS)

    def page_spec(j):
        return pl.BlockSpec((1, 1, 2, N_HEADS, HEAD_DIM, PAGE_SIZE),
                            lambda b, pt, j=j: (pt[b * (ns * n_pages) + j], layer, 0, 0, 0, 0))

    full = lambda *shape: pl.BlockSpec(shape, lambda b, pt: (0,) * len(shape))
    return pl.pallas_call(
        functools.partial(_moba_decode_body, n_pages=n_pages, past_len=past_len, ns=ns),
        grid_spec=pltpu.PrefetchScalarGridSpec(
            num_scalar_prefetch=1,
            grid=(nseq // ns,),
            in_specs=[pl.BlockSpec((ns * steps, MOBA_W), lambda b, pt: (b, 0))]
            + [page_spec(j) for j in range(ns * n_pages)]
            + [full(past_len, LANES), full(LANES, past_len)],
            out_specs=pl.BlockSpec((ns * steps, GROUP_WIDTH), lambda b, pt: (b, 0)),
        ),
        out_shape=jax.ShapeDtypeStruct((nseq * steps, GROUP_WIDTH), F32),
        compiler_params=_cparams(("parallel",)),
        name="moba_decode",
    )(page_ids, x, *([cache_t] * (ns * n_pages)), jnp.asarray(emean.reshape(past_len, LANES), BF16),
      jnp.asarray(eblk))


def _regroup_w_in(w):
    d = w.shape[0]
    z = lambda n: jnp.zeros((d, n), w.dtype)
    gq, gk, gv, ga, gr = w[:, 0:128], w[:, 128:256], w[:, 256:512], w[:, 512:528], w[:, 528:784]
    ret = w[:, 784:1808]
    nq, nkv, ng = w[:, 1808:2064], w[:, 2064:2448], w[:, 2448:2460]
    moba = w[:, 2460:3228]
    return jnp.concatenate([gq, gk, gv, gr, ga, z(112), ret, nq, nkv, ng, z(116), moba], axis=1).astype(BF16)


def _cmp_weights(w1, pos, w2):
    w1r = w1.reshape(2, CMP_LEN, HEAD_DIM, CMP_HIDDEN)
    zero = jnp.zeros((CMP_LEN, HEAD_DIM, CMP_HIDDEN), w1.dtype)
    w1bd = jnp.concatenate([jnp.concatenate([w1r[0], zero], axis=2),
                            jnp.concatenate([zero, w1r[1]], axis=2)], axis=1)
    zero2 = jnp.zeros((CMP_HIDDEN, HEAD_DIM), w2.dtype)
    w2bd = jnp.concatenate([jnp.concatenate([w2[0], zero2], axis=1),
                            jnp.concatenate([zero2, w2[1]], axis=1)], axis=0)
    posf = jnp.concatenate([pos[0], pos[1]], axis=1)
    return w1bd.astype(BF16), posf, w2bd.astype(BF16)


def kernel(x_prompt, x_sample, cache_nsa, cache_moba, cache_nsa_win, state_gla, state_ret, page_table, w_in, gla_w_a2, gla_b_a, gla_norm, ret_norm, nsa_cmp_w1, nsa_cmp_w2, nsa_cmp_pos, w_out, ffn_w_up, ffn_w_down, ln_g, ln_b):
    bp, sp, d = x_prompt.shape
    bs, ss, _ = x_sample.shape
    past_len = page_table.shape[1] * PAGE_SIZE
    n_win = cache_nsa_win.shape[2]

    nsa_t = jnp.transpose(cache_nsa, (0, 1, 3, 4, 2))
    moba_t = jnp.transpose(cache_moba, (0, 1, 3, 4, 5, 2))
    win_t = jnp.transpose(cache_nsa_win, (0, 1, 3, 4, 2))
    gla_t = jnp.transpose(state_gla, (0, 2, 3, 4, 1)).reshape(DEPTH, N_HEADS * GLA_DK, HEAD_DIM, bs)
    ret_t = jnp.transpose(state_ret, (0, 2, 3, 4, 1)).reshape(DEPTH, N_HEADS * HEAD_DIM, HEAD_DIM, bs)
    page_ids = page_table.reshape(-1)

    log_gamma = np.log1p(-np.power(2.0, -5.0 - np.arange(N_HEADS, dtype=np.float64))).astype(np.float32)
    la_ret = jnp.asarray(np.repeat(log_gamma, HEAD_DIM))
    zeros128 = jnp.zeros((1, 128), F32)

    xp = x_prompt.reshape(bp * sp, d)
    xs = x_sample.reshape(bs * ss, d)
    outs = {k: [] for k in ("nsa_p", "moba_p", "win_p", "gla_p", "ret_p",
                            "nsa_s", "moba_s", "win_s", "gla_s", "ret_s")}
    for l in range(DEPTH):
        wu = ffn_w_up[l].astype(BF16)
        wd = ffn_w_down[l].astype(BF16)
        wi = _regroup_w_in(w_in[l])
        wo = w_out[l].astype(BF16)
        g = ln_g[l].reshape(3, 1, d)
        b = ln_b[l].reshape(3, 1, d)
        wa = jnp.zeros((128, 128), F32).at[0:GLA_RANK, :].set(gla_w_a2[l])
        ba = gla_b_a[l].reshape(1, 128)
        gn = gla_norm[l].reshape(1, GROUP_WIDTH)
        rn = ret_norm[l].reshape(1, GROUP_WIDTH)
        w1bd, posf, w2bd = _cmp_weights(nsa_cmp_w1[l], nsa_cmp_pos[l], nsa_cmp_w2[l])

        xp = ffn_ln(xp, wu[0], wd[0], g[0], b[0])
        pg, pr, pn, pm = proj(xp, wi)
        o_gla, st_gla = gla_prompt(pg, bp, sp, wa, ba, gn)
        o_ret, st_ret = ret_prompt(pr, bp, sp, rn)
        o_nsa = nsa_prompt(pn, bp, sp, w1bd, posf, w2bd)
        o_moba = moba_prompt(pm, bp, sp)
        xp = out_ln(xp, (o_gla, o_ret, o_nsa, o_moba), wo, g[1], b[1])
        xp = ffn_ln(xp, wu[1], wd[1], g[2], b[2])
        pn3 = pn.reshape(bp, sp, NSA_W)
        outs["nsa_p"].append(pn3[:, :, 256:512].reshape(bp, sp, 4, HEAD_DIM))
        outs["moba_p"].append(pm.reshape(bp, sp, MOBA_W)[:, :, 256:768].reshape(bp, sp, 2, N_HEADS, HEAD_DIM))
        keep = min(WINDOW, sp)
        outs["win_p"].append(pn3[:, sp - keep:, 512:640].reshape(bp, keep, 2, HEAD_DIM))
        outs["gla_p"].append(st_gla)
        outs["ret_p"].append(st_ret)

        xs = ffn_ln(xs, wu[0], wd[0], g[0], b[0])
        pg, pr, pn, pm = proj(xs, wi)
        to_lanes = lambda a: jnp.transpose(a.reshape(bs, ss, a.shape[1]), (1, 2, 0))
        og_t, sg_t = lin_decode(to_lanes(pg), gla_t[l], "gla", wa.T,
                                ba.reshape(128, 1), jnp.zeros((128, 1), F32), gn.reshape(GROUP_WIDTH, 1))
        or_t, sr_t = lin_decode(to_lanes(pr), ret_t[l], "ret", jnp.zeros((128, 128), F32),
                                jnp.zeros((128, 1), F32), la_ret.reshape(256, 1), rn.reshape(GROUP_WIDTH, 1))
        from_lanes = lambda a: jnp.transpose(a, (2, 0, 1)).reshape(bs * ss, GROUP_WIDTH)
        o_nsa = nsa_decode(pn, page_ids, nsa_t, l, win_t, w1bd, posf, w2bd, steps=ss, past_len=past_len)
        o_moba = moba_decode(pm, page_ids, moba_t, l, steps=ss, past_len=past_len)
        xs = out_ln(xs, (from_lanes(og_t), from_lanes(or_t), o_nsa, o_moba), wo, g[1], b[1])
        xs = ffn_ln(xs, wu[1], wd[1], g[2], b[2])
        pn3 = pn.reshape(bs, ss, NSA_W)
        outs["nsa_s"].append(pn3[:, :, 256:512].reshape(bs, ss, 4, HEAD_DIM))
        outs["moba_s"].append(pm.reshape(bs, ss, MOBA_W)[:, :, 256:768].reshape(bs, ss, 2, N_HEADS, HEAD_DIM))
        outs["win_s"].append(pn3[:, :, 512:640].reshape(bs, ss, 2, HEAD_DIM))
        outs["gla_s"].append(jnp.transpose(sg_t.reshape(N_HEADS, GLA_DK, HEAD_DIM, bs), (3, 0, 1, 2)))
        outs["ret_s"].append(jnp.transpose(sr_t.reshape(N_HEADS, HEAD_DIM, HEAD_DIM, bs), (3, 0, 1, 2)))

    new_win = jnp.stack(outs["win_s"], axis=0)
    if n_win > 0:
        win_s = jnp.concatenate([cache_nsa_win[:, :, ss:], new_win], axis=2)
    else:
        win_s = new_win[:, :, ss - min(WINDOW, ss):]
    return (xp.reshape(bp, sp, d), xs.reshape(bs, ss, d),
            jnp.stack(outs["nsa_p"], axis=1), jnp.stack(outs["moba_p"], axis=1),
            jnp.stack(outs["win_p"], axis=0), jnp.stack(outs["gla_p"], axis=0), jnp.stack(outs["ret_p"], axis=0),
            jnp.stack(outs["nsa_s"], axis=1), jnp.stack(outs["moba_s"], axis=1),
            win_s, jnp.stack(outs["gla_s"], axis=0), jnp.stack(outs["ret_s"], axis=0))
```

```python
import functools
import math

import numpy as np
import jax
import jax.numpy as jnp
from jax import lax
from jax.experimental import pallas as pl
from jax.experimental.pallas import tpu as pltpu

F32 = jnp.float32
BF16 = jnp.bfloat16
HI = lax.Precision.HIGHEST

D_MODEL = 1024
DEPTH = 4
PAGE_SIZE = 128
HEAD_DIM = 64
N_HEADS = 4
GROUP_WIDTH = 256
GLA_DK = 32
GLA_RANK = 16
GLA_TAU = 16.0
CMP_LEN = 32
CMP_STRIDE = 16
CMP_HIDDEN = 128
SEL_BLOCK = 64
N_SEL = 8
WINDOW = 512
MOBA_BLOCK = 256
MOBA_TOPK = 3
D_FF = 2816
ALPHA = (2 * DEPTH) ** 0.25
LN_EPS = 1e-5
NEG = -1e30
BIG = 1e30
TINY = 1e-30

LANES = 128
VMEM_LIMIT = 56 * 1024 * 1024

GLA_W = 896
RET_W = 1024
NSA_W = 768
MOBA_W = 768
PROJ_SPLITS = (GLA_W, RET_W, 256, 256, 256, 256, 512)


def _slopes():
    n = 2 * N_HEADS
    s = [2.0 ** (-8.0 * i / n) for i in range(1, n + 1)]
    return s[0::2], s[1::2]


NSA_SLOPES, MOBA_SLOPES = _slopes()


def _bdot(a, b):
    return jnp.dot(a.astype(BF16), b.astype(BF16), preferred_element_type=F32)


def _bdot_nt(a, b):
    return lax.dot_general(a.astype(BF16), b.astype(BF16), (((1,), (1,)), ((), ())),
                           preferred_element_type=F32)


def _bdot_tn(a, b):
    return lax.dot_general(a.astype(BF16), b.astype(BF16), (((0,), (0,)), ((), ())),
                           preferred_element_type=F32)


def _hdot(a, b):
    return jnp.dot(a, b, precision=HI, preferred_element_type=F32)


def _hdot_nt(a, b):
    return lax.dot_general(a, b, (((1,), (1,)), ((), ())), precision=HI, preferred_element_type=F32)


def _hdot_tn(a, b):
    return lax.dot_general(a, b, (((0,), (0,)), ((), ())), precision=HI, preferred_element_type=F32)


def _split3(a):
    hi = a.astype(BF16)
    r1 = a - hi.astype(F32)
    mid = r1.astype(BF16)
    lo = (r1 - mid.astype(F32)).astype(BF16)
    return hi, mid, lo


def _xdot(a, c, dims=(((1,), (0,)), ((), ()))):
    cb = c.astype(BF16)
    return sum(lax.dot_general(part, cb, dims, preferred_element_type=F32) for part in _split3(a))


def _xdot_left(c, a):
    cb = c.astype(BF16)
    return sum(jnp.dot(cb, part, preferred_element_type=F32) for part in _split3(a))


def _layer_norm(y, g, b):
    mu = jnp.mean(y, axis=-1, keepdims=True)
    d = y - mu
    var = jnp.mean(d * d, axis=-1, keepdims=True)
    return d * lax.rsqrt(var + LN_EPS) * g + b


def _cparams(sem):
    return pltpu.CompilerParams(dimension_semantics=sem, vmem_limit_bytes=VMEM_LIMIT)


def _ffn_ln_body(x_ref, wu_ref, wg_ref, wd_ref, g_ref, b_ref, o_ref, acc_ref):
    j = pl.program_id(1)

    @pl.when(j == 0)
    def _():
        acc_ref[...] = jnp.zeros_like(acc_ref)

    xb = x_ref[...].astype(BF16)
    u = jnp.dot(xb, wu_ref[...], preferred_element_type=F32)
    gt = jnp.dot(xb, wg_ref[...], preferred_element_type=F32)
    a = (gt * jax.nn.sigmoid(gt) * u).astype(BF16)
    acc_ref[...] += jnp.dot(a, wd_ref[...], preferred_element_type=F32)

    @pl.when(j == pl.num_programs(1) - 1)
    def _():
        y = ALPHA * x_ref[...] + 0.5 * acc_ref[...]
        o_ref[...] = _layer_norm(y, g_ref[...], b_ref[...])


def ffn_ln(x, w_up, w_down, layer, half, g, b, *, tm=512, tf=1408):
    m, d = x.shape
    f = w_down.shape[2]
    tm = min(tm, m)
    nf = f // tf
    return pl.pallas_call(
        _ffn_ln_body,
        grid=(m // tm, nf),
        in_specs=[
            pl.BlockSpec((tm, d), lambda i, j: (i, 0)),
            pl.BlockSpec((None, None, d, tf), lambda i, j: (layer, half, 0, j)),
            pl.BlockSpec((None, None, d, tf), lambda i, j: (layer, half, 0, j + nf)),
            pl.BlockSpec((None, None, tf, d), lambda i, j: (layer, half, j, 0)),
            pl.BlockSpec((1, d), lambda i, j: (0, 0)),
            pl.BlockSpec((1, d), lambda i, j: (0, 0)),
        ],
        out_specs=pl.BlockSpec((tm, d), lambda i, j: (i, 0)),
        out_shape=jax.ShapeDtypeStruct((m, d), F32),
        scratch_shapes=[pltpu.VMEM((tm, d), F32)],
        compiler_params=_cparams(("parallel", "arbitrary")),
        name="ffn_ln",
    )(x, w_up, w_up, w_down, g, b)


def _proj_body(x_ref, w_ref, *o_refs):
    xb = x_ref[...].astype(BF16)
    off = 0
    for o_ref in o_refs:
        wdt = o_ref.shape[1]
        o_ref[...] = jnp.dot(xb, w_ref[:, off:off + wdt], preferred_element_type=F32)
        off += wdt


def proj(x, w, layer, *, tm=512):
    m, d = x.shape
    n = w.shape[2]
    tm = min(tm, m)
    return pl.pallas_call(
        _proj_body,
        grid=(m // tm,),
        in_specs=[pl.BlockSpec((tm, d), lambda i: (i, 0)),
                  pl.BlockSpec((None, d, n), lambda i: (layer, 0, 0))],
        out_specs=[pl.BlockSpec((tm, wdt), lambda i: (i, 0)) for wdt in PROJ_SPLITS],
        out_shape=[jax.ShapeDtypeStruct((m, wdt), F32) for wdt in PROJ_SPLITS],
        compiler_params=_cparams(("parallel",)),
        name="proj",
    )(x, w)


def _out_ln_body(x_ref, o0_ref, o1_ref, o2_ref, o3_ref, w_ref, g_ref, b_ref, y_ref):
    mix = None
    for gi, o_ref in enumerate((o0_ref, o1_ref, o2_ref, o3_ref)):
        part = jnp.dot(o_ref[...].astype(BF16), w_ref[gi * GROUP_WIDTH:(gi + 1) * GROUP_WIDTH, :],
                       preferred_element_type=F32)
        mix = part if mix is None else mix + part
    y_ref[...] = _layer_norm(ALPHA * x_ref[...] + mix, g_ref[...], b_ref[...])


def out_ln(x, outs, w_out, layer, g, b, *, tm=512):
    m, d = x.shape
    tm = min(tm, m)
    row = lambda i: (i, 0)
    return pl.pallas_call(
        _out_ln_body,
        grid=(m // tm,),
        in_specs=[pl.BlockSpec((tm, d), row)]
        + [pl.BlockSpec((tm, GROUP_WIDTH), row) for _ in range(4)]
        + [pl.BlockSpec((None, d, d), lambda i: (layer, 0, 0)),
           pl.BlockSpec((1, d), lambda i: (0, 0)),
           pl.BlockSpec((1, d), lambda i: (0, 0))],
        out_specs=pl.BlockSpec((tm, d), row),
        out_shape=jax.ShapeDtypeStruct((m, d), F32),
        compiler_params=_cparams(("parallel",)),
        name="out_ln",
    )(x, *outs, w_out, g, b)


def _head_group_norm(o, jn, gain):
    mu = _xdot(o, jn)
    d = o - mu
    var = _xdot(d * d, jn)
    return d * lax.rsqrt(var + LN_EPS) * gain


GLA_CHUNK = 16
GLA_ROWS = 128
RET_CHUNK = 128


def _gla_prompt_body(x_ref, wa_ref, ba_ref, gain_ref, tri_ref, tot_ref, emask_ref, jm_ref, jn_ref,
                     o_ref, st_ref, s_scr):
    c, r = GLA_CHUNK, GLA_ROWS
    g = r // c
    kd = 128
    seq = x_ref.shape[0]
    s_scr[...] = jnp.zeros_like(s_scr)
    jm = jm_ref[...]
    si = lax.broadcasted_iota(jnp.int32, (g, c, c, kd), 1)
    ti = lax.broadcasted_iota(jnp.int32, (g, c, c, kd), 2)
    causal = ti >= si

    def step(i, carry):
        rows = pl.ds(pl.multiple_of(i * r, r), r)
        q = x_ref[rows, 0:128] * (GLA_DK ** -0.5)
        k = x_ref[rows, 128:256]
        v = x_ref[rows, 256:512]
        gate = x_ref[rows, 512:768]
        la = jax.nn.log_sigmoid(_hdot(x_ref[rows, 768:896], wa_ref[...]) + ba_ref[...]) / GLA_TAU
        bt = _hdot(tri_ref[...], la)
        btot = _hdot(tot_ref[...], la)
        bt4 = bt.reshape(g, c, kd)
        q4 = q.reshape(g, c, kd)
        k4 = k.reshape(g, c, kd)
        dlt = jnp.minimum(bt4[:, None, :, :] - bt4[:, :, None, :], 0.0)
        w = jnp.where(causal, q4[:, None, :, :] * k4[:, :, None, :] * jnp.exp(dlt), 0.0)
        z = _bdot(w.reshape(g * c * c, kd), jm).reshape(g, c, c, GROUP_WIDTH)
        o = jnp.sum(z * v.reshape(g, c, GROUP_WIDTH)[:, :, None, :], axis=1).reshape(r, GROUP_WIDTH)
        qd = q * jnp.exp(bt)
        kdn = k * jnp.exp(btot - bt)
        edec = jnp.exp(_hdot_tn(la, emask_ref[...]))
        s_cur = s_scr[...]
        inter = []
        for j in range(g):
            rs = slice(j * c, (j + 1) * c)
            inter.append(_bdot(qd[rs], s_cur))
            upd = jnp.where(jm > 0, _bdot_tn(kdn[rs], v[rs]), 0.0)
            s_cur = jnp.broadcast_to(edec[:, j:j + 1], (kd, GROUP_WIDTH)) * s_cur + upd
        s_scr[...] = s_cur
        o = o + jnp.concatenate(inter, axis=0)
        o_ref[rows, :] = _head_group_norm(o, jn_ref[...], gain_ref[...]) * (gate * jax.nn.sigmoid(gate))
        return carry

    lax.fori_loop(0, seq // r, step, 0)
    s_fin = s_scr[...]
    for h in range(N_HEADS):
        st_ref[0, h] = s_fin[h * GLA_DK:(h + 1) * GLA_DK, h * HEAD_DIM:(h + 1) * HEAD_DIM]


def _ret_prompt_body(x_ref, gain_ref, dstack_ref, qdec_ref, kdec_ref, sdec_ref, hm_ref, jm_ref, jn_ref,
                     o_ref, st_ref, s_scr):
    c = RET_CHUNK
    seq = x_ref.shape[0]
    s_scr[...] = jnp.zeros_like(s_scr)

    def step(i, carry):
        rows = pl.ds(pl.multiple_of(i * c, c), c)
        q = x_ref[rows, 0:256]
        k = x_ref[rows, 256:512] * (HEAD_DIM ** -0.5)
        v = x_ref[rows, 512:768]
        gate = x_ref[rows, 768:1024]
        hm = hm_ref[...]
        qs = jnp.concatenate([q] * N_HEADS, axis=0) * hm
        sc = _bdot_nt(qs, k) * dstack_ref[...]
        of = _bdot(sc, v) * hm
        o = of[0:c]
        for h in range(1, N_HEADS):
            o = o + of[h * c:(h + 1) * c]
        s_old = s_scr[...]
        o = o + _bdot(q * qdec_ref[...], s_old)
        upd = jnp.where(jm_ref[...] > 0, _bdot_tn(k * kdec_ref[...], v), 0.0)
        s_scr[...] = sdec_ref[...] * s_old + upd
        o_ref[rows, :] = _head_group_norm(o, jn_ref[...], gain_ref[...]) * (gate * jax.nn.sigmoid(gate))
        return carry

    lax.fori_loop(0, seq // c, step, 0)
    s_fin = s_scr[...]
    for h in range(N_HEADS):
        st_ref[0, h] = s_fin[h * HEAD_DIM:(h + 1) * HEAD_DIM, h * HEAD_DIM:(h + 1) * HEAD_DIM]


def _blockdiag(kd, dk):
    r = np.arange(kd)[:, None] // dk
    cidx = np.arange(GROUP_WIDTH)[None, :] // HEAD_DIM
    return (r == cidx).astype(np.float32)


def _lin_prompt_call(body, name, x, batch, seq, kd, dk, consts):
    full = lambda a: pl.BlockSpec(a.shape, lambda b: (0,) * a.ndim)
    return pl.pallas_call(
        body,
        grid=(batch,),
        in_specs=[pl.BlockSpec((seq, x.shape[1]), lambda b: (b, 0))] + [full(a) for a in consts],
        out_specs=[pl.BlockSpec((seq, GROUP_WIDTH), lambda b: (b, 0)),
                   pl.BlockSpec((1, N_HEADS, dk, HEAD_DIM), lambda b: (b, 0, 0, 0))],
        out_shape=[jax.ShapeDtypeStruct((batch * seq, GROUP_WIDTH), F32),
                   jax.ShapeDtypeStruct((batch, N_HEADS, dk, HEAD_DIM), F32)],
        scratch_shapes=[pltpu.VMEM((kd, GROUP_WIDTH), F32)],
        compiler_params=_cparams(("parallel",)),
        name=name,
    )(x, *consts)


def gla_prompt(x, batch, seq, wa, ba, gain):
    r, c = GLA_ROWS, GLA_CHUNK
    same = (np.arange(r)[:, None] // c) == (np.arange(r)[None, :] // c)
    tri = (same & (np.arange(r)[:, None] >= np.arange(r)[None, :])).astype(np.float32)
    emask = ((np.arange(r)[:, None] // c) == np.arange(LANES)[None, :]).astype(np.float32)
    consts = [wa, ba, gain, jnp.asarray(tri), jnp.asarray(same.astype(np.float32)), jnp.asarray(emask),
              jnp.asarray(_blockdiag(128, GLA_DK)), jnp.asarray(_blockdiag(GROUP_WIDTH, HEAD_DIM) / HEAD_DIM)]
    return _lin_prompt_call(_gla_prompt_body, "gla_prompt", x, batch, seq, 128, GLA_DK, consts)


def ret_prompt(x, batch, seq, gain):
    c = RET_CHUNK
    log_gamma = np.log1p(-np.power(2.0, -5.0 - np.arange(N_HEADS, dtype=np.float64)))
    t = np.arange(c)
    diff = t[:, None] - t[None, :]
    dstack = np.concatenate([np.where(diff >= 0, np.exp(lg * np.maximum(diff, 0)), 0.0) for lg in log_gamma], axis=0)
    per_lane = np.repeat(log_gamma, HEAD_DIM)
    qdec = np.exp(per_lane[None, :] * (t[:, None] + 1))
    kdec = np.exp(per_lane[None, :] * (c - 1 - t[:, None]))
    sdec = np.broadcast_to(np.exp(per_lane * c)[:, None], (GROUP_WIDTH, GROUP_WIDTH))
    hm = np.concatenate([np.broadcast_to(np.arange(GROUP_WIDTH)[None, :] // HEAD_DIM == h, (c, GROUP_WIDTH))
                         for h in range(N_HEADS)], axis=0)
    as_f32 = lambda a: jnp.asarray(np.asarray(a, np.float32))
    consts = [gain, as_f32(dstack), as_f32(qdec), as_f32(kdec), as_f32(sdec), as_f32(hm),
              jnp.asarray(_blockdiag(GROUP_WIDTH, HEAD_DIM)),
              jnp.asarray(_blockdiag(GROUP_WIDTH, HEAD_DIM) / HEAD_DIM)]
    return _lin_prompt_call(_ret_prompt_body, "ret_prompt", x, batch, seq, GROUP_WIDTH, HEAD_DIM, consts)


def _lin_decode_body(x_ref, s0_ref, wa_ref, ba_ref, la_ref, gain_ref, o_ref, s1_ref, a_scr, o_scr,
                     *, kind, kd, dk):
    steps = x_ref.shape[0]
    if kind == "gla":
        qo, ko, vo, go = 0, 128, 256, 512
        for t in range(steps):
            pre = _hdot(wa_ref[...], x_ref[t, 768:896, :]) + ba_ref[...]
            a_scr[t] = jnp.exp(jax.nn.log_sigmoid(pre) / GLA_TAU)
        qscale, kscale = GLA_DK ** -0.5, 1.0
    else:
        qo, ko, vo, go = 0, 256, 512, 768
        for t in range(steps):
            a_scr[t] = jnp.exp(jnp.broadcast_to(la_ref[...], a_scr.shape[1:]))
        qscale, kscale = 1.0, HEAD_DIM ** -0.5
    o_scr[...] = jnp.zeros_like(o_scr)

    def body(j8, carry):
        r0 = pl.multiple_of(j8 * 8, 8)
        h = r0 // dk
        vrows = pl.ds(pl.multiple_of(vo + h * HEAD_DIM, HEAD_DIM), HEAD_DIM)
        orows = pl.ds(pl.multiple_of(h * HEAD_DIM, HEAD_DIM), HEAD_DIM)
        a8 = [a_scr[t, pl.ds(r0, 8), :] for t in range(steps)]
        q8 = [x_ref[t, pl.ds(qo + r0, 8), :] * qscale for t in range(steps)]
        k8 = [x_ref[t, pl.ds(ko + r0, 8), :] * kscale for t in range(steps)]
        for jj in range(8):
            sj = s0_ref[r0 + jj]
            for t in range(steps):
                vt = x_ref[t, vrows, :]
                sj = a8[t][jj:jj + 1, :] * sj + k8[t][jj:jj + 1, :] * vt
                o_scr[t, orows, :] += q8[t][jj:jj + 1, :] * sj
            s1_ref[r0 + jj] = sj
        return carry

    lax.fori_loop(0, kd // 8, body, 0)
    for t in range(steps):
        for h in range(N_HEADS):
            blk = slice(h * HEAD_DIM, (h + 1) * HEAD_DIM)
            o = o_scr[t, blk, :]
            mu = jnp.mean(o, axis=0, keepdims=True)
            d = o - mu
            var = jnp.mean(d * d, axis=0, keepdims=True)
            gate = x_ref[t, go + h * HEAD_DIM:go + (h + 1) * HEAD_DIM, :]
            o_ref[t, blk, :] = d * lax.rsqrt(var + LN_EPS) * gain_ref[blk, :] * (gate * jax.nn.sigmoid(gate))


def lin_decode(xt, s0, kind, wa_t, ba_col, la_col, gain_col):
    kd, dk = (128, GLA_DK) if kind == "gla" else (256, HEAD_DIM)
    steps, _, nb = xt.shape
    return pl.pallas_call(
        functools.partial(_lin_decode_body, kind=kind, kd=kd, dk=dk),
        out_shape=[jax.ShapeDtypeStruct((steps, GROUP_WIDTH, nb), F32),
                   jax.ShapeDtypeStruct((kd, HEAD_DIM, nb), F32)],
        scratch_shapes=[pltpu.VMEM((steps, kd, nb), F32), pltpu.VMEM((steps, GROUP_WIDTH, nb), F32)],
        compiler_params=pltpu.CompilerParams(vmem_limit_bytes=VMEM_LIMIT),
        name="lin_decode_" + kind,
    )(xt, s0, wa_t, ba_col, la_col, gain_col)


def _cmp_mlp(xs_list, pos_ref, w1_ref, w2_ref, n_rows):
    xcat = jnp.concatenate(
        [jnp.concatenate([(xs[pl.ds(r, n_rows, stride=CMP_STRIDE), :] + pos_ref[r:r + 1, :]).astype(BF16)
                          for r in range(CMP_LEN)], axis=1) for xs in xs_list], axis=0)
    hid = jnp.dot(xcat, w1_ref[...].reshape(CMP_LEN * LANES, 2 * CMP_HIDDEN), preferred_element_type=F32)
    return _bdot(jax.nn.gelu(hid), w2_ref[...])


def _rank_select(vals, idx, n_rows, n_top, axis):
    cnt = jnp.zeros(vals.shape, F32)
    for j in range(n_rows):
        vj = vals[j:j + 1, :] if axis == 0 else vals[:, j:j + 1]
        before = (vj > vals) | ((vj == vals) & (j < idx))
        cnt = cnt + before.astype(F32)
    return cnt < n_top


DECODE_SEQS = 4
TQ = 256
TK = 256


POS_HI_LANE = HEAD_DIM
POS_LO_LANE = HEAD_DIM + 1


def _nt(a, b):
    return lax.dot_general(a, b, (((1,), (1,)), ((), ())), preferred_element_type=F32)


def _tn(a, b):
    return lax.dot_general(a, b, (((0,), (0,)), ((), ())), preferred_element_type=F32)


def _key_tile(x128, pos):
    lane = lax.broadcasted_iota(jnp.int32, x128.shape, 1)
    feat = jnp.where(lane == POS_HI_LANE, (pos // 64).astype(F32),
                     jnp.where(lane == POS_LO_LANE, (pos % 64).astype(F32), 0.0))
    return jnp.where(lane < HEAD_DIM, x128, feat).astype(BF16)


def _query_stack(q256, slopes):
    tq = q256.shape[0]
    lane = lax.broadcasted_iota(jnp.int32, (tq, LANES), 1)
    parts = []
    for h in range(N_HEADS):
        pair = q256[:, (h // 2) * LANES:(h // 2 + 1) * LANES]
        if h % 2:
            pair = pltpu.roll(pair, HEAD_DIM, 1)
        const = jnp.where(lane == POS_HI_LANE, 64.0 * slopes[h], jnp.where(lane == POS_LO_LANE, slopes[h], 0.0))
        parts.append(jnp.where(lane < HEAD_DIM, pair * (HEAD_DIM ** -0.5), const).astype(BF16))
    return jnp.concatenate(parts, axis=0)


def _tile_step(k_fn, v_fn, qst_scr, bias, m_scr, l_scr, acc_scr, first, tq, group):
    starts = list(range(0, N_HEADS * tq, group))
    scores = []
    for c0 in starts:
        s = _nt(k_fn(c0 // tq), qst_scr[c0:c0 + group, :])
        scores.append(s if bias is None else s + bias[:, c0:c0 + group])
    probs, stats = [], []
    for c0, s in zip(starts, scores):
        cols = slice(c0, c0 + group)
        if first:
            m_new = jnp.max(s, axis=0, keepdims=True)
            p = jnp.exp(s - m_new)
            stats.append((m_new, jnp.sum(p, axis=0, keepdims=True), None))
        else:
            m_old = m_scr[:, cols]
            m_new = jnp.maximum(m_old, jnp.max(s, axis=0, keepdims=True))
            p = jnp.exp(s - m_new)
            corr = jnp.exp(m_old - m_new)
            stats.append((m_new, corr * l_scr[:, cols] + jnp.sum(p, axis=0, keepdims=True), corr))
        probs.append(p.astype(BF16))
    for c0, p, (m_new, l_new, corr) in zip(starts, probs, stats):
        cols = slice(c0, c0 + group)
        pv = _tn(v_fn(c0 // tq), p)
        acc_scr[:, cols] = pv if first else corr * acc_scr[:, cols] + pv
        m_scr[:, cols] = m_new
        l_scr[:, cols] = l_new


def _tile4(x):
    return jnp.concatenate([x] * N_HEADS, axis=1)


def _tri_bias(tq, keep_lower):
    row = lax.broadcasted_iota(jnp.int32, (TK, tq), 0)
    col = lax.broadcasted_iota(jnp.int32, (TK, tq), 1)
    return _tile4(jnp.where((row <= col) if keep_lower else (row > col), 0.0, NEG))


def _nsa_prompt_body(q_ref, gate_ref, kv_ref, win_ref, w1_ref, pos_ref, w2_ref, imp_ref, o_ref,
                     xs_scr, kc_scr, vc_scr, ks_scr, vs_scr, kw_scr, vw_scr, selb_scr,
                     qst_scr, ocmp_scr, m_scr, l_scr, acc_scr, *, seq):
    qi = pl.program_id(1)
    tq = q_ref.shape[0]
    ncp = seq // CMP_STRIDE
    n_cmp = ncp - 1
    n_sel = seq // SEL_BLOCK
    blocks_per_tile = TK // SEL_BLOCK

    @pl.when(qi == 0)
    def _():
        xs_scr[0:seq, :] = kv_ref[:, 0:128]
        xs_scr[seq:seq + CMP_STRIDE, :] = jnp.zeros((CMP_STRIDE, LANES), F32)
        kvc = _cmp_mlp([xs_scr], pos_ref, w1_ref, w2_ref, ncp)
        cend = lax.broadcasted_iota(jnp.int32, (ncp, LANES), 0) * CMP_STRIDE + (CMP_LEN - 1)
        kc_scr[...] = _key_tile(kvc, cend)
        vc_scr[...] = kvc[:, HEAD_DIM:2 * HEAD_DIM].astype(BF16)
        kpos = lax.broadcasted_iota(jnp.int32, (seq, LANES), 0)
        ks_scr[...] = _key_tile(kv_ref[:, 128:256], kpos)
        vs_scr[...] = kv_ref[:, 192:256].astype(BF16)
        kw_scr[...] = _key_tile(win_ref[:, 0:128], kpos)
        vw_scr[...] = win_ref[:, 64:128].astype(BF16)

    qpos = qi * tq + lax.broadcasted_iota(jnp.int32, (1, tq), 1)
    gates_t = jax.nn.sigmoid(gate_ref[:, 128:256]).T
    qst_scr[...] = _query_stack(q_ref[...], NSA_SLOPES)

    n_io = lax.broadcasted_iota(jnp.int32, (ncp, tq), 0)
    cmask = _tile4(((n_io * CMP_STRIDE + (CMP_LEN - 1)) <= qpos) & (n_io < n_cmp))
    s = jnp.where(cmask, _nt(kc_scr[...], qst_scr[...]), NEG)
    e = jnp.where(cmask, jnp.exp(s - jnp.max(s, axis=0, keepdims=True)), 0.0)
    pc = e / jnp.maximum(jnp.sum(e, axis=0, keepdims=True), TINY)
    ocmp_scr[...] = _tn(vc_scr[...], pc.astype(BF16))
    pc_sum = pc[:, 0:tq]
    for h in range(1, N_HEADS):
        pc_sum = pc_sum + pc[:, h * tq:(h + 1) * tq]

    imp = _xdot_left(imp_ref[...], pc_sum)
    blk = lax.broadcasted_iota(jnp.int32, (n_sel, tq), 0)
    cur = qpos // SEL_BLOCK
    forced = (blk == 0) | (blk == cur) | (blk == cur - 1)
    vals = jnp.where(blk <= cur, jnp.where(forced, BIG, imp), NEG)
    chosen = _rank_select(vals, blk, n_sel, min(N_SEL, n_sel), 0)
    selb_scr[...] = jnp.where(chosen, 0.0, NEG)

    def sel_bias(kt):
        rows = [jnp.broadcast_to(selb_scr[pl.ds(kt * blocks_per_tile + i, 1), :], (SEL_BLOCK, tq))
                for i in range(blocks_per_tile)]
        return _tile4(jnp.concatenate(rows, axis=0))

    def tile_rows(kt):
        return pl.ds(pl.multiple_of(kt * TK, TK), TK)

    def step(k_scr, v_scr, rows, bias, first):
        _tile_step(lambda h: k_scr[rows, :], lambda h: v_scr[rows, :], qst_scr, bias,
                   m_scr, l_scr, acc_scr, first, tq, 2 * tq)

    diag = tile_rows(qi)
    step(ks_scr, vs_scr, diag, sel_bias(qi) + _tri_bias(tq, True), True)

    def sel_body(kt, carry):
        step(ks_scr, vs_scr, tile_rows(kt), sel_bias(kt), False)
        return carry

    lax.fori_loop(0, qi, sel_body, 0)
    gate_row = lambda j: jnp.concatenate([gates_t[3 * h + j:3 * h + j + 1, :] for h in range(N_HEADS)], axis=1)
    out = gate_row(0) * ocmp_scr[...] + gate_row(1) * (acc_scr[...] / l_scr[...])

    step(kw_scr, vw_scr, diag, _tri_bias(tq, True), True)

    @pl.when(qi >= 1)
    def _():
        step(kw_scr, vw_scr, tile_rows(qi - 1), None, False)

    @pl.when(qi >= WINDOW // TK)
    def _():
        step(kw_scr, vw_scr, tile_rows(qi - WINDOW // TK), _tri_bias(tq, False), False)

    out = out + gate_row(2) * (acc_scr[...] / l_scr[...])
    o_ref[...] = jnp.concatenate([out[:, h * tq:(h + 1) * tq] for h in range(N_HEADS)], axis=0).T


def _cmp_to_sel(n_cmp, n_sel):
    ratio, span = SEL_BLOCK // CMP_STRIDE, CMP_LEN // CMP_STRIDE
    j, m, n = np.meshgrid(np.arange(n_sel), np.arange(ratio), np.arange(span), indexing="ij")
    i = ratio * j + m - n
    ok = (i >= 0) & (i < n_cmp)
    mat = np.zeros((n_cmp, n_sel), np.float32)
    np.add.at(mat, (i[ok], j[ok]), 1.0)
    return mat


def nsa_prompt(xq, xkv, xwg, batch, seq, w1, pos, w2):
    nq = seq // TQ
    ncp = seq // CMP_STRIDE
    n_sel = seq // SEL_BLOCK
    imp_t = np.zeros((n_sel, ncp), np.float32)
    imp_t[:, :ncp - 1] = _cmp_to_sel(ncp - 1, n_sel).T
    full = lambda *shape: pl.BlockSpec(shape, lambda b, i: (0,) * len(shape))
    return pl.pallas_call(
        functools.partial(_nsa_prompt_body, seq=seq),
        grid=(batch, nq),
        in_specs=[pl.BlockSpec((TQ, GROUP_WIDTH), lambda b, i: (b * nq + i, 0)),
                  pl.BlockSpec((TQ, GROUP_WIDTH), lambda b, i: (b * nq + i, 0)),
                  pl.BlockSpec((seq, GROUP_WIDTH), lambda b, i: (b, 0)),
                  pl.BlockSpec((seq, GROUP_WIDTH), lambda b, i: (b, 0)),
                  full(CMP_LEN, LANES, 2 * CMP_HIDDEN), full(CMP_LEN, LANES),
                  full(2 * CMP_HIDDEN, LANES), full(n_sel, ncp)],
        out_specs=pl.BlockSpec((TQ, GROUP_WIDTH), lambda b, i: (b * nq + i, 0)),
        out_shape=jax.ShapeDtypeStruct((batch * seq, GROUP_WIDTH), F32),
        scratch_shapes=[pltpu.VMEM((seq + CMP_STRIDE, LANES), F32),
                        pltpu.VMEM((ncp, LANES), BF16), pltpu.VMEM((ncp, HEAD_DIM), BF16),
                        pltpu.VMEM((seq, LANES), BF16), pltpu.VMEM((seq, HEAD_DIM), BF16),
                        pltpu.VMEM((seq, LANES), BF16), pltpu.VMEM((seq, HEAD_DIM), BF16),
                        pltpu.VMEM((n_sel, TQ), F32),
                        pltpu.VMEM((N_HEADS * TQ, LANES), BF16), pltpu.VMEM((HEAD_DIM, N_HEADS * TQ), F32),
                        pltpu.VMEM((1, N_HEADS * TQ), F32), pltpu.VMEM((1, N_HEADS * TQ), F32),
                        pltpu.VMEM((HEAD_DIM, N_HEADS * TQ), F32)],
        compiler_params=_cparams(("parallel", "arbitrary")),
        name="nsa_prompt",
    )(xq, xwg, xkv, xwg, w1, pos, w2, jnp.asarray(imp_t))


def _moba_prompt_body(q_ref, kv_ref, o_ref, km_scr, ka_scr, va_scr, selb_scr, qst_scr,
                      m_scr, l_scr, acc_scr, *, seq):
    qi = pl.program_id(1)
    tq = q_ref.shape[0]
    nb = seq // MOBA_BLOCK

    @pl.when(qi == 0)
    def _():
        for n in range(nb):
            km_scr[n:n + 1, :] = jnp.mean(kv_ref[n * MOBA_BLOCK:(n + 1) * MOBA_BLOCK, 0:GROUP_WIDTH],
                                          axis=0, keepdims=True)

        kpos = lax.broadcasted_iota(jnp.int32, (seq, LANES), 0)
        for h in range(N_HEADS):
            pair = kv_ref[:, (h // 2) * LANES:(h // 2 + 1) * LANES]
            if h % 2:
                pair = pltpu.roll(pair, HEAD_DIM, 1)
            ka_scr[h] = _key_tile(pair, kpos)
            va_scr[h] = kv_ref[:, GROUP_WIDTH + h * HEAD_DIM:GROUP_WIDTH + (h + 1) * HEAD_DIM].astype(BF16)

    qpos = qi * tq + lax.broadcasted_iota(jnp.int32, (1, tq), 1)
    own = qpos // MOBA_BLOCK
    blk = lax.broadcasted_iota(jnp.int32, (nb, tq), 0)
    past = blk < own
    for h in range(N_HEADS):
        hs = slice(h * HEAD_DIM, (h + 1) * HEAD_DIM)
        gate = _hdot_nt(km_scr[:, hs], q_ref[:, hs])
        chosen = _rank_select(jnp.where(past, gate, NEG), blk, nb, min(MOBA_TOPK, nb), 0) & past
        selb_scr[h] = jnp.where(chosen, 0.0, NEG)

    qst_scr[...] = _query_stack(q_ref[...], MOBA_SLOPES)

    def step(rows, bias, first):
        _tile_step(lambda h: ka_scr[h, rows, :], lambda h: va_scr[h, rows, :], qst_scr, bias,
                   m_scr, l_scr, acc_scr, first, tq, tq)

    step(pl.ds(pl.multiple_of(qi * TK, TK), TK), _tri_bias(tq, True), True)

    def body(kt, carry):
        bias = jnp.concatenate([selb_scr[h, pl.ds(kt, 1), :] for h in range(N_HEADS)], axis=1)
        step(pl.ds(pl.multiple_of(kt * TK, TK), TK), bias, False)
        return carry

    lax.fori_loop(0, qi, body, 0)
    out = acc_scr[...] / l_scr[...]
    o_ref[...] = jnp.concatenate([out[:, h * tq:(h + 1) * tq] for h in range(N_HEADS)], axis=0).T


def moba_prompt(xq, xkv, batch, seq):
    nq = seq // TQ
    nb = seq // MOBA_BLOCK
    return pl.pallas_call(
        functools.partial(_moba_prompt_body, seq=seq),
        grid=(batch, nq),
        in_specs=[pl.BlockSpec((TQ, GROUP_WIDTH), lambda b, i: (b * nq + i, 0)),
                  pl.BlockSpec((seq, 2 * GROUP_WIDTH), lambda b, i: (b, 0))],
        out_specs=pl.BlockSpec((TQ, GROUP_WIDTH), lambda b, i: (b * nq + i, 0)),
        out_shape=jax.ShapeDtypeStruct((batch * seq, GROUP_WIDTH), F32),
        scratch_shapes=[pltpu.VMEM((nb, GROUP_WIDTH), F32),
                        pltpu.VMEM((N_HEADS, seq, LANES), BF16), pltpu.VMEM((N_HEADS, seq, HEAD_DIM), BF16),
                        pltpu.VMEM((N_HEADS, nb, TQ), F32),
                        pltpu.VMEM((N_HEADS * TQ, LANES), BF16),
                        pltpu.VMEM((1, N_HEADS * TQ), F32), pltpu.VMEM((1, N_HEADS * TQ), F32),
                        pltpu.VMEM((HEAD_DIM, N_HEADS * TQ), F32)],
        compiler_params=_cparams(("parallel", "arbitrary")),
        name="moba_prompt",
    )(xq, xkv)


def _softmax_rows(parts, masks):
    masked = [jnp.where(mk, s, NEG) for s, mk in zip(parts, masks)]
    m = masked[0].max(axis=1, keepdims=True)
    for s in masked[1:]:
        m = jnp.maximum(m, s.max(axis=1, keepdims=True))
    es = [jnp.where(mk, jnp.exp(s - m), 0.0) for s, mk in zip(masked, masks)]
    den = es[0].sum(axis=1, keepdims=True)
    for e in es[1:]:
        den = den + e.sum(axis=1, keepdims=True)
    inv = 1.0 / jnp.maximum(den, TINY)
    return [e * inv for e in es]


def _stack_heads(x, off):
    return jnp.concatenate([x[:, off + h * HEAD_DIM:off + (h + 1) * HEAD_DIM] for h in range(N_HEADS)], axis=0)


def _round_robin(gens):
    outs = [None] * len(gens)
    live = list(range(len(gens)))
    while live:
        for i in list(live):
            try:
                next(gens[i])
            except StopIteration as stop:
                outs[i] = stop.value
                live.remove(i)
    return outs


def _pad_rows(x, n):
    return jnp.concatenate([x, jnp.zeros((n - x.shape[0], x.shape[1]), x.dtype)], axis=0)


def _nsa_decode_body(*refs, n_pages, past_len, ns):
    x_refs = refs[1:4]
    all_pages = refs[4:4 + ns * n_pages]
    (win_ref, w1_ref, pos_ref, w2_ref, imp_ref, esel_ref, o_ref, xs_scr) = refs[4 + ns * n_pages:]
    steps = x_refs[0].shape[0] // ns
    x_rows = lambda i: jnp.concatenate([r[i * steps:(i + 1) * steps, :] for r in x_refs], axis=1)
    ncp = past_len // CMP_STRIDE
    for i in range(ns):
        for p in range(n_pages):
            xs_scr[i, p * PAGE_SIZE:(p + 1) * PAGE_SIZE, :] = (
                all_pages[i * n_pages + p][0, 0, 0:2].reshape(LANES, PAGE_SIZE).T)
        xs_scr[i, past_len:past_len + CMP_STRIDE, :] = jnp.zeros((CMP_STRIDE, LANES), F32)
    kvc_all = _cmp_mlp([xs_scr.at[i] for i in range(ns)], pos_ref, w1_ref, w2_ref, ncp)
    outs = _round_robin([_nsa_decode_seq(
        x_rows(i), all_pages[i * n_pages:(i + 1) * n_pages], win_ref, i,
        kvc_all[i * ncp:(i + 1) * ncp], imp_ref, esel_ref, n_pages, past_len) for i in range(ns)])
    for i in range(ns):
        o_ref[i * steps:(i + 1) * steps, :] = outs[i]


def _nsa_decode_seq(x, page_refs, win_ref, wi, kvc, imp_ref, esel_ref, n_pages, past_len):
    steps = x.shape[0]
    rows = N_HEADS * steps
    ncp = past_len // CMP_STRIDE
    n_cmp = ncp - 1
    tpos = past_len + lax.broadcasted_iota(jnp.int32, (steps, 1), 0)
    qpos = jnp.concatenate([tpos] * N_HEADS, axis=0)
    slope = jnp.concatenate([jnp.full((steps, 1), s, F32) for s in NSA_SLOPES], axis=0)
    qs = _stack_heads(x, 0) * (HEAD_DIM ** -0.5)
    gates = jax.nn.sigmoid(x[:, 640:768])
    gcol = [jnp.concatenate([gates[:, 3 * h + j:3 * h + j + 1] for h in range(N_HEADS)], axis=0)
            for j in range(3)]
    new_io = lax.broadcasted_iota(jnp.int32, (rows, LANES), 1)
    new_pos = past_len + new_io
    new_ok = (new_io < steps) & (new_pos <= qpos)

    k_cmp, v_cmp = kvc[:, 0:HEAD_DIM], kvc[:, HEAD_DIM:2 * HEAD_DIM]
    n_io = lax.broadcasted_iota(jnp.int32, (rows, ncp), 1)
    cend = n_io * CMP_STRIDE + (CMP_LEN - 1)
    (pc,) = _softmax_rows([_bdot_nt(qs, k_cmp) + slope * cend.astype(F32)],
                          [(cend <= qpos) & (n_io < n_cmp)])
    o_cmp = _bdot(pc, v_cmp)
    yield
    pc_sum = pc[0:steps]
    for h in range(1, N_HEADS):
        pc_sum = pc_sum + pc[h * steps:(h + 1) * steps]

    n_sel = -(-(past_len + steps) // SEL_BLOCK)
    imp = _hdot(pc_sum, imp_ref[...])
    yield
    blk = lax.broadcasted_iota(jnp.int32, (steps, LANES), 1)
    cur = tpos // SEL_BLOCK
    forced = (blk == 0) | (blk == cur) | (blk == cur - 1)
    vals = jnp.where((blk <= cur) & (blk < n_sel), jnp.where(forced, BIG, imp), NEG)
    chosen = _rank_select(vals, blk, n_sel, min(N_SEL, n_sel), 1).astype(F32)
    key_sel = _bdot(chosen, esel_ref[...])
    key_sel = jnp.concatenate([key_sel] * N_HEADS, axis=0) > 0.5
    yield

    cat_pages = lambda c: jnp.concatenate([page_refs[p][0, 0, c].astype(BF16) for p in range(n_pages)], axis=1)
    s_past = _bdot(qs, cat_pages(2))
    kpos = lax.broadcasted_iota(jnp.int32, (rows, past_len), 1)
    k_new = _pad_rows(x[:, 384:448], LANES)
    v_new = _pad_rows(x[:, 448:512], LANES)
    s_new = _bdot_nt(qs, k_new) + slope * new_pos.astype(F32)
    yield
    p_past, p_new = _softmax_rows([s_past + slope * kpos.astype(F32), s_new],
                                  [key_sel[:, 0:past_len] & (kpos <= qpos),
                                   key_sel[:, past_len:past_len + LANES] & new_ok])
    o_sel = _bdot(p_new, v_new) + _bdot_nt(p_past, cat_pages(3))
    yield

    n_win = win_ref.shape[-1]
    wpos = (past_len - n_win) + lax.broadcasted_iota(jnp.int32, (rows, n_win), 1)
    kw_new = _pad_rows(x[:, 512:576], LANES)
    vw_new = _pad_rows(x[:, 576:640], LANES)
    s_w = _bdot(qs, win_ref[0, wi, 0]) + slope * wpos.astype(F32)
    s_wn = _bdot_nt(qs, kw_new) + slope * new_pos.astype(F32)
    yield
    dist = qpos - wpos
    p_w, p_wn = _softmax_rows([s_w, s_wn],
                              [(dist >= 0) & (dist < WINDOW), new_ok & (qpos - new_pos < WINDOW)])
    o_win = _bdot_nt(p_w, win_ref[0, wi, 1]) + _bdot(p_wn, vw_new)

    out = gcol[0] * o_cmp + gcol[1] * o_sel + gcol[2] * o_win
    return jnp.concatenate([out[h * steps:(h + 1) * steps] for h in range(N_HEADS)], axis=1)


def nsa_decode(xs, page_ids, cache_t, layer, win_t, w1, pos, w2, *, steps, past_len):
    nseq = xs[0].shape[0] // steps
    n_pages = past_len // PAGE_SIZE
    ncp = past_len // CMP_STRIDE
    n_sel = -(-(past_len + steps) // SEL_BLOCK)
    imp = np.zeros((ncp, LANES), np.float32)
    imp[:ncp - 1, :n_sel] = _cmp_to_sel(ncp - 1, n_sel)
    kblk = np.concatenate([np.arange(past_len) // SEL_BLOCK,
                           (past_len + np.arange(LANES)) // SEL_BLOCK])
    esel = (np.arange(LANES)[:, None] == kblk[None, :]).astype(np.float32)
    n_win = win_t.shape[-1]

    ns = math.gcd(nseq, DECODE_SEQS)

    def page_spec(j):
        return pl.BlockSpec((1, 1, 4, HEAD_DIM, PAGE_SIZE),
                            lambda b, pt, j=j: (pt[b * (ns * n_pages) + j], layer, 0, 0, 0))

    full = lambda *shape: pl.BlockSpec(shape, lambda b, pt: (0,) * len(shape))
    return pl.pallas_call(
        functools.partial(_nsa_decode_body, n_pages=n_pages, past_len=past_len, ns=ns),
        grid_spec=pltpu.PrefetchScalarGridSpec(
            num_scalar_prefetch=1,
            grid=(nseq // ns,),
            in_specs=[pl.BlockSpec((ns * steps, GROUP_WIDTH), lambda b, pt: (b, 0)) for _ in xs]
            + [page_spec(j) for j in range(ns * n_pages)]
            + [pl.BlockSpec((1, ns, 2, HEAD_DIM, n_win), lambda b, pt: (layer, b, 0, 0, 0)),
               full(CMP_LEN, LANES, 2 * CMP_HIDDEN), full(CMP_LEN, LANES),
               full(2 * CMP_HIDDEN, LANES), full(ncp, LANES), full(LANES, past_len + LANES)],
            out_specs=pl.BlockSpec((ns * steps, GROUP_WIDTH), lambda b, pt: (b, 0)),
            scratch_shapes=[pltpu.VMEM((ns, past_len + CMP_STRIDE, LANES), F32)],
        ),
        out_shape=jax.ShapeDtypeStruct((nseq * steps, GROUP_WIDTH), F32),
        compiler_params=_cparams(("parallel",)),
        name="nsa_decode",
    )(page_ids, *xs, *([cache_t] * (ns * n_pages)), win_t, w1, pos, w2, jnp.asarray(imp), jnp.asarray(esel))


def _moba_decode_body(*refs, n_pages, past_len, ns):
    x_refs = refs[1:3]
    all_pages = refs[3:3 + ns * n_pages]
    emean_ref, eblk_ref, o_ref = refs[3 + ns * n_pages:]
    steps = x_refs[0].shape[0] // ns
    x_rows = lambda i: jnp.concatenate([r[i * steps:(i + 1) * steps, :] for r in x_refs], axis=1)
    outs = _round_robin([_moba_decode_seq(
        x_rows(i), all_pages[i * n_pages:(i + 1) * n_pages],
        emean_ref, eblk_ref, n_pages, past_len) for i in range(ns)])
    for i in range(ns):
        o_ref[i * steps:(i + 1) * steps, :] = outs[i]


def _moba_decode_seq(x, page_refs, emean_ref, eblk_ref, n_pages, past_len):
    steps = x.shape[0]
    rows = N_HEADS * steps
    tpos = past_len + lax.broadcasted_iota(jnp.int32, (steps, 1), 0)
    qpos = jnp.concatenate([tpos] * N_HEADS, axis=0)
    own = qpos // MOBA_BLOCK
    slope = jnp.concatenate([jnp.full((steps, 1), s, F32) for s in MOBA_SLOPES], axis=0)
    head_of_row = lax.broadcasted_iota(jnp.int32, (rows, GROUP_WIDTH), 0) // steps
    head_of_col = lax.broadcasted_iota(jnp.int32, (rows, GROUP_WIDTH), 1) // HEAD_DIM
    diag = head_of_row == head_of_col
    q_bd = jnp.where(diag, jnp.concatenate([x[:, 0:GROUP_WIDTH]] * N_HEADS, axis=0), 0.0)

    cat_pages = lambda c: jnp.concatenate(
        [page_refs[p][0, 0, c].reshape(GROUP_WIDTH, PAGE_SIZE).astype(BF16) for p in range(n_pages)], axis=1)
    kt_all = cat_pages(0)
    kmean_t = jnp.dot(kt_all, emean_ref[...], preferred_element_type=F32)
    yield
    gate = _hdot(q_bd, kmean_t)
    yield
    nb = -(-(past_len + steps) // MOBA_BLOCK)
    blk = lax.broadcasted_iota(jnp.int32, (rows, LANES), 1)
    past = (blk < own) & (blk < nb)
    vals = jnp.where(past, gate, NEG)
    chosen = (_rank_select(vals, blk, nb, min(MOBA_TOPK, nb), 1) & past).astype(F32)
    key_sel = _bdot(chosen, eblk_ref[...]) > 0.5
    yield

    q_sc = q_bd * (HEAD_DIM ** -0.5)
    s_past = _bdot(q_sc, kt_all)
    kpos = lax.broadcasted_iota(jnp.int32, (rows, past_len), 1)
    new_io = lax.broadcasted_iota(jnp.int32, (rows, LANES), 1)
    new_pos = past_len + new_io
    k_new = _pad_rows(x[:, 256:512], LANES)
    v_new = _pad_rows(x[:, 512:768], LANES)
    s_new = _bdot_nt(q_sc, k_new) + slope * new_pos.astype(F32)
    yield
    in_own_past = (kpos // MOBA_BLOCK) == own
    in_own_new = (new_pos // MOBA_BLOCK) == own
    p_past, p_new = _softmax_rows(
        [s_past + slope * kpos.astype(F32), s_new],
        [key_sel | in_own_past, (new_io < steps) & (new_pos <= qpos) & in_own_new])
    o_all = _bdot(p_new, v_new) + _bdot_nt(p_past, cat_pages(1))
    o_all = jnp.where(diag, o_all, 0.0)
    out = o_all[0:steps]
    for h in range(1, N_HEADS):
        out = out + o_all[h * steps:(h + 1) * steps]
    return out


def moba_decode(xs, page_ids, cache_t, layer, *, steps, past_len):
    nseq = xs[0].shape[0] // steps
    n_pages = past_len // PAGE_SIZE
    pages_per_blk = MOBA_BLOCK // PAGE_SIZE
    emean = np.zeros((n_pages, PAGE_SIZE, LANES), np.float32)
    for p in range(n_pages):
        emean[p, :, p // pages_per_blk] = 1.0 / MOBA_BLOCK
    eblk = (np.arange(LANES)[:, None] == (np.arange(past_len) // MOBA_BLOCK)[None, :]).astype(np.float32)

    ns = math.gcd(nseq, DECODE_SEQS)

    def page_spec(j):
        return pl.BlockSpec((1, 1, 2, N_HEADS, HEAD_DIM, PAGE_SIZE),
                            lambda b, pt, j=j: (pt[b * (ns * n_pages) + j], layer, 0, 0, 0, 0))

    full = lambda *shape: pl.BlockSpec(shape, lambda b, pt: (0,) * len(shape))
    return pl.pallas_call(
        functools.partial(_moba_decode_body, n_pages=n_pages, past_len=past_len, ns=ns),
        grid_spec=pltpu.PrefetchScalarGridSpec(
            num_scalar_prefetch=1,
            grid=(nseq // ns,),
            in_specs=[pl.BlockSpec((ns * steps, a.shape[1]), lambda b, pt: (b, 0)) for a in xs]
            + [page_spec(j) for j in range(ns * n_pages)]
            + [full(past_len, LANES), full(LANES, past_len)],
            out_specs=pl.BlockSpec((ns * steps, GROUP_WIDTH), lambda b, pt: (b, 0)),
        ),
        out_shape=jax.ShapeDtypeStruct((nseq * steps, GROUP_WIDTH), F32),
        compiler_params=_cparams(("parallel",)),
        name="moba_decode",
    )(page_ids, *xs, *([cache_t] * (ns * n_pages)), jnp.asarray(emean.reshape(past_len, LANES), BF16),
      jnp.asarray(eblk))


def _regroup_w_in(w):
    z = lambda n: jnp.zeros(w.shape[:2] + (n,), w.dtype)
    col = lambda a, b: w[:, :, a:b]
    gq, gk, gv, ga, gr = col(0, 128), col(128, 256), col(256, 512), col(512, 528), col(528, 784)
    ret = col(784, 1808)
    nq, nkv, ng = col(1808, 2064), col(2064, 2448), col(2448, 2460)
    moba = col(2460, 3228)
    return jnp.concatenate([gq, gk, gv, gr, ga, z(112), ret, nq, nkv, ng, z(116), moba], axis=2).astype(BF16)


def _cmp_weights(w1, pos, w2):
    w1r = w1.reshape(2, CMP_LEN, HEAD_DIM, CMP_HIDDEN)
    zero = jnp.zeros((CMP_LEN, HEAD_DIM, CMP_HIDDEN), w1.dtype)
    w1bd = jnp.concatenate([jnp.concatenate([w1r[0], zero], axis=2),
                            jnp.concatenate([zero, w1r[1]], axis=2)], axis=1)
    zero2 = jnp.zeros((CMP_HIDDEN, HEAD_DIM), w2.dtype)
    w2bd = jnp.concatenate([jnp.concatenate([w2[0], zero2], axis=1),
                            jnp.concatenate([zero2, w2[1]], axis=1)], axis=0)
    posf = jnp.concatenate([pos[0], pos[1]], axis=1)
    return w1bd.astype(BF16), posf, w2bd.astype(BF16)


def kernel(x_prompt, x_sample, cache_nsa, cache_moba, cache_nsa_win, state_gla, state_ret, page_table, w_in, gla_w_a2, gla_b_a, gla_norm, ret_norm, nsa_cmp_w1, nsa_cmp_w2, nsa_cmp_pos, w_out, ffn_w_up, ffn_w_down, ln_g, ln_b):
    bp, sp, d = x_prompt.shape
    bs, ss, _ = x_sample.shape
    past_len = page_table.shape[1] * PAGE_SIZE
    n_win = cache_nsa_win.shape[2]

    nsa_t = jnp.transpose(cache_nsa, (0, 1, 3, 4, 2))
    moba_t = jnp.transpose(cache_moba, (0, 1, 3, 4, 5, 2))
    win_t = jnp.transpose(cache_nsa_win, (0, 1, 3, 4, 2))
    gla_t = jnp.transpose(state_gla, (0, 2, 3, 4, 1)).reshape(DEPTH, N_HEADS * GLA_DK, HEAD_DIM, bs)
    ret_t = jnp.transpose(state_ret, (0, 2, 3, 4, 1)).reshape(DEPTH, N_HEADS * HEAD_DIM, HEAD_DIM, bs)
    page_ids = page_table.reshape(-1)

    log_gamma = np.log1p(-np.power(2.0, -5.0 - np.arange(N_HEADS, dtype=np.float64))).astype(np.float32)
    la_ret = jnp.asarray(np.repeat(log_gamma, HEAD_DIM))
    zeros128 = jnp.zeros((1, 128), F32)

    xp = x_prompt.reshape(bp * sp, d)
    xs = x_sample.reshape(bs * ss, d)
    outs = {k: [] for k in ("nsa_p", "moba_p", "win_p", "gla_p", "ret_p",
                            "nsa_s", "moba_s", "win_s", "gla_s", "ret_s")}
    wu = ffn_w_up.astype(BF16)
    wd = ffn_w_down.astype(BF16)
    wi = _regroup_w_in(w_in)
    wo = w_out.astype(BF16)
    for l in range(DEPTH):
        g = ln_g[l].reshape(3, 1, d)
        b = ln_b[l].reshape(3, 1, d)
        wa = jnp.zeros((128, 128), F32).at[0:GLA_RANK, :].set(gla_w_a2[l])
        ba = gla_b_a[l].reshape(1, 128)
        gn = gla_norm[l].reshape(1, GROUP_WIDTH)
        rn = ret_norm[l].reshape(1, GROUP_WIDTH)
        w1bd, posf, w2bd = _cmp_weights(nsa_cmp_w1[l], nsa_cmp_pos[l], nsa_cmp_w2[l])

        xp = ffn_ln(xp, wu, wd, l, 0, g[0], b[0])
        pg, pr, nq, nkv, nwg, mq, mkv = proj(xp, wi, l)
        o_gla, st_gla = gla_prompt(pg, bp, sp, wa, ba, gn)
        o_ret, st_ret = ret_prompt(pr, bp, sp, rn)
        o_nsa = nsa_prompt(nq, nkv, nwg, bp, sp, w1bd, posf, w2bd)
        o_moba = moba_prompt(mq, mkv, bp, sp)
        xp = out_ln(xp, (o_gla, o_ret, o_nsa, o_moba), wo, l, g[1], b[1])
        xp = ffn_ln(xp, wu, wd, l, 1, g[2], b[2])
        outs["nsa_p"].append(nkv.reshape(bp, sp, 4, HEAD_DIM))
        outs["moba_p"].append(mkv.reshape(bp, sp, 2, N_HEADS, HEAD_DIM))
        keep = min(WINDOW, sp)
        outs["win_p"].append(nwg.reshape(bp, sp, GROUP_WIDTH)[:, sp - keep:, 0:128].reshape(bp, keep, 2, HEAD_DIM))
        outs["gla_p"].append(st_gla)
        outs["ret_p"].append(st_ret)

        xs = ffn_ln(xs, wu, wd, l, 0, g[0], b[0])
        pg, pr, nq, nkv, nwg, mq, mkv = proj(xs, wi, l)
        to_lanes = lambda a: jnp.transpose(a.reshape(bs, ss, a.shape[1]), (1, 2, 0))
        og_t, sg_t = lin_decode(to_lanes(pg), gla_t[l], "gla", wa.T,
                                ba.reshape(128, 1), jnp.zeros((128, 1), F32), gn.reshape(GROUP_WIDTH, 1))
        or_t, sr_t = lin_decode(to_lanes(pr), ret_t[l], "ret", jnp.zeros((128, 128), F32),
                                jnp.zeros((128, 1), F32), la_ret.reshape(256, 1), rn.reshape(GROUP_WIDTH, 1))
        from_lanes = lambda a: jnp.transpose(a, (2, 0, 1)).reshape(bs * ss, GROUP_WIDTH)
        o_nsa = nsa_decode((nq, nkv, nwg), page_ids, nsa_t, l, win_t, w1bd, posf, w2bd, steps=ss, past_len=past_len)
        o_moba = moba_decode((mq, mkv), page_ids, moba_t, l, steps=ss, past_len=past_len)
        xs = out_ln(xs, (from_lanes(og_t), from_lanes(or_t), o_nsa, o_moba), wo, l, g[1], b[1])
        xs = ffn_ln(xs, wu, wd, l, 1, g[2], b[2])
        outs["nsa_s"].append(nkv.reshape(bs, ss, 4, HEAD_DIM))
        outs["moba_s"].append(mkv.reshape(bs, ss, 2, N_HEADS, HEAD_DIM))
        outs["win_s"].append(nwg[:, 0:128].reshape(bs, ss, 2, HEAD_DIM))
        outs["gla_s"].append(jnp.transpose(sg_t.reshape(N_HEADS, GLA_DK, HEAD_DIM, bs), (3, 0, 1, 2)))
        outs["ret_s"].append(jnp.transpose(sr_t.reshape(N_HEADS, HEAD_DIM, HEAD_DIM, bs), (3, 0, 1, 2)))

    new_win = jnp.stack(outs["win_s"], axis=0)
    if n_win > 0:
        win_s = jnp.concatenate([cache_nsa_win[:, :, ss:], new_win], axis=2)
    else:
        win_s = new_win[:, :, ss - min(WINDOW, ss):]
    return (xp.reshape(bp, sp, d), xs.reshape(bs, ss, d),
            jnp.stack(outs["nsa_p"], axis=1), jnp.stack(outs["moba_p"], axis=1),
            jnp.stack(outs["win_p"], axis=0), jnp.stack(outs["gla_p"], axis=0), jnp.stack(outs["ret_p"], axis=0),
            jnp.stack(outs["nsa_s"], axis=1), jnp.stack(outs["moba_s"], axis=1),
            win_s, jnp.stack(outs["gla_s"], axis=0), jnp.stack(outs["ret_s"], axis=0))
```

```python
import functools
import math

import numpy as np
import jax
import jax.numpy as jnp
from jax import lax
from jax.experimental import pallas as pl
from jax.experimental.pallas import tpu as pltpu

F32 = jnp.float32
BF16 = jnp.bfloat16
HI = lax.Precision.HIGHEST

D_MODEL = 1024
DEPTH = 4
PAGE_SIZE = 128
HEAD_DIM = 64
N_HEADS = 4
GROUP_WIDTH = 256
GLA_DK = 32
GLA_RANK = 16
GLA_TAU = 16.0
CMP_LEN = 32
CMP_STRIDE = 16
CMP_HIDDEN = 128
SEL_BLOCK = 64
N_SEL = 8
WINDOW = 512
MOBA_BLOCK = 256
MOBA_TOPK = 3
D_FF = 2816
ALPHA = (2 * DEPTH) ** 0.25
LN_EPS = 1e-5
NEG = -1e30
BIG = 1e30
TINY = 1e-30

LANES = 128
VMEM_LIMIT = 56 * 1024 * 1024

GLA_W = 896
RET_W = 1024
NSA_W = 768
MOBA_W = 768
PROJ_SPLITS = (GLA_W, RET_W, 256, 256, 256, 256, 512)


def _slopes():
    n = 2 * N_HEADS
    s = [2.0 ** (-8.0 * i / n) for i in range(1, n + 1)]
    return s[0::2], s[1::2]


NSA_SLOPES, MOBA_SLOPES = _slopes()


def _bdot(a, b):
    return jnp.dot(a.astype(BF16), b.astype(BF16), preferred_element_type=F32)


def _bdot_nt(a, b):
    return lax.dot_general(a.astype(BF16), b.astype(BF16), (((1,), (1,)), ((), ())),
                           preferred_element_type=F32)


def _bdot_tn(a, b):
    return lax.dot_general(a.astype(BF16), b.astype(BF16), (((0,), (0,)), ((), ())),
                           preferred_element_type=F32)


def _hdot(a, b):
    return jnp.dot(a, b, precision=HI, preferred_element_type=F32)


def _hdot_nt(a, b):
    return lax.dot_general(a, b, (((1,), (1,)), ((), ())), precision=HI, preferred_element_type=F32)


def _hdot_tn(a, b):
    return lax.dot_general(a, b, (((0,), (0,)), ((), ())), precision=HI, preferred_element_type=F32)


def _split3(a):
    hi = a.astype(BF16)
    r1 = a - hi.astype(F32)
    mid = r1.astype(BF16)
    lo = (r1 - mid.astype(F32)).astype(BF16)
    return hi, mid, lo


def _xdot(a, c, dims=(((1,), (0,)), ((), ()))):
    cb = c.astype(BF16)
    return sum(lax.dot_general(part, cb, dims, preferred_element_type=F32) for part in _split3(a))


def _xdot_left(c, a):
    cb = c.astype(BF16)
    return sum(jnp.dot(cb, part, preferred_element_type=F32) for part in _split3(a))


def _layer_norm(y, g, b):
    mu = jnp.mean(y, axis=-1, keepdims=True)
    d = y - mu
    var = jnp.mean(d * d, axis=-1, keepdims=True)
    return d * lax.rsqrt(var + LN_EPS) * g + b


def _cparams(sem):
    return pltpu.CompilerParams(dimension_semantics=sem, vmem_limit_bytes=VMEM_LIMIT)


def _ffn_ln_body(x_ref, wu_ref, wg_ref, wd_ref, g_ref, b_ref, o_ref, acc_ref):
    j = pl.program_id(1)

    @pl.when(j == 0)
    def _():
        acc_ref[...] = jnp.zeros_like(acc_ref)

    xb = x_ref[...].astype(BF16)
    u = jnp.dot(xb, wu_ref[...], preferred_element_type=F32)
    gt = jnp.dot(xb, wg_ref[...], preferred_element_type=F32)
    a = (gt * jax.nn.sigmoid(gt) * u).astype(BF16)
    acc_ref[...] += jnp.dot(a, wd_ref[...], preferred_element_type=F32)

    @pl.when(j == pl.num_programs(1) - 1)
    def _():
        y = ALPHA * x_ref[...] + 0.5 * acc_ref[...]
        o_ref[...] = _layer_norm(y, g_ref[...], b_ref[...])


def ffn_ln(x, w_up, w_down, layer, g, b, *, tm=512, tf=1408):
    m, d = x.shape
    f = w_down.shape[2]
    tm = min(tm, m)
    nf = f // tf
    return pl.pallas_call(
        _ffn_ln_body,
        grid=(m // tm, nf),
        in_specs=[
            pl.BlockSpec((tm, d), lambda i, j: (i, 0)),
            pl.BlockSpec((None, None, d, tf), lambda i, j: (layer, 0, 0, j)),
            pl.BlockSpec((None, None, d, tf), lambda i, j: (layer, 0, 0, j + nf)),
            pl.BlockSpec((None, None, tf, d), lambda i, j: (layer, 0, j, 0)),
            pl.BlockSpec((1, d), lambda i, j: (0, 0)),
            pl.BlockSpec((1, d), lambda i, j: (0, 0)),
        ],
        out_specs=pl.BlockSpec((tm, d), lambda i, j: (i, 0)),
        out_shape=jax.ShapeDtypeStruct((m, d), F32),
        scratch_shapes=[pltpu.VMEM((tm, d), F32)],
        compiler_params=_cparams(("parallel", "arbitrary")),
        name="ffn_ln",
    )(x, w_up, w_up, w_down, g, b)


def _proj_body(x_ref, w_ref, *o_refs):
    xb = x_ref[...].astype(BF16)
    off = 0
    for o_ref in o_refs:
        wdt = o_ref.shape[1]
        o_ref[...] = jnp.dot(xb, w_ref[:, off:off + wdt], preferred_element_type=F32)
        off += wdt


def proj(x, w, layer, *, tm=512):
    m, d = x.shape
    n = w.shape[2]
    tm = min(tm, m)
    return pl.pallas_call(
        _proj_body,
        grid=(m // tm,),
        in_specs=[pl.BlockSpec((tm, d), lambda i: (i, 0)),
                  pl.BlockSpec((None, d, n), lambda i: (layer, 0, 0))],
        out_specs=[pl.BlockSpec((tm, wdt), lambda i: (i, 0)) for wdt in PROJ_SPLITS],
        out_shape=[jax.ShapeDtypeStruct((m, wdt), F32) for wdt in PROJ_SPLITS],
        compiler_params=_cparams(("parallel",)),
        name="proj",
    )(x, w)


def _out_ffn_ln_body(x_ref, o0_ref, o1_ref, o2_ref, o3_ref, wo_ref, g1_ref, b1_ref,
                     wu_ref, wg_ref, wd_ref, g2_ref, b2_ref, y_ref, mid_scr, acc_ref):
    j = pl.program_id(1)

    @pl.when(j == 0)
    def _():
        mix = None
        for gi, o_ref in enumerate((o0_ref, o1_ref, o2_ref, o3_ref)):
            part = jnp.dot(o_ref[...].astype(BF16), wo_ref[gi * GROUP_WIDTH:(gi + 1) * GROUP_WIDTH, :],
                           preferred_element_type=F32)
            mix = part if mix is None else mix + part
        mid_scr[...] = _layer_norm(ALPHA * x_ref[...] + mix, g1_ref[...], b1_ref[...])
        acc_ref[...] = jnp.zeros_like(acc_ref)

    xb = mid_scr[...].astype(BF16)
    u = jnp.dot(xb, wu_ref[...], preferred_element_type=F32)
    gt = jnp.dot(xb, wg_ref[...], preferred_element_type=F32)
    a = (gt * jax.nn.sigmoid(gt) * u).astype(BF16)
    acc_ref[...] += jnp.dot(a, wd_ref[...], preferred_element_type=F32)

    @pl.when(j == pl.num_programs(1) - 1)
    def _():
        y = ALPHA * mid_scr[...] + 0.5 * acc_ref[...]
        y_ref[...] = _layer_norm(y, g2_ref[...], b2_ref[...])


def out_ffn_ln(x, outs, w_out, w_up, w_down, layer, g1, b1, g2, b2, *, tm=512, tf=1408):
    m, d = x.shape
    f = w_down.shape[2]
    tm = min(tm, m)
    nf = f // tf
    row = lambda i, j: (i, 0)
    vec = pl.BlockSpec((1, d), lambda i, j: (0, 0))
    return pl.pallas_call(
        _out_ffn_ln_body,
        grid=(m // tm, nf),
        in_specs=[pl.BlockSpec((tm, d), row)]
        + [pl.BlockSpec((tm, GROUP_WIDTH), row) for _ in range(4)]
        + [pl.BlockSpec((None, d, d), lambda i, j: (layer, 0, 0)), vec, vec,
           pl.BlockSpec((None, None, d, tf), lambda i, j: (layer, 1, 0, j)),
           pl.BlockSpec((None, None, d, tf), lambda i, j: (layer, 1, 0, j + nf)),
           pl.BlockSpec((None, None, tf, d), lambda i, j: (layer, 1, j, 0)), vec, vec],
        out_specs=pl.BlockSpec((tm, d), row),
        out_shape=jax.ShapeDtypeStruct((m, d), F32),
        scratch_shapes=[pltpu.VMEM((tm, d), F32), pltpu.VMEM((tm, d), F32)],
        compiler_params=_cparams(("parallel", "arbitrary")),
        name="out_ffn_ln",
    )(x, *outs, w_out, g1, b1, w_up, w_up, w_down, g2, b2)


def _head_group_norm(o, jn, gain):
    mu = _xdot(o, jn)
    d = o - mu
    var = _xdot(d * d, jn)
    return d * lax.rsqrt(var + LN_EPS) * gain


GLA_CHUNK = 16
GLA_ROWS = 128
RET_CHUNK = 128


def _gla_prompt_body(x_ref, wa_ref, ba_ref, gain_ref, tri_ref, tot_ref, emask_ref, jm_ref, jn_ref,
                     o_ref, st_ref, s_scr):
    c, r = GLA_CHUNK, GLA_ROWS
    g = r // c
    kd = 128
    seq = x_ref.shape[0]
    s_scr[...] = jnp.zeros_like(s_scr)
    jm = jm_ref[...]
    si = lax.broadcasted_iota(jnp.int32, (g, c, c, kd), 1)
    ti = lax.broadcasted_iota(jnp.int32, (g, c, c, kd), 2)
    causal = ti >= si

    def step(i, carry):
        rows = pl.ds(pl.multiple_of(i * r, r), r)
        q = x_ref[rows, 0:128] * (GLA_DK ** -0.5)
        k = x_ref[rows, 128:256]
        v = x_ref[rows, 256:512]
        gate = x_ref[rows, 512:768]
        la = jax.nn.log_sigmoid(_hdot(x_ref[rows, 768:896], wa_ref[...]) + ba_ref[...]) / GLA_TAU
        bt = _hdot(tri_ref[...], la)
        btot = _hdot(tot_ref[...], la)
        bt4 = bt.reshape(g, c, kd)
        q4 = q.reshape(g, c, kd)
        k4 = k.reshape(g, c, kd)
        dlt = jnp.minimum(bt4[:, None, :, :] - bt4[:, :, None, :], 0.0)
        w = jnp.where(causal, q4[:, None, :, :] * k4[:, :, None, :] * jnp.exp(dlt), 0.0)
        z = _bdot(w.reshape(g * c * c, kd), jm).reshape(g, c, c, GROUP_WIDTH)
        o = jnp.sum(z * v.reshape(g, c, GROUP_WIDTH)[:, :, None, :], axis=1).reshape(r, GROUP_WIDTH)
        qd = q * jnp.exp(bt)
        kdn = k * jnp.exp(btot - bt)
        edec = jnp.exp(_hdot_tn(la, emask_ref[...]))
        s_cur = s_scr[...]
        inter = []
        for j in range(g):
            rs = slice(j * c, (j + 1) * c)
            inter.append(_bdot(qd[rs], s_cur))
            upd = jnp.where(jm > 0, _bdot_tn(kdn[rs], v[rs]), 0.0)
            s_cur = jnp.broadcast_to(edec[:, j:j + 1], (kd, GROUP_WIDTH)) * s_cur + upd
        s_scr[...] = s_cur
        o = o + jnp.concatenate(inter, axis=0)
        o_ref[rows, :] = _head_group_norm(o, jn_ref[...], gain_ref[...]) * (gate * jax.nn.sigmoid(gate))
        return carry

    lax.fori_loop(0, seq // r, step, 0)
    s_fin = s_scr[...]
    for h in range(N_HEADS):
        st_ref[0, h] = s_fin[h * GLA_DK:(h + 1) * GLA_DK, h * HEAD_DIM:(h + 1) * HEAD_DIM]


def _ret_prompt_body(x_ref, gain_ref, dstack_ref, qdec_ref, kdec_ref, sdec_ref, hm_ref, jm_ref, jn_ref,
                     o_ref, st_ref, s_scr):
    c = RET_CHUNK
    seq = x_ref.shape[0]
    s_scr[...] = jnp.zeros_like(s_scr)

    def step(i, carry):
        rows = pl.ds(pl.multiple_of(i * c, c), c)
        q = x_ref[rows, 0:256]
        k = x_ref[rows, 256:512] * (HEAD_DIM ** -0.5)
        v = x_ref[rows, 512:768]
        gate = x_ref[rows, 768:1024]
        hm = hm_ref[...]
        qs = jnp.concatenate([q] * N_HEADS, axis=0) * hm
        sc = _bdot_nt(qs, k) * dstack_ref[...]
        of = _bdot(sc, v) * hm
        o = of[0:c]
        for h in range(1, N_HEADS):
            o = o + of[h * c:(h + 1) * c]
        s_old = s_scr[...]
        o = o + _bdot(q * qdec_ref[...], s_old)
        upd = jnp.where(jm_ref[...] > 0, _bdot_tn(k * kdec_ref[...], v), 0.0)
        s_scr[...] = sdec_ref[...] * s_old + upd
        o_ref[rows, :] = _head_group_norm(o, jn_ref[...], gain_ref[...]) * (gate * jax.nn.sigmoid(gate))
        return carry

    lax.fori_loop(0, seq // c, step, 0)
    s_fin = s_scr[...]
    for h in range(N_HEADS):
        st_ref[0, h] = s_fin[h * HEAD_DIM:(h + 1) * HEAD_DIM, h * HEAD_DIM:(h + 1) * HEAD_DIM]


def _blockdiag(kd, dk):
    r = np.arange(kd)[:, None] // dk
    cidx = np.arange(GROUP_WIDTH)[None, :] // HEAD_DIM
    return (r == cidx).astype(np.float32)


def _lin_prompt_call(body, name, x, batch, seq, kd, dk, consts):
    full = lambda a: pl.BlockSpec(a.shape, lambda b: (0,) * a.ndim)
    return pl.pallas_call(
        body,
        grid=(batch,),
        in_specs=[pl.BlockSpec((seq, x.shape[1]), lambda b: (b, 0))] + [full(a) for a in consts],
        out_specs=[pl.BlockSpec((seq, GROUP_WIDTH), lambda b: (b, 0)),
                   pl.BlockSpec((1, N_HEADS, dk, HEAD_DIM), lambda b: (b, 0, 0, 0))],
        out_shape=[jax.ShapeDtypeStruct((batch * seq, GROUP_WIDTH), F32),
                   jax.ShapeDtypeStruct((batch, N_HEADS, dk, HEAD_DIM), F32)],
        scratch_shapes=[pltpu.VMEM((kd, GROUP_WIDTH), F32)],
        compiler_params=_cparams(("parallel",)),
        name=name,
    )(x, *consts)


def gla_prompt(x, batch, seq, wa, ba, gain):
    r, c = GLA_ROWS, GLA_CHUNK
    same = (np.arange(r)[:, None] // c) == (np.arange(r)[None, :] // c)
    tri = (same & (np.arange(r)[:, None] >= np.arange(r)[None, :])).astype(np.float32)
    emask = ((np.arange(r)[:, None] // c) == np.arange(LANES)[None, :]).astype(np.float32)
    consts = [wa, ba, gain, jnp.asarray(tri), jnp.asarray(same.astype(np.float32)), jnp.asarray(emask),
              jnp.asarray(_blockdiag(128, GLA_DK)), jnp.asarray(_blockdiag(GROUP_WIDTH, HEAD_DIM) / HEAD_DIM)]
    return _lin_prompt_call(_gla_prompt_body, "gla_prompt", x, batch, seq, 128, GLA_DK, consts)


def ret_prompt(x, batch, seq, gain):
    c = RET_CHUNK
    log_gamma = np.log1p(-np.power(2.0, -5.0 - np.arange(N_HEADS, dtype=np.float64)))
    t = np.arange(c)
    diff = t[:, None] - t[None, :]
    dstack = np.concatenate([np.where(diff >= 0, np.exp(lg * np.maximum(diff, 0)), 0.0) for lg in log_gamma], axis=0)
    per_lane = np.repeat(log_gamma, HEAD_DIM)
    qdec = np.exp(per_lane[None, :] * (t[:, None] + 1))
    kdec = np.exp(per_lane[None, :] * (c - 1 - t[:, None]))
    sdec = np.broadcast_to(np.exp(per_lane * c)[:, None], (GROUP_WIDTH, GROUP_WIDTH))
    hm = np.concatenate([np.broadcast_to(np.arange(GROUP_WIDTH)[None, :] // HEAD_DIM == h, (c, GROUP_WIDTH))
                         for h in range(N_HEADS)], axis=0)
    as_f32 = lambda a: jnp.asarray(np.asarray(a, np.float32))
    consts = [gain, as_f32(dstack), as_f32(qdec), as_f32(kdec), as_f32(sdec), as_f32(hm),
              jnp.asarray(_blockdiag(GROUP_WIDTH, HEAD_DIM)),
              jnp.asarray(_blockdiag(GROUP_WIDTH, HEAD_DIM) / HEAD_DIM)]
    return _lin_prompt_call(_ret_prompt_body, "ret_prompt", x, batch, seq, GROUP_WIDTH, HEAD_DIM, consts)


def _lin_decode_body(x_ref, s0_ref, wa_ref, ba_ref, la_ref, gain_ref, o_ref, s1_ref, a_scr, o_scr,
                     *, kind, kd, dk):
    steps = x_ref.shape[0]
    if kind == "gla":
        qo, ko, vo, go = 0, 128, 256, 512
        for t in range(steps):
            pre = _hdot(wa_ref[...], x_ref[t, 768:896, :]) + ba_ref[...]
            a_scr[t] = jnp.exp(jax.nn.log_sigmoid(pre) / GLA_TAU)
        qscale, kscale = GLA_DK ** -0.5, 1.0
    else:
        qo, ko, vo, go = 0, 256, 512, 768
        for t in range(steps):
            a_scr[t] = jnp.exp(jnp.broadcast_to(la_ref[...], a_scr.shape[1:]))
        qscale, kscale = 1.0, HEAD_DIM ** -0.5
    o_scr[...] = jnp.zeros_like(o_scr)

    def body(j8, carry):
        r0 = pl.multiple_of(j8 * 8, 8)
        h = r0 // dk
        vrows = pl.ds(pl.multiple_of(vo + h * HEAD_DIM, HEAD_DIM), HEAD_DIM)
        orows = pl.ds(pl.multiple_of(h * HEAD_DIM, HEAD_DIM), HEAD_DIM)
        a8 = [a_scr[t, pl.ds(r0, 8), :] for t in range(steps)]
        q8 = [x_ref[t, pl.ds(qo + r0, 8), :] * qscale for t in range(steps)]
        k8 = [x_ref[t, pl.ds(ko + r0, 8), :] * kscale for t in range(steps)]
        for jj in range(8):
            sj = s0_ref[r0 + jj]
            for t in range(steps):
                vt = x_ref[t, vrows, :]
                sj = a8[t][jj:jj + 1, :] * sj + k8[t][jj:jj + 1, :] * vt
                o_scr[t, orows, :] += q8[t][jj:jj + 1, :] * sj
            s1_ref[r0 + jj] = sj
        return carry

    lax.fori_loop(0, kd // 8, body, 0)
    for t in range(steps):
        for h in range(N_HEADS):
            blk = slice(h * HEAD_DIM, (h + 1) * HEAD_DIM)
            o = o_scr[t, blk, :]
            mu = jnp.mean(o, axis=0, keepdims=True)
            d = o - mu
            var = jnp.mean(d * d, axis=0, keepdims=True)
            gate = x_ref[t, go + h * HEAD_DIM:go + (h + 1) * HEAD_DIM, :]
            o_ref[t, blk, :] = d * lax.rsqrt(var + LN_EPS) * gain_ref[blk, :] * (gate * jax.nn.sigmoid(gate))


def lin_decode(xt, s0, kind, wa_t, ba_col, la_col, gain_col):
    kd, dk = (128, GLA_DK) if kind == "gla" else (256, HEAD_DIM)
    steps, _, nb = xt.shape
    return pl.pallas_call(
        functools.partial(_lin_decode_body, kind=kind, kd=kd, dk=dk),
        out_shape=[jax.ShapeDtypeStruct((steps, GROUP_WIDTH, nb), F32),
                   jax.ShapeDtypeStruct((kd, HEAD_DIM, nb), F32)],
        scratch_shapes=[pltpu.VMEM((steps, kd, nb), F32), pltpu.VMEM((steps, GROUP_WIDTH, nb), F32)],
        compiler_params=pltpu.CompilerParams(vmem_limit_bytes=VMEM_LIMIT),
        name="lin_decode_" + kind,
    )(xt, s0, wa_t, ba_col, la_col, gain_col)


def _cmp_mlp(xs_list, pos_ref, w1_ref, w2_ref, n_rows):
    xcat = jnp.concatenate(
        [jnp.concatenate([(xs[pl.ds(r, n_rows, stride=CMP_STRIDE), :] + pos_ref[r:r + 1, :]).astype(BF16)
                          for r in range(CMP_LEN)], axis=1) for xs in xs_list], axis=0)
    hid = jnp.dot(xcat, w1_ref[...].reshape(CMP_LEN * LANES, 2 * CMP_HIDDEN), preferred_element_type=F32)
    return _bdot(jax.nn.gelu(hid), w2_ref[...])


def _rank_select(vals, idx, n_rows, n_top, axis):
    cnt = jnp.zeros(vals.shape, F32)
    for j in range(n_rows):
        vj = vals[j:j + 1, :] if axis == 0 else vals[:, j:j + 1]
        before = (vj > vals) | ((vj == vals) & (j < idx))
        cnt = cnt + before.astype(F32)
    return cnt < n_top


DECODE_SEQS = 4
TQ = 256
TK = 256


POS_HI_LANE = HEAD_DIM
POS_LO_LANE = HEAD_DIM + 1


def _nt(a, b):
    return lax.dot_general(a, b, (((1,), (1,)), ((), ())), preferred_element_type=F32)


def _tn(a, b):
    return lax.dot_general(a, b, (((0,), (0,)), ((), ())), preferred_element_type=F32)


def _key_tile(x128, pos):
    lane = lax.broadcasted_iota(jnp.int32, x128.shape, 1)
    feat = jnp.where(lane == POS_HI_LANE, (pos // 64).astype(F32),
                     jnp.where(lane == POS_LO_LANE, (pos % 64).astype(F32), 0.0))
    return jnp.where(lane < HEAD_DIM, x128, feat).astype(BF16)


def _query_stack(q256, slopes):
    tq = q256.shape[0]
    lane = lax.broadcasted_iota(jnp.int32, (tq, LANES), 1)
    parts = []
    for h in range(N_HEADS):
        pair = q256[:, (h // 2) * LANES:(h // 2 + 1) * LANES]
        if h % 2:
            pair = pltpu.roll(pair, HEAD_DIM, 1)
        const = jnp.where(lane == POS_HI_LANE, 64.0 * slopes[h], jnp.where(lane == POS_LO_LANE, slopes[h], 0.0))
        parts.append(jnp.where(lane < HEAD_DIM, pair * (HEAD_DIM ** -0.5), const).astype(BF16))
    return jnp.concatenate(parts, axis=0)


def _tile_step(k_fn, v_fn, qst_scr, bias, m_scr, l_scr, acc_scr, first, tq, group):
    starts = list(range(0, N_HEADS * tq, group))
    scores = []
    for c0 in starts:
        s = _nt(k_fn(c0 // tq), qst_scr[c0:c0 + group, :])
        scores.append(s if bias is None else s + bias[:, c0:c0 + group])
    probs, stats = [], []
    for c0, s in zip(starts, scores):
        cols = slice(c0, c0 + group)
        if first:
            m_new = jnp.max(s, axis=0, keepdims=True)
            p = jnp.exp(s - m_new)
            stats.append((m_new, jnp.sum(p, axis=0, keepdims=True), None))
        else:
            m_old = m_scr[:, cols]
            m_new = jnp.maximum(m_old, jnp.max(s, axis=0, keepdims=True))
            p = jnp.exp(s - m_new)
            corr = jnp.exp(m_old - m_new)
            stats.append((m_new, corr * l_scr[:, cols] + jnp.sum(p, axis=0, keepdims=True), corr))
        probs.append(p.astype(BF16))
    for c0, p, (m_new, l_new, corr) in zip(starts, probs, stats):
        cols = slice(c0, c0 + group)
        pv = _tn(v_fn(c0 // tq), p)
        acc_scr[:, cols] = pv if first else corr * acc_scr[:, cols] + pv
        m_scr[:, cols] = m_new
        l_scr[:, cols] = l_new


def _tile4(x):
    return jnp.concatenate([x] * N_HEADS, axis=1)


def _tri_bias(tq, keep_lower):
    row = lax.broadcasted_iota(jnp.int32, (TK, tq), 0)
    col = lax.broadcasted_iota(jnp.int32, (TK, tq), 1)
    return _tile4(jnp.where((row <= col) if keep_lower else (row > col), 0.0, NEG))


def _nsa_prompt_body(q_ref, gate_ref, kv_ref, win_ref, w1_ref, pos_ref, w2_ref, imp_ref, o_ref,
                     xs_scr, kc_scr, vc_scr, ks_scr, vs_scr, kw_scr, vw_scr, selb_scr,
                     qst_scr, ocmp_scr, m_scr, l_scr, acc_scr, *, seq):
    qi = pl.program_id(1)
    tq = q_ref.shape[0]
    ncp = seq // CMP_STRIDE
    n_cmp = ncp - 1
    n_sel = seq // SEL_BLOCK
    blocks_per_tile = TK // SEL_BLOCK

    @pl.when(qi == 0)
    def _():
        xs_scr[0:seq, :] = kv_ref[:, 0:128]
        xs_scr[seq:seq + CMP_STRIDE, :] = jnp.zeros((CMP_STRIDE, LANES), F32)
        kvc = _cmp_mlp([xs_scr], pos_ref, w1_ref, w2_ref, ncp)
        cend = lax.broadcasted_iota(jnp.int32, (ncp, LANES), 0) * CMP_STRIDE + (CMP_LEN - 1)
        kc_scr[...] = _key_tile(kvc, cend)
        vc_scr[...] = kvc[:, HEAD_DIM:2 * HEAD_DIM].astype(BF16)
        kpos = lax.broadcasted_iota(jnp.int32, (seq, LANES), 0)
        ks_scr[...] = _key_tile(kv_ref[:, 128:256], kpos)
        vs_scr[...] = kv_ref[:, 192:256].astype(BF16)
        kw_scr[...] = _key_tile(win_ref[:, 0:128], kpos)
        vw_scr[...] = win_ref[:, 64:128].astype(BF16)

    qpos = qi * tq + lax.broadcasted_iota(jnp.int32, (1, tq), 1)
    gates_t = jax.nn.sigmoid(gate_ref[:, 128:256]).T
    qst_scr[...] = _query_stack(q_ref[...], NSA_SLOPES)

    n_io = lax.broadcasted_iota(jnp.int32, (ncp, tq), 0)
    cmask = _tile4(((n_io * CMP_STRIDE + (CMP_LEN - 1)) <= qpos) & (n_io < n_cmp))
    s = jnp.where(cmask, _nt(kc_scr[...], qst_scr[...]), NEG)
    e = jnp.where(cmask, jnp.exp(s - jnp.max(s, axis=0, keepdims=True)), 0.0)
    pc = e / jnp.maximum(jnp.sum(e, axis=0, keepdims=True), TINY)
    ocmp_scr[...] = _tn(vc_scr[...], pc.astype(BF16))
    pc_sum = pc[:, 0:tq]
    for h in range(1, N_HEADS):
        pc_sum = pc_sum + pc[:, h * tq:(h + 1) * tq]

    imp = _xdot_left(imp_ref[...], pc_sum)
    blk = lax.broadcasted_iota(jnp.int32, (n_sel, tq), 0)
    cur = qpos // SEL_BLOCK
    forced = (blk == 0) | (blk == cur) | (blk == cur - 1)
    vals = jnp.where(blk <= cur, jnp.where(forced, BIG, imp), NEG)
    chosen = _rank_select(vals, blk, n_sel, min(N_SEL, n_sel), 0)
    selb_scr[...] = jnp.where(chosen, 0.0, NEG)

    def sel_bias(kt):
        rows = [jnp.broadcast_to(selb_scr[pl.ds(kt * blocks_per_tile + i, 1), :], (SEL_BLOCK, tq))
                for i in range(blocks_per_tile)]
        return _tile4(jnp.concatenate(rows, axis=0))

    def tile_rows(kt):
        return pl.ds(pl.multiple_of(kt * TK, TK), TK)

    def step(k_scr, v_scr, rows, bias, first):
        _tile_step(lambda h: k_scr[rows, :], lambda h: v_scr[rows, :], qst_scr, bias,
                   m_scr, l_scr, acc_scr, first, tq, 2 * tq)

    diag = tile_rows(qi)
    step(ks_scr, vs_scr, diag, sel_bias(qi) + _tri_bias(tq, True), True)

    def sel_body(kt, carry):
        step(ks_scr, vs_scr, tile_rows(kt), sel_bias(kt), False)
        return carry

    lax.fori_loop(0, qi, sel_body, 0)
    gate_row = lambda j: jnp.concatenate([gates_t[3 * h + j:3 * h + j + 1, :] for h in range(N_HEADS)], axis=1)
    out = gate_row(0) * ocmp_scr[...] + gate_row(1) * (acc_scr[...] / l_scr[...])

    step(kw_scr, vw_scr, diag, _tri_bias(tq, True), True)

    @pl.when(qi >= 1)
    def _():
        step(kw_scr, vw_scr, tile_rows(qi - 1), None, False)

    @pl.when(qi >= WINDOW // TK)
    def _():
        step(kw_scr, vw_scr, tile_rows(qi - WINDOW // TK), _tri_bias(tq, False), False)

    out = out + gate_row(2) * (acc_scr[...] / l_scr[...])
    o_ref[...] = jnp.concatenate([out[:, h * tq:(h + 1) * tq] for h in range(N_HEADS)], axis=0).T


def _cmp_to_sel(n_cmp, n_sel):
    ratio, span = SEL_BLOCK // CMP_STRIDE, CMP_LEN // CMP_STRIDE
    j, m, n = np.meshgrid(np.arange(n_sel), np.arange(ratio), np.arange(span), indexing="ij")
    i = ratio * j + m - n
    ok = (i >= 0) & (i < n_cmp)
    mat = np.zeros((n_cmp, n_sel), np.float32)
    np.add.at(mat, (i[ok], j[ok]), 1.0)
    return mat


def nsa_prompt(xq, xkv, xwg, batch, seq, w1, pos, w2):
    nq = seq // TQ
    ncp = seq // CMP_STRIDE
    n_sel = seq // SEL_BLOCK
    imp_t = np.zeros((n_sel, ncp), np.float32)
    imp_t[:, :ncp - 1] = _cmp_to_sel(ncp - 1, n_sel).T
    full = lambda *shape: pl.BlockSpec(shape, lambda b, i: (0,) * len(shape))
    return pl.pallas_call(
        functools.partial(_nsa_prompt_body, seq=seq),
        grid=(batch, nq),
        in_specs=[pl.BlockSpec((TQ, GROUP_WIDTH), lambda b, i: (b * nq + i, 0)),
                  pl.BlockSpec((TQ, GROUP_WIDTH), lambda b, i: (b * nq + i, 0)),
                  pl.BlockSpec((seq, GROUP_WIDTH), lambda b, i: (b, 0)),
                  pl.BlockSpec((seq, GROUP_WIDTH), lambda b, i: (b, 0)),
                  full(CMP_LEN, LANES, 2 * CMP_HIDDEN), full(CMP_LEN, LANES),
                  full(2 * CMP_HIDDEN, LANES), full(n_sel, ncp)],
        out_specs=pl.BlockSpec((TQ, GROUP_WIDTH), lambda b, i: (b * nq + i, 0)),
        out_shape=jax.ShapeDtypeStruct((batch * seq, GROUP_WIDTH), F32),
        scratch_shapes=[pltpu.VMEM((seq + CMP_STRIDE, LANES), F32),
                        pltpu.VMEM((ncp, LANES), BF16), pltpu.VMEM((ncp, HEAD_DIM), BF16),
                        pltpu.VMEM((seq, LANES), BF16), pltpu.VMEM((seq, HEAD_DIM), BF16),
                        pltpu.VMEM((seq, LANES), BF16), pltpu.VMEM((seq, HEAD_DIM), BF16),
                        pltpu.VMEM((n_sel, TQ), F32),
                        pltpu.VMEM((N_HEADS * TQ, LANES), BF16), pltpu.VMEM((HEAD_DIM, N_HEADS * TQ), F32),
                        pltpu.VMEM((1, N_HEADS * TQ), F32), pltpu.VMEM((1, N_HEADS * TQ), F32),
                        pltpu.VMEM((HEAD_DIM, N_HEADS * TQ), F32)],
        compiler_params=_cparams(("parallel", "arbitrary")),
        name="nsa_prompt",
    )(xq, xwg, xkv, xwg, w1, pos, w2, jnp.asarray(imp_t))


def _moba_prompt_body(q_ref, kv_ref, o_ref, km_scr, ka_scr, va_scr, selb_scr, qst_scr,
                      m_scr, l_scr, acc_scr, *, seq):
    qi = pl.program_id(1)
    tq = q_ref.shape[0]
    nb = seq // MOBA_BLOCK

    @pl.when(qi == 0)
    def _():
        for n in range(nb):
            km_scr[n:n + 1, :] = jnp.mean(kv_ref[n * MOBA_BLOCK:(n + 1) * MOBA_BLOCK, 0:GROUP_WIDTH],
                                          axis=0, keepdims=True)

        kpos = lax.broadcasted_iota(jnp.int32, (seq, LANES), 0)
        for h in range(N_HEADS):
            pair = kv_ref[:, (h // 2) * LANES:(h // 2 + 1) * LANES]
            if h % 2:
                pair = pltpu.roll(pair, HEAD_DIM, 1)
            ka_scr[h] = _key_tile(pair, kpos)
            va_scr[h] = kv_ref[:, GROUP_WIDTH + h * HEAD_DIM:GROUP_WIDTH + (h + 1) * HEAD_DIM].astype(BF16)

    qpos = qi * tq + lax.broadcasted_iota(jnp.int32, (1, tq), 1)
    own = qpos // MOBA_BLOCK
    blk = lax.broadcasted_iota(jnp.int32, (nb, tq), 0)
    past = blk < own
    for h in range(N_HEADS):
        hs = slice(h * HEAD_DIM, (h + 1) * HEAD_DIM)
        gate = _hdot_nt(km_scr[:, hs], q_ref[:, hs])
        chosen = _rank_select(jnp.where(past, gate, NEG), blk, nb, min(MOBA_TOPK, nb), 0) & past
        selb_scr[h] = jnp.where(chosen, 0.0, NEG)

    qst_scr[...] = _query_stack(q_ref[...], MOBA_SLOPES)

    def step(rows, bias, first):
        _tile_step(lambda h: ka_scr[h, rows, :], lambda h: va_scr[h, rows, :], qst_scr, bias,
                   m_scr, l_scr, acc_scr, first, tq, tq)

    step(pl.ds(pl.multiple_of(qi * TK, TK), TK), _tri_bias(tq, True), True)

    def body(kt, carry):
        bias = jnp.concatenate([selb_scr[h, pl.ds(kt, 1), :] for h in range(N_HEADS)], axis=1)
        step(pl.ds(pl.multiple_of(kt * TK, TK), TK), bias, False)
        return carry

    lax.fori_loop(0, qi, body, 0)
    out = acc_scr[...] / l_scr[...]
    o_ref[...] = jnp.concatenate([out[:, h * tq:(h + 1) * tq] for h in range(N_HEADS)], axis=0).T


def moba_prompt(xq, xkv, batch, seq):
    nq = seq // TQ
    nb = seq // MOBA_BLOCK
    return pl.pallas_call(
        functools.partial(_moba_prompt_body, seq=seq),
        grid=(batch, nq),
        in_specs=[pl.BlockSpec((TQ, GROUP_WIDTH), lambda b, i: (b * nq + i, 0)),
                  pl.BlockSpec((seq, 2 * GROUP_WIDTH), lambda b, i: (b, 0))],
        out_specs=pl.BlockSpec((TQ, GROUP_WIDTH), lambda b, i: (b * nq + i, 0)),
        out_shape=jax.ShapeDtypeStruct((batch * seq, GROUP_WIDTH), F32),
        scratch_shapes=[pltpu.VMEM((nb, GROUP_WIDTH), F32),
                        pltpu.VMEM((N_HEADS, seq, LANES), BF16), pltpu.VMEM((N_HEADS, seq, HEAD_DIM), BF16),
                        pltpu.VMEM((N_HEADS, nb, TQ), F32),
                        pltpu.VMEM((N_HEADS * TQ, LANES), BF16),
                        pltpu.VMEM((1, N_HEADS * TQ), F32), pltpu.VMEM((1, N_HEADS * TQ), F32),
                        pltpu.VMEM((HEAD_DIM, N_HEADS * TQ), F32)],
        compiler_params=_cparams(("parallel", "arbitrary")),
        name="moba_prompt",
    )(xq, xkv)


def _softmax_rows(parts, masks):
    masked = [jnp.where(mk, s, NEG) for s, mk in zip(parts, masks)]
    m = masked[0].max(axis=1, keepdims=True)
    for s in masked[1:]:
        m = jnp.maximum(m, s.max(axis=1, keepdims=True))
    es = [jnp.where(mk, jnp.exp(s - m), 0.0) for s, mk in zip(masked, masks)]
    den = es[0].sum(axis=1, keepdims=True)
    for e in es[1:]:
        den = den + e.sum(axis=1, keepdims=True)
    inv = 1.0 / jnp.maximum(den, TINY)
    return [e * inv for e in es]


def _stack_heads(x, off):
    return jnp.concatenate([x[:, off + h * HEAD_DIM:off + (h + 1) * HEAD_DIM] for h in range(N_HEADS)], axis=0)


def _round_robin(gens):
    outs = [None] * len(gens)
    live = list(range(len(gens)))
    while live:
        for i in list(live):
            try:
                next(gens[i])
            except StopIteration as stop:
                outs[i] = stop.value
                live.remove(i)
    return outs


def _pad_rows(x, n):
    return jnp.concatenate([x, jnp.zeros((n - x.shape[0], x.shape[1]), x.dtype)], axis=0)


def _nsa_decode_body(*refs, n_pages, past_len, ns):
    x_refs = refs[1:4]
    all_pages = refs[4:4 + ns * n_pages]
    (win_ref, w1_ref, pos_ref, w2_ref, imp_ref, esel_ref, place_ref, _, o_ref, wout_ref,
     xs_scr) = refs[4 + ns * n_pages:]
    steps = x_refs[0].shape[0] // ns
    x_rows = lambda i: jnp.concatenate([r[i * steps:(i + 1) * steps, :] for r in x_refs], axis=1)
    ncp = past_len // CMP_STRIDE
    n_win = win_ref.shape[-1]
    lane = lax.broadcasted_iota(jnp.int32, (HEAD_DIM, n_win), 1)
    for i in range(ns):
        new_rows = x_refs[2][i * steps:(i + 1) * steps, 0:2 * HEAD_DIM]
        for c in range(2):
            placed = _hdot_tn(new_rows[:, c * HEAD_DIM:(c + 1) * HEAD_DIM], place_ref[...])
            shifted = pltpu.roll(win_ref[0, i, c], n_win - steps, 1)
            wout_ref[0, i, c] = jnp.where(lane < n_win - steps, shifted, placed)
    for i in range(ns):
        for p in range(n_pages):
            xs_scr[i, p * PAGE_SIZE:(p + 1) * PAGE_SIZE, :] = (
                all_pages[i * n_pages + p][0, 0, 0:2].reshape(LANES, PAGE_SIZE).T)
        xs_scr[i, past_len:past_len + CMP_STRIDE, :] = jnp.zeros((CMP_STRIDE, LANES), F32)
    kvc_all = _cmp_mlp([xs_scr.at[i] for i in range(ns)], pos_ref, w1_ref, w2_ref, ncp)
    outs = _round_robin([_nsa_decode_seq(
        x_rows(i), all_pages[i * n_pages:(i + 1) * n_pages], win_ref, i,
        kvc_all[i * ncp:(i + 1) * ncp], imp_ref, esel_ref, n_pages, past_len) for i in range(ns)])
    for i in range(ns):
        o_ref[i * steps:(i + 1) * steps, :] = outs[i]


def _nsa_decode_seq(x, page_refs, win_ref, wi, kvc, imp_ref, esel_ref, n_pages, past_len):
    steps = x.shape[0]
    rows = N_HEADS * steps
    ncp = past_len // CMP_STRIDE
    n_cmp = ncp - 1
    tpos = past_len + lax.broadcasted_iota(jnp.int32, (steps, 1), 0)
    qpos = jnp.concatenate([tpos] * N_HEADS, axis=0)
    slope = jnp.concatenate([jnp.full((steps, 1), s, F32) for s in NSA_SLOPES], axis=0)
    qs = _stack_heads(x, 0) * (HEAD_DIM ** -0.5)
    gates = jax.nn.sigmoid(x[:, 640:768])
    gcol = [jnp.concatenate([gates[:, 3 * h + j:3 * h + j + 1] for h in range(N_HEADS)], axis=0)
            for j in range(3)]
    new_io = lax.broadcasted_iota(jnp.int32, (rows, LANES), 1)
    new_pos = past_len + new_io
    new_ok = (new_io < steps) & (new_pos <= qpos)

    k_cmp, v_cmp = kvc[:, 0:HEAD_DIM], kvc[:, HEAD_DIM:2 * HEAD_DIM]
    n_io = lax.broadcasted_iota(jnp.int32, (rows, ncp), 1)
    cend = n_io * CMP_STRIDE + (CMP_LEN - 1)
    (pc,) = _softmax_rows([_bdot_nt(qs, k_cmp) + slope * cend.astype(F32)],
                          [(cend <= qpos) & (n_io < n_cmp)])
    o_cmp = _bdot(pc, v_cmp)
    yield
    pc_sum = pc[0:steps]
    for h in range(1, N_HEADS):
        pc_sum = pc_sum + pc[h * steps:(h + 1) * steps]

    n_sel = -(-(past_len + steps) // SEL_BLOCK)
    imp = _hdot(pc_sum, imp_ref[...])
    yield
    blk = lax.broadcasted_iota(jnp.int32, (steps, LANES), 1)
    cur = tpos // SEL_BLOCK
    forced = (blk == 0) | (blk == cur) | (blk == cur - 1)
    vals = jnp.where((blk <= cur) & (blk < n_sel), jnp.where(forced, BIG, imp), NEG)
    chosen = _rank_select(vals, blk, n_sel, min(N_SEL, n_sel), 1).astype(F32)
    key_sel = _bdot(chosen, esel_ref[...])
    key_sel = jnp.concatenate([key_sel] * N_HEADS, axis=0) > 0.5
    yield

    cat_pages = lambda c: jnp.concatenate([page_refs[p][0, 0, c].astype(BF16) for p in range(n_pages)], axis=1)
    s_past = _bdot(qs, cat_pages(2))
    kpos = lax.broadcasted_iota(jnp.int32, (rows, past_len), 1)
    k_new = _pad_rows(x[:, 384:448], LANES)
    v_new = _pad_rows(x[:, 448:512], LANES)
    s_new = _bdot_nt(qs, k_new) + slope * new_pos.astype(F32)
    yield
    p_past, p_new = _softmax_rows([s_past + slope * kpos.astype(F32), s_new],
                                  [key_sel[:, 0:past_len] & (kpos <= qpos),
                                   key_sel[:, past_len:past_len + LANES] & new_ok])
    o_sel = _bdot(p_new, v_new) + _bdot_nt(p_past, cat_pages(3))
    yield

    n_win = win_ref.shape[-1]
    wpos = (past_len - n_win) + lax.broadcasted_iota(jnp.int32, (rows, n_win), 1)
    kw_new = _pad_rows(x[:, 512:576], LANES)
    vw_new = _pad_rows(x[:, 576:640], LANES)
    s_w = _bdot(qs, win_ref[0, wi, 0]) + slope * wpos.astype(F32)
    s_wn = _bdot_nt(qs, kw_new) + slope * new_pos.astype(F32)
    yield
    dist = qpos - wpos
    p_w, p_wn = _softmax_rows([s_w, s_wn],
                              [(dist >= 0) & (dist < WINDOW), new_ok & (qpos - new_pos < WINDOW)])
    o_win = _bdot_nt(p_w, win_ref[0, wi, 1]) + _bdot(p_wn, vw_new)

    out = gcol[0] * o_cmp + gcol[1] * o_sel + gcol[2] * o_win
    return jnp.concatenate([out[h * steps:(h + 1) * steps] for h in range(N_HEADS)], axis=1)


def nsa_decode(xs, page_ids, cache_t, layer, win_t, win_next, w1, pos, w2, *, steps, past_len):
    nseq = xs[0].shape[0] // steps
    n_pages = past_len // PAGE_SIZE
    ncp = past_len // CMP_STRIDE
    n_sel = -(-(past_len + steps) // SEL_BLOCK)
    imp = np.zeros((ncp, LANES), np.float32)
    imp[:ncp - 1, :n_sel] = _cmp_to_sel(ncp - 1, n_sel)
    kblk = np.concatenate([np.arange(past_len) // SEL_BLOCK,
                           (past_len + np.arange(LANES)) // SEL_BLOCK])
    esel = (np.arange(LANES)[:, None] == kblk[None, :]).astype(np.float32)
    n_win = win_t.shape[-1]

    ns = math.gcd(nseq, DECODE_SEQS)

    def page_spec(j):
        return pl.BlockSpec((1, 1, 4, HEAD_DIM, PAGE_SIZE),
                            lambda b, pt, j=j: (pt[b * (ns * n_pages) + j], layer, 0, 0, 0))

    place = (np.arange(n_win)[None, :] == (n_win - steps + np.arange(steps))[:, None]).astype(np.float32)
    full = lambda *shape: pl.BlockSpec(shape, lambda b, pt: (0,) * len(shape))
    win_spec = pl.BlockSpec((1, ns, 2, HEAD_DIM, n_win), lambda b, pt: (layer, b, 0, 0, 0))
    inputs = (page_ids, *xs, *([cache_t] * (ns * n_pages)), win_t, w1, pos, w2, jnp.asarray(imp),
              jnp.asarray(esel), jnp.asarray(place), win_next)
    return pl.pallas_call(
        functools.partial(_nsa_decode_body, n_pages=n_pages, past_len=past_len, ns=ns),
        grid_spec=pltpu.PrefetchScalarGridSpec(
            num_scalar_prefetch=1,
            grid=(nseq // ns,),
            in_specs=[pl.BlockSpec((ns * steps, GROUP_WIDTH), lambda b, pt: (b, 0)) for _ in xs]
            + [page_spec(j) for j in range(ns * n_pages)]
            + [win_spec,
               full(CMP_LEN, LANES, 2 * CMP_HIDDEN), full(CMP_LEN, LANES),
               full(2 * CMP_HIDDEN, LANES), full(ncp, LANES), full(LANES, past_len + LANES),
               full(steps, n_win), pl.BlockSpec(memory_space=pl.ANY)],
            out_specs=[pl.BlockSpec((ns * steps, GROUP_WIDTH), lambda b, pt: (b, 0)), win_spec],
            scratch_shapes=[pltpu.VMEM((ns, past_len + CMP_STRIDE, LANES), F32)],
        ),
        out_shape=[jax.ShapeDtypeStruct((nseq * steps, GROUP_WIDTH), F32),
                   jax.ShapeDtypeStruct(win_next.shape, F32)],
        input_output_aliases={len(inputs) - 1: 1},
        compiler_params=_cparams(("parallel",)),
        name="nsa_decode",
    )(*inputs)


def _moba_decode_body(*refs, n_pages, past_len, ns):
    x_refs = refs[1:3]
    all_pages = refs[3:3 + ns * n_pages]
    emean_ref, eblk_ref, o_ref = refs[3 + ns * n_pages:]
    steps = x_refs[0].shape[0] // ns
    x_rows = lambda i: jnp.concatenate([r[i * steps:(i + 1) * steps, :] for r in x_refs], axis=1)
    outs = _round_robin([_moba_decode_seq(
        x_rows(i), all_pages[i * n_pages:(i + 1) * n_pages],
        emean_ref, eblk_ref, n_pages, past_len) for i in range(ns)])
    for i in range(ns):
        o_ref[i * steps:(i + 1) * steps, :] = outs[i]


def _moba_decode_seq(x, page_refs, emean_ref, eblk_ref, n_pages, past_len):
    steps = x.shape[0]
    rows = N_HEADS * steps
    tpos = past_len + lax.broadcasted_iota(jnp.int32, (steps, 1), 0)
    qpos = jnp.concatenate([tpos] * N_HEADS, axis=0)
    own = qpos // MOBA_BLOCK
    slope = jnp.concatenate([jnp.full((steps, 1), s, F32) for s in MOBA_SLOPES], axis=0)
    head_of_row = lax.broadcasted_iota(jnp.int32, (rows, GROUP_WIDTH), 0) // steps
    head_of_col = lax.broadcasted_iota(jnp.int32, (rows, GROUP_WIDTH), 1) // HEAD_DIM
    diag = head_of_row == head_of_col
    q_bd = jnp.where(diag, jnp.concatenate([x[:, 0:GROUP_WIDTH]] * N_HEADS, axis=0), 0.0)

    cat_pages = lambda c: jnp.concatenate(
        [page_refs[p][0, 0, c].reshape(GROUP_WIDTH, PAGE_SIZE).astype(BF16) for p in range(n_pages)], axis=1)
    kt_all = cat_pages(0)
    kmean_t = jnp.dot(kt_all, emean_ref[...], preferred_element_type=F32)
    yield
    gate = _hdot(q_bd, kmean_t)
    yield
    nb = -(-(past_len + steps) // MOBA_BLOCK)
    blk = lax.broadcasted_iota(jnp.int32, (rows, LANES), 1)
    past = (blk < own) & (blk < nb)
    vals = jnp.where(past, gate, NEG)
    chosen = (_rank_select(vals, blk, nb, min(MOBA_TOPK, nb), 1) & past).astype(F32)
    key_sel = _bdot(chosen, eblk_ref[...]) > 0.5
    yield

    q_sc = q_bd * (HEAD_DIM ** -0.5)
    s_past = _bdot(q_sc, kt_all)
    kpos = lax.broadcasted_iota(jnp.int32, (rows, past_len), 1)
    new_io = lax.broadcasted_iota(jnp.int32, (rows, LANES), 1)
    new_pos = past_len + new_io
    k_new = _pad_rows(x[:, 256:512], LANES)
    v_new = _pad_rows(x[:, 512:768], LANES)
    s_new = _bdot_nt(q_sc, k_new) + slope * new_pos.astype(F32)
    yield
    in_own_past = (kpos // MOBA_BLOCK) == own
    in_own_new = (new_pos // MOBA_BLOCK) == own
    p_past, p_new = _softmax_rows(
        [s_past + slope * kpos.astype(F32), s_new],
        [key_sel | in_own_past, (new_io < steps) & (new_pos <= qpos) & in_own_new])
    o_all = _bdot(p_new, v_new) + _bdot_nt(p_past, cat_pages(1))
    o_all = jnp.where(diag, o_all, 0.0)
    out = o_all[0:steps]
    for h in range(1, N_HEADS):
        out = out + o_all[h * steps:(h + 1) * steps]
    return out


def moba_decode(xs, page_ids, cache_t, layer, *, steps, past_len):
    nseq = xs[0].shape[0] // steps
    n_pages = past_len // PAGE_SIZE
    pages_per_blk = MOBA_BLOCK // PAGE_SIZE
    emean = np.zeros((n_pages, PAGE_SIZE, LANES), np.float32)
    for p in range(n_pages):
        emean[p, :, p // pages_per_blk] = 1.0 / MOBA_BLOCK
    eblk = (np.arange(LANES)[:, None] == (np.arange(past_len) // MOBA_BLOCK)[None, :]).astype(np.float32)

    ns = math.gcd(nseq, DECODE_SEQS)

    def page_spec(j):
        return pl.BlockSpec((1, 1, 2, N_HEADS, HEAD_DIM, PAGE_SIZE),
                            lambda b, pt, j=j: (pt[b * (ns * n_pages) + j], layer, 0, 0, 0, 0))

    full = lambda *shape: pl.BlockSpec(shape, lambda b, pt: (0,) * len(shape))
    return pl.pallas_call(
        functools.partial(_moba_decode_body, n_pages=n_pages, past_len=past_len, ns=ns),
        grid_spec=pltpu.PrefetchScalarGridSpec(
            num_scalar_prefetch=1,
            grid=(nseq // ns,),
            in_specs=[pl.BlockSpec((ns * steps, a.shape[1]), lambda b, pt: (b, 0)) for a in xs]
            + [page_spec(j) for j in range(ns * n_pages)]
            + [full(past_len, LANES), full(LANES, past_len)],
            out_specs=pl.BlockSpec((ns * steps, GROUP_WIDTH), lambda b, pt: (b, 0)),
        ),
        out_shape=jax.ShapeDtypeStruct((nseq * steps, GROUP_WIDTH), F32),
        compiler_params=_cparams(("parallel",)),
        name="moba_decode",
    )(page_ids, *xs, *([cache_t] * (ns * n_pages)), jnp.asarray(emean.reshape(past_len, LANES), BF16),
      jnp.asarray(eblk))


def _regroup_w_in(w):
    z = lambda n: jnp.zeros(w.shape[:2] + (n,), w.dtype)
    col = lambda a, b: w[:, :, a:b]
    gq, gk, gv, ga, gr = col(0, 128), col(128, 256), col(256, 512), col(512, 528), col(528, 784)
    ret = col(784, 1808)
    nq, nkv, ng = col(1808, 2064), col(2064, 2448), col(2448, 2460)
    moba = col(2460, 3228)
    return jnp.concatenate([gq, gk, gv, gr, ga, z(112), ret, nq, nkv, ng, z(116), moba], axis=2).astype(BF16)


def _cmp_weights(w1, pos, w2):
    w1r = w1.reshape(2, CMP_LEN, HEAD_DIM, CMP_HIDDEN)
    zero = jnp.zeros((CMP_LEN, HEAD_DIM, CMP_HIDDEN), w1.dtype)
    w1bd = jnp.concatenate([jnp.concatenate([w1r[0], zero], axis=2),
                            jnp.concatenate([zero, w1r[1]], axis=2)], axis=1)
    zero2 = jnp.zeros((CMP_HIDDEN, HEAD_DIM), w2.dtype)
    w2bd = jnp.concatenate([jnp.concatenate([w2[0], zero2], axis=1),
                            jnp.concatenate([zero2, w2[1]], axis=1)], axis=0)
    posf = jnp.concatenate([pos[0], pos[1]], axis=1)
    return w1bd.astype(BF16), posf, w2bd.astype(BF16)


def kernel(x_prompt, x_sample, cache_nsa, cache_moba, cache_nsa_win, state_gla, state_ret, page_table, w_in, gla_w_a2, gla_b_a, gla_norm, ret_norm, nsa_cmp_w1, nsa_cmp_w2, nsa_cmp_pos, w_out, ffn_w_up, ffn_w_down, ln_g, ln_b):
    bp, sp, d = x_prompt.shape
    bs, ss, _ = x_sample.shape
    past_len = page_table.shape[1] * PAGE_SIZE
    assert cache_nsa_win.shape[2] >= ss, "the sample group's window buffer must hold at least the new rows"

    nsa_t = jnp.transpose(cache_nsa, (0, 1, 3, 4, 2))
    moba_t = jnp.transpose(cache_moba, (0, 1, 3, 4, 5, 2))
    win_t = jnp.transpose(cache_nsa_win, (0, 1, 3, 4, 2))
    gla_t = jnp.transpose(state_gla, (0, 2, 3, 4, 1)).reshape(DEPTH, N_HEADS * GLA_DK, HEAD_DIM, bs)
    ret_t = jnp.transpose(state_ret, (0, 2, 3, 4, 1)).reshape(DEPTH, N_HEADS * HEAD_DIM, HEAD_DIM, bs)
    page_ids = page_table.reshape(-1)

    log_gamma = np.log1p(-np.power(2.0, -5.0 - np.arange(N_HEADS, dtype=np.float64))).astype(np.float32)
    la_ret = jnp.asarray(np.repeat(log_gamma, HEAD_DIM))
    win_next = jnp.zeros(win_t.shape, F32)

    xp = x_prompt.reshape(bp * sp, d)
    xs = x_sample.reshape(bs * ss, d)
    outs = {k: [] for k in ("nsa_p", "moba_p", "win_p", "gla_p", "ret_p", "nsa_s", "moba_s", "gla_s", "ret_s")}
    wu = ffn_w_up.astype(BF16)
    wd = ffn_w_down.astype(BF16)
    wi = _regroup_w_in(w_in)
    wo = w_out.astype(BF16)
    for l in range(DEPTH):
        g = ln_g[l].reshape(3, 1, d)
        b = ln_b[l].reshape(3, 1, d)
        wa = jnp.zeros((128, 128), F32).at[0:GLA_RANK, :].set(gla_w_a2[l])
        ba = gla_b_a[l].reshape(1, 128)
        gn = gla_norm[l].reshape(1, GROUP_WIDTH)
        rn = ret_norm[l].reshape(1, GROUP_WIDTH)
        w1bd, posf, w2bd = _cmp_weights(nsa_cmp_w1[l], nsa_cmp_pos[l], nsa_cmp_w2[l])

        xp = ffn_ln(xp, wu, wd, l, g[0], b[0])
        pg, pr, nq, nkv, nwg, mq, mkv = proj(xp, wi, l)
        o_gla, st_gla = gla_prompt(pg, bp, sp, wa, ba, gn)
        o_ret, st_ret = ret_prompt(pr, bp, sp, rn)
        o_nsa = nsa_prompt(nq, nkv, nwg, bp, sp, w1bd, posf, w2bd)
        o_moba = moba_prompt(mq, mkv, bp, sp)
        xp = out_ffn_ln(xp, (o_gla, o_ret, o_nsa, o_moba), wo, wu, wd, l, g[1], b[1], g[2], b[2])
        outs["nsa_p"].append(nkv.reshape(bp, sp, 4, HEAD_DIM))
        outs["moba_p"].append(mkv.reshape(bp, sp, 2, N_HEADS, HEAD_DIM))
        keep = min(WINDOW, sp)
        outs["win_p"].append(nwg.reshape(bp, sp, GROUP_WIDTH)[:, sp - keep:, 0:128].reshape(bp, keep, 2, HEAD_DIM))
        outs["gla_p"].append(st_gla)
        outs["ret_p"].append(st_ret)

        xs = ffn_ln(xs, wu, wd, l, g[0], b[0])
        pg, pr, nq, nkv, nwg, mq, mkv = proj(xs, wi, l)
        to_lanes = lambda a: jnp.transpose(a.reshape(bs, ss, a.shape[1]), (1, 2, 0))
        og_t, sg_t = lin_decode(to_lanes(pg), gla_t[l], "gla", wa.T,
                                ba.reshape(128, 1), jnp.zeros((128, 1), F32), gn.reshape(GROUP_WIDTH, 1))
        or_t, sr_t = lin_decode(to_lanes(pr), ret_t[l], "ret", jnp.zeros((128, 128), F32),
                                jnp.zeros((128, 1), F32), la_ret.reshape(256, 1), rn.reshape(GROUP_WIDTH, 1))
        from_lanes = lambda a: jnp.transpose(a, (2, 0, 1)).reshape(bs * ss, GROUP_WIDTH)
        o_nsa, win_next = nsa_decode((nq, nkv, nwg), page_ids, nsa_t, l, win_t, win_next, w1bd, posf, w2bd,
                                     steps=ss, past_len=past_len)
        o_moba = moba_decode((mq, mkv), page_ids, moba_t, l, steps=ss, past_len=past_len)
        xs = out_ffn_ln(xs, (from_lanes(og_t), from_lanes(or_t), o_nsa, o_moba), wo, wu, wd, l,
                        g[1], b[1], g[2], b[2])
        outs["nsa_s"].append(nkv.reshape(bs, ss, 4, HEAD_DIM))
        outs["moba_s"].append(mkv.reshape(bs, ss, 2, N_HEADS, HEAD_DIM))
        outs["gla_s"].append(jnp.transpose(sg_t.reshape(N_HEADS, GLA_DK, HEAD_DIM, bs), (3, 0, 1, 2)))
        outs["ret_s"].append(jnp.transpose(sr_t.reshape(N_HEADS, HEAD_DIM, HEAD_DIM, bs), (3, 0, 1, 2)))

    win_s = jnp.transpose(win_next, (0, 1, 4, 2, 3))
    return (xp.reshape(bp, sp, d), xs.reshape(bs, ss, d),
            jnp.stack(outs["nsa_p"], axis=1), jnp.stack(outs["moba_p"], axis=1),
            jnp.stack(outs["win_p"], axis=0), jnp.stack(outs["gla_p"], axis=0), jnp.stack(outs["ret_p"], axis=0),
            jnp.stack(outs["nsa_s"], axis=1), jnp.stack(outs["moba_s"], axis=1),
            win_s, jnp.stack(outs["gla_s"], axis=0), jnp.stack(outs["ret_s"], axis=0))
```

```python
import functools
import math

import numpy as np
import jax
import jax.numpy as jnp
from jax import lax
from jax.experimental import pallas as pl
from jax.experimental.pallas import tpu as pltpu

F32 = jnp.float32
BF16 = jnp.bfloat16
HI = lax.Precision.HIGHEST

D_MODEL = 1024
DEPTH = 4
PAGE_SIZE = 128
HEAD_DIM = 64
N_HEADS = 4
GROUP_WIDTH = 256
GLA_DK = 32
GLA_RANK = 16
GLA_TAU = 16.0
CMP_LEN = 32
CMP_STRIDE = 16
CMP_HIDDEN = 128
SEL_BLOCK = 64
N_SEL = 8
WINDOW = 512
MOBA_BLOCK = 256
MOBA_TOPK = 3
D_FF = 2816
ALPHA = (2 * DEPTH) ** 0.25
LN_EPS = 1e-5
NEG = -1e30
BIG = 1e30
TINY = 1e-30

LANES = 128
VMEM_LIMIT = 56 * 1024 * 1024

GLA_W = 896
RET_W = 1024
NSA_W = 768
MOBA_W = 768
PROJ_SPLITS = (GLA_W, RET_W, 256, 256, 256, 256, 512)


def _slopes():
    n = 2 * N_HEADS
    s = [2.0 ** (-8.0 * i / n) for i in range(1, n + 1)]
    return s[0::2], s[1::2]


NSA_SLOPES, MOBA_SLOPES = _slopes()


def _bdot(a, b):
    return jnp.dot(a.astype(BF16), b.astype(BF16), preferred_element_type=F32)


def _bdot_nt(a, b):
    return lax.dot_general(a.astype(BF16), b.astype(BF16), (((1,), (1,)), ((), ())),
                           preferred_element_type=F32)


def _bdot_tn(a, b):
    return lax.dot_general(a.astype(BF16), b.astype(BF16), (((0,), (0,)), ((), ())),
                           preferred_element_type=F32)


def _hdot(a, b):
    return jnp.dot(a, b, precision=HI, preferred_element_type=F32)


def _hdot_nt(a, b):
    return lax.dot_general(a, b, (((1,), (1,)), ((), ())), precision=HI, preferred_element_type=F32)


def _hdot_tn(a, b):
    return lax.dot_general(a, b, (((0,), (0,)), ((), ())), precision=HI, preferred_element_type=F32)


def _split3(a):
    hi = a.astype(BF16)
    r1 = a - hi.astype(F32)
    mid = r1.astype(BF16)
    lo = (r1 - mid.astype(F32)).astype(BF16)
    return hi, mid, lo


def _xdot(a, c, dims=(((1,), (0,)), ((), ()))):
    cb = c.astype(BF16)
    return sum(lax.dot_general(part, cb, dims, preferred_element_type=F32) for part in _split3(a))


def _xdot_left(c, a):
    cb = c.astype(BF16)
    return sum(jnp.dot(cb, part, preferred_element_type=F32) for part in _split3(a))


def _layer_norm(y, g, b):
    mu = jnp.mean(y, axis=-1, keepdims=True)
    d = y - mu
    var = jnp.mean(d * d, axis=-1, keepdims=True)
    return d * lax.rsqrt(var + LN_EPS) * g + b


def _cparams(sem):
    return pltpu.CompilerParams(dimension_semantics=sem, vmem_limit_bytes=VMEM_LIMIT)


MXU_TILE = 256


def _swiglu(xb, wup_ref, wdn_ref):
    f = wdn_ref.shape[0]
    cut = min(f, (f // 2 // MXU_TILE + 1) * MXU_TILE)
    out = None
    for a0, a1 in ((0, cut), (cut, f)):
        if a1 == a0:
            continue
        u = jnp.dot(xb, wup_ref[:, a0:a1], preferred_element_type=F32)
        gt = jnp.dot(xb, wup_ref[:, f + a0:f + a1], preferred_element_type=F32)
        a = (gt * jax.nn.sigmoid(gt) * u).astype(BF16)
        part = jnp.dot(a, wdn_ref[a0:a1, :], preferred_element_type=F32)
        out = part if out is None else out + part
    return out


def _ffn_ln_body(x_ref, wup_ref, wdn_ref, g_ref, b_ref, o_ref):
    x = x_ref[...]
    y = ALPHA * x + 0.5 * _swiglu(x.astype(BF16), wup_ref, wdn_ref)
    o_ref[...] = _layer_norm(y, g_ref[...], b_ref[...])


def ffn_ln(x, w_up, w_down, layer, g, b, *, tm=512):
    m, d = x.shape
    f = w_down.shape[2]
    tm = min(tm, m)
    return pl.pallas_call(
        _ffn_ln_body,
        grid=(m // tm,),
        in_specs=[
            pl.BlockSpec((tm, d), lambda i: (i, 0)),
            pl.BlockSpec((None, None, d, 2 * f), lambda i: (layer, 0, 0, 0)),
            pl.BlockSpec((None, None, f, d), lambda i: (layer, 0, 0, 0)),
            pl.BlockSpec((1, d), lambda i: (0, 0)),
            pl.BlockSpec((1, d), lambda i: (0, 0)),
        ],
        out_specs=pl.BlockSpec((tm, d), lambda i: (i, 0)),
        out_shape=jax.ShapeDtypeStruct((m, d), F32),
        compiler_params=_cparams(("parallel",)),
        name="ffn_ln",
    )(x, w_up, w_down, g, b)


def _proj_body(x_ref, w_ref, *o_refs):
    xb = x_ref[...].astype(BF16)
    off = 0
    for o_ref in o_refs:
        wdt = o_ref.shape[1]
        o_ref[...] = jnp.dot(xb, w_ref[:, off:off + wdt], preferred_element_type=F32)
        off += wdt


def proj(x, w, layer, *, tm=512):
    m, d = x.shape
    n = w.shape[2]
    tm = min(tm, m)
    return pl.pallas_call(
        _proj_body,
        grid=(m // tm,),
        in_specs=[pl.BlockSpec((tm, d), lambda i: (i, 0)),
                  pl.BlockSpec((None, d, n), lambda i: (layer, 0, 0))],
        out_specs=[pl.BlockSpec((tm, wdt), lambda i: (i, 0)) for wdt in PROJ_SPLITS],
        out_shape=[jax.ShapeDtypeStruct((m, wdt), F32) for wdt in PROJ_SPLITS],
        compiler_params=_cparams(("parallel",)),
        name="proj",
    )(x, w)


def _out_ffn_ln_body(x_ref, o0_ref, o1_ref, o2_ref, o3_ref, wo_ref, g1_ref, b1_ref,
                     wup_ref, wdn_ref, g2_ref, b2_ref, y_ref):
    mix = None
    for gi, o_ref in enumerate((o0_ref, o1_ref, o2_ref, o3_ref)):
        part = jnp.dot(o_ref[...].astype(BF16), wo_ref[gi * GROUP_WIDTH:(gi + 1) * GROUP_WIDTH, :],
                       preferred_element_type=F32)
        mix = part if mix is None else mix + part
    mid = _layer_norm(ALPHA * x_ref[...] + mix, g1_ref[...], b1_ref[...])
    y = ALPHA * mid + 0.5 * _swiglu(mid.astype(BF16), wup_ref, wdn_ref)
    y_ref[...] = _layer_norm(y, g2_ref[...], b2_ref[...])


def out_ffn_ln(x, outs, w_out, w_up, w_down, layer, g1, b1, g2, b2, *, tm=512):
    m, d = x.shape
    f = w_down.shape[2]
    tm = min(tm, m)
    row = lambda i: (i, 0)
    vec = pl.BlockSpec((1, d), lambda i: (0, 0))
    return pl.pallas_call(
        _out_ffn_ln_body,
        grid=(m // tm,),
        in_specs=[pl.BlockSpec((tm, d), row)]
        + [pl.BlockSpec((tm, GROUP_WIDTH), row) for _ in range(4)]
        + [pl.BlockSpec((None, d, d), lambda i: (layer, 0, 0)), vec, vec,
           pl.BlockSpec((None, None, d, 2 * f), lambda i: (layer, 1, 0, 0)),
           pl.BlockSpec((None, None, f, d), lambda i: (layer, 1, 0, 0)), vec, vec],
        out_specs=pl.BlockSpec((tm, d), row),
        out_shape=jax.ShapeDtypeStruct((m, d), F32),
        compiler_params=_cparams(("parallel",)),
        name="out_ffn_ln",
    )(x, *outs, w_out, g1, b1, w_up, w_down, g2, b2)


def _head_group_norm(o, jn, gain):
    mu = _xdot(o, jn)
    d = o - mu
    var = _xdot(d * d, jn)
    return d * lax.rsqrt(var + LN_EPS) * gain


GLA_CHUNK = 16
GLA_ROWS = 128
RET_CHUNK = 128


def _gla_prompt_body(x_ref, wa_ref, ba_ref, gain_ref, tri_ref, tot_ref, emask_ref, jm_ref, jn_ref,
                     o_ref, st_ref, s_scr):
    c, r = GLA_CHUNK, GLA_ROWS
    g = r // c
    kd = 128
    seq = x_ref.shape[0]
    s_scr[...] = jnp.zeros_like(s_scr)
    jm = jm_ref[...]
    si = lax.broadcasted_iota(jnp.int32, (g, c, c, kd), 1)
    ti = lax.broadcasted_iota(jnp.int32, (g, c, c, kd), 2)
    causal = ti >= si

    def step(i, carry):
        rows = pl.ds(pl.multiple_of(i * r, r), r)
        q = x_ref[rows, 0:128] * (GLA_DK ** -0.5)
        k = x_ref[rows, 128:256]
        v = x_ref[rows, 256:512]
        gate = x_ref[rows, 512:768]
        la = jax.nn.log_sigmoid(_hdot(x_ref[rows, 768:896], wa_ref[...]) + ba_ref[...]) / GLA_TAU
        bt = _hdot(tri_ref[...], la)
        btot = _hdot(tot_ref[...], la)
        bt4 = bt.reshape(g, c, kd)
        q4 = q.reshape(g, c, kd)
        k4 = k.reshape(g, c, kd)
        dlt = jnp.minimum(bt4[:, None, :, :] - bt4[:, :, None, :], 0.0)
        w = jnp.where(causal, q4[:, None, :, :] * k4[:, :, None, :] * jnp.exp(dlt), 0.0)
        z = _bdot(w.reshape(g * c * c, kd), jm).reshape(g, c, c, GROUP_WIDTH)
        o = jnp.sum(z * v.reshape(g, c, GROUP_WIDTH)[:, :, None, :], axis=1).reshape(r, GROUP_WIDTH)
        qd = q * jnp.exp(bt)
        kdn = k * jnp.exp(btot - bt)
        edec = jnp.exp(_hdot_tn(la, emask_ref[...]))
        s_cur = s_scr[...]
        inter = []
        for j in range(g):
            rs = slice(j * c, (j + 1) * c)
            inter.append(_bdot(qd[rs], s_cur))
            upd = jnp.where(jm > 0, _bdot_tn(kdn[rs], v[rs]), 0.0)
            s_cur = jnp.broadcast_to(edec[:, j:j + 1], (kd, GROUP_WIDTH)) * s_cur + upd
        s_scr[...] = s_cur
        o = o + jnp.concatenate(inter, axis=0)
        o_ref[rows, :] = _head_group_norm(o, jn_ref[...], gain_ref[...]) * (gate * jax.nn.sigmoid(gate))
        return carry

    lax.fori_loop(0, seq // r, step, 0)
    s_fin = s_scr[...]
    for h in range(N_HEADS):
        st_ref[0, h] = s_fin[h * GLA_DK:(h + 1) * GLA_DK, h * HEAD_DIM:(h + 1) * HEAD_DIM]


def _ret_prompt_body(x_ref, gain_ref, dstack_ref, qdec_ref, kdec_ref, sdec_ref, hm_ref, jm_ref, jn_ref,
                     o_ref, st_ref, s_scr):
    c = RET_CHUNK
    seq = x_ref.shape[0]
    s_scr[...] = jnp.zeros_like(s_scr)

    def step(i, carry):
        rows = pl.ds(pl.multiple_of(i * c, c), c)
        q = x_ref[rows, 0:256]
        k = x_ref[rows, 256:512] * (HEAD_DIM ** -0.5)
        v = x_ref[rows, 512:768]
        gate = x_ref[rows, 768:1024]
        hm = hm_ref[...]
        qs = jnp.concatenate([q] * N_HEADS, axis=0) * hm
        sc = _bdot_nt(qs, k) * dstack_ref[...]
        of = _bdot(sc, v) * hm
        o = of[0:c]
        for h in range(1, N_HEADS):
            o = o + of[h * c:(h + 1) * c]
        s_old = s_scr[...]
        o = o + _bdot(q * qdec_ref[...], s_old)
        upd = jnp.where(jm_ref[...] > 0, _bdot_tn(k * kdec_ref[...], v), 0.0)
        s_scr[...] = sdec_ref[...] * s_old + upd
        o_ref[rows, :] = _head_group_norm(o, jn_ref[...], gain_ref[...]) * (gate * jax.nn.sigmoid(gate))
        return carry

    lax.fori_loop(0, seq // c, step, 0)
    s_fin = s_scr[...]
    for h in range(N_HEADS):
        st_ref[0, h] = s_fin[h * HEAD_DIM:(h + 1) * HEAD_DIM, h * HEAD_DIM:(h + 1) * HEAD_DIM]


def _blockdiag(kd, dk):
    r = np.arange(kd)[:, None] // dk
    cidx = np.arange(GROUP_WIDTH)[None, :] // HEAD_DIM
    return (r == cidx).astype(np.float32)


def _lin_prompt_call(body, name, x, batch, seq, kd, dk, consts):
    full = lambda a: pl.BlockSpec(a.shape, lambda b: (0,) * a.ndim)
    return pl.pallas_call(
        body,
        grid=(batch,),
        in_specs=[pl.BlockSpec((seq, x.shape[1]), lambda b: (b, 0))] + [full(a) for a in consts],
        out_specs=[pl.BlockSpec((seq, GROUP_WIDTH), lambda b: (b, 0)),
                   pl.BlockSpec((1, N_HEADS, dk, HEAD_DIM), lambda b: (b, 0, 0, 0))],
        out_shape=[jax.ShapeDtypeStruct((batch * seq, GROUP_WIDTH), F32),
                   jax.ShapeDtypeStruct((batch, N_HEADS, dk, HEAD_DIM), F32)],
        scratch_shapes=[pltpu.VMEM((kd, GROUP_WIDTH), F32)],
        compiler_params=_cparams(("parallel",)),
        name=name,
    )(x, *consts)


def gla_prompt(x, batch, seq, wa, ba, gain):
    r, c = GLA_ROWS, GLA_CHUNK
    same = (np.arange(r)[:, None] // c) == (np.arange(r)[None, :] // c)
    tri = (same & (np.arange(r)[:, None] >= np.arange(r)[None, :])).astype(np.float32)
    emask = ((np.arange(r)[:, None] // c) == np.arange(LANES)[None, :]).astype(np.float32)
    consts = [wa, ba, gain, jnp.asarray(tri), jnp.asarray(same.astype(np.float32)), jnp.asarray(emask),
              jnp.asarray(_blockdiag(128, GLA_DK)), jnp.asarray(_blockdiag(GROUP_WIDTH, HEAD_DIM) / HEAD_DIM)]
    return _lin_prompt_call(_gla_prompt_body, "gla_prompt", x, batch, seq, 128, GLA_DK, consts)


def ret_prompt(x, batch, seq, gain):
    c = RET_CHUNK
    log_gamma = np.log1p(-np.power(2.0, -5.0 - np.arange(N_HEADS, dtype=np.float64)))
    t = np.arange(c)
    diff = t[:, None] - t[None, :]
    dstack = np.concatenate([np.where(diff >= 0, np.exp(lg * np.maximum(diff, 0)), 0.0) for lg in log_gamma], axis=0)
    per_lane = np.repeat(log_gamma, HEAD_DIM)
    qdec = np.exp(per_lane[None, :] * (t[:, None] + 1))
    kdec = np.exp(per_lane[None, :] * (c - 1 - t[:, None]))
    sdec = np.broadcast_to(np.exp(per_lane * c)[:, None], (GROUP_WIDTH, GROUP_WIDTH))
    hm = np.concatenate([np.broadcast_to(np.arange(GROUP_WIDTH)[None, :] // HEAD_DIM == h, (c, GROUP_WIDTH))
                         for h in range(N_HEADS)], axis=0)
    as_f32 = lambda a: jnp.asarray(np.asarray(a, np.float32))
    consts = [gain, as_f32(dstack), as_f32(qdec), as_f32(kdec), as_f32(sdec), as_f32(hm),
              jnp.asarray(_blockdiag(GROUP_WIDTH, HEAD_DIM)),
              jnp.asarray(_blockdiag(GROUP_WIDTH, HEAD_DIM) / HEAD_DIM)]
    return _lin_prompt_call(_ret_prompt_body, "ret_prompt", x, batch, seq, GROUP_WIDTH, HEAD_DIM, consts)


def _lin_decode_body(x_ref, s0_ref, wa_ref, ba_ref, la_ref, gain_ref, o_ref, s1_ref, a_scr, o_scr,
                     *, kind, kd, dk):
    steps = x_ref.shape[0]
    if kind == "gla":
        qo, ko, vo, go = 0, 128, 256, 512
        for t in range(steps):
            pre = _hdot(wa_ref[...], x_ref[t, 768:896, :]) + ba_ref[...]
            a_scr[t] = jnp.exp(jax.nn.log_sigmoid(pre) / GLA_TAU)
        qscale, kscale = GLA_DK ** -0.5, 1.0
    else:
        qo, ko, vo, go = 0, 256, 512, 768
        for t in range(steps):
            a_scr[t] = jnp.exp(jnp.broadcast_to(la_ref[...], a_scr.shape[1:]))
        qscale, kscale = 1.0, HEAD_DIM ** -0.5
    o_scr[...] = jnp.zeros_like(o_scr)

    def body(j8, carry):
        r0 = pl.multiple_of(j8 * 8, 8)
        h = r0 // dk
        vrows = pl.ds(pl.multiple_of(vo + h * HEAD_DIM, HEAD_DIM), HEAD_DIM)
        orows = pl.ds(pl.multiple_of(h * HEAD_DIM, HEAD_DIM), HEAD_DIM)
        a8 = [a_scr[t, pl.ds(r0, 8), :] for t in range(steps)]
        q8 = [x_ref[t, pl.ds(qo + r0, 8), :] * qscale for t in range(steps)]
        k8 = [x_ref[t, pl.ds(ko + r0, 8), :] * kscale for t in range(steps)]
        for jj in range(8):
            sj = s0_ref[r0 + jj]
            for t in range(steps):
                vt = x_ref[t, vrows, :]
                sj = a8[t][jj:jj + 1, :] * sj + k8[t][jj:jj + 1, :] * vt
                o_scr[t, orows, :] += q8[t][jj:jj + 1, :] * sj
            s1_ref[r0 + jj] = sj
        return carry

    lax.fori_loop(0, kd // 8, body, 0)
    for t in range(steps):
        for h in range(N_HEADS):
            blk = slice(h * HEAD_DIM, (h + 1) * HEAD_DIM)
            o = o_scr[t, blk, :]
            mu = jnp.mean(o, axis=0, keepdims=True)
            d = o - mu
            var = jnp.mean(d * d, axis=0, keepdims=True)
            gate = x_ref[t, go + h * HEAD_DIM:go + (h + 1) * HEAD_DIM, :]
            o_ref[t, blk, :] = d * lax.rsqrt(var + LN_EPS) * gain_ref[blk, :] * (gate * jax.nn.sigmoid(gate))


def lin_decode(xt, s0, kind, wa_t, ba_col, la_col, gain_col):
    kd, dk = (128, GLA_DK) if kind == "gla" else (256, HEAD_DIM)
    steps, _, nb = xt.shape
    return pl.pallas_call(
        functools.partial(_lin_decode_body, kind=kind, kd=kd, dk=dk),
        out_shape=[jax.ShapeDtypeStruct((steps, GROUP_WIDTH, nb), F32),
                   jax.ShapeDtypeStruct((kd, HEAD_DIM, nb), F32)],
        scratch_shapes=[pltpu.VMEM((steps, kd, nb), F32), pltpu.VMEM((steps, GROUP_WIDTH, nb), F32)],
        compiler_params=pltpu.CompilerParams(vmem_limit_bytes=VMEM_LIMIT),
        name="lin_decode_" + kind,
    )(xt, s0, wa_t, ba_col, la_col, gain_col)


def _cmp_mlp(xs_list, pos_ref, w1_ref, w2_ref, n_rows):
    xcat = jnp.concatenate(
        [jnp.concatenate([(xs[pl.ds(r, n_rows, stride=CMP_STRIDE), :] + pos_ref[r:r + 1, :]).astype(BF16)
                          for r in range(CMP_LEN)], axis=1) for xs in xs_list], axis=0)
    hid = jnp.dot(xcat, w1_ref[...].reshape(CMP_LEN * LANES, 2 * CMP_HIDDEN), preferred_element_type=F32)
    return _bdot(jax.nn.gelu(hid), w2_ref[...])


def _rank_select(vals, idx, n_rows, n_top, axis):
    cnt = jnp.zeros(vals.shape, F32)
    for j in range(n_rows):
        vj = vals[j:j + 1, :] if axis == 0 else vals[:, j:j + 1]
        before = (vj > vals) | ((vj == vals) & (j < idx))
        cnt = cnt + before.astype(F32)
    return cnt < n_top


DECODE_SEQS = 4
TQ = 256
TK = 256


POS_HI_LANE = HEAD_DIM
POS_LO_LANE = HEAD_DIM + 1


def _nt(a, b):
    return lax.dot_general(a, b, (((1,), (1,)), ((), ())), preferred_element_type=F32)


def _tn(a, b):
    return lax.dot_general(a, b, (((0,), (0,)), ((), ())), preferred_element_type=F32)


def _key_tile(x128, pos):
    lane = lax.broadcasted_iota(jnp.int32, x128.shape, 1)
    feat = jnp.where(lane == POS_HI_LANE, (pos // 64).astype(F32),
                     jnp.where(lane == POS_LO_LANE, (pos % 64).astype(F32), 0.0))
    return jnp.where(lane < HEAD_DIM, x128, feat).astype(BF16)


def _query_stack(q256, slopes):
    tq = q256.shape[0]
    lane = lax.broadcasted_iota(jnp.int32, (tq, LANES), 1)
    parts = []
    for h in range(N_HEADS):
        pair = q256[:, (h // 2) * LANES:(h // 2 + 1) * LANES]
        if h % 2:
            pair = pltpu.roll(pair, HEAD_DIM, 1)
        const = jnp.where(lane == POS_HI_LANE, 64.0 * slopes[h], jnp.where(lane == POS_LO_LANE, slopes[h], 0.0))
        parts.append(jnp.where(lane < HEAD_DIM, pair * (HEAD_DIM ** -0.5), const).astype(BF16))
    return jnp.concatenate(parts, axis=0)


def _tile_step(k_fn, v_fn, qst_scr, bias, m_scr, l_scr, acc_scr, first, tq, group):
    starts = list(range(0, N_HEADS * tq, group))
    scores = []
    for c0 in starts:
        s = _nt(k_fn(c0 // tq), qst_scr[c0:c0 + group, :])
        scores.append(s if bias is None else s + bias[:, c0:c0 + group])
    probs, stats = [], []
    for c0, s in zip(starts, scores):
        cols = slice(c0, c0 + group)
        if first:
            m_new = jnp.max(s, axis=0, keepdims=True)
            p = jnp.exp(s - m_new)
            stats.append((m_new, jnp.sum(p, axis=0, keepdims=True), None))
        else:
            m_old = m_scr[:, cols]
            m_new = jnp.maximum(m_old, jnp.max(s, axis=0, keepdims=True))
            p = jnp.exp(s - m_new)
            corr = jnp.exp(m_old - m_new)
            stats.append((m_new, corr * l_scr[:, cols] + jnp.sum(p, axis=0, keepdims=True), corr))
        probs.append(p.astype(BF16))
    for c0, p, (m_new, l_new, corr) in zip(starts, probs, stats):
        cols = slice(c0, c0 + group)
        pv = _tn(v_fn(c0 // tq), p)
        acc_scr[:, cols] = pv if first else corr * acc_scr[:, cols] + pv
        m_scr[:, cols] = m_new
        l_scr[:, cols] = l_new


def _tile4(x):
    return jnp.concatenate([x] * N_HEADS, axis=1)


def _tri_bias(tq, keep_lower):
    row = lax.broadcasted_iota(jnp.int32, (TK, tq), 0)
    col = lax.broadcasted_iota(jnp.int32, (TK, tq), 1)
    return _tile4(jnp.where((row <= col) if keep_lower else (row > col), 0.0, NEG))


def _nsa_prompt_body(q_ref, gate_ref, kv_ref, win_ref, w1_ref, pos_ref, w2_ref, imp_ref, o_ref,
                     xs_scr, kc_scr, vc_scr, ks_scr, vs_scr, kw_scr, vw_scr, selb_scr,
                     qst_scr, ocmp_scr, m_scr, l_scr, acc_scr, *, seq):
    qi = pl.program_id(1)
    tq = q_ref.shape[0]
    ncp = seq // CMP_STRIDE
    n_cmp = ncp - 1
    n_sel = seq // SEL_BLOCK
    blocks_per_tile = TK // SEL_BLOCK

    @pl.when(qi == 0)
    def _():
        xs_scr[0:seq, :] = kv_ref[:, 0:128]
        xs_scr[seq:seq + CMP_STRIDE, :] = jnp.zeros((CMP_STRIDE, LANES), F32)
        kvc = _cmp_mlp([xs_scr], pos_ref, w1_ref, w2_ref, ncp)
        cend = lax.broadcasted_iota(jnp.int32, (ncp, LANES), 0) * CMP_STRIDE + (CMP_LEN - 1)
        kc_scr[...] = _key_tile(kvc, cend)
        vc_scr[...] = kvc[:, HEAD_DIM:2 * HEAD_DIM].astype(BF16)
        kpos = lax.broadcasted_iota(jnp.int32, (seq, LANES), 0)
        ks_scr[...] = _key_tile(kv_ref[:, 128:256], kpos)
        vs_scr[...] = kv_ref[:, 192:256].astype(BF16)
        kw_scr[...] = _key_tile(win_ref[:, 0:128], kpos)
        vw_scr[...] = win_ref[:, 64:128].astype(BF16)

    qpos = qi * tq + lax.broadcasted_iota(jnp.int32, (1, tq), 1)
    gates_t = jax.nn.sigmoid(gate_ref[:, 128:256]).T
    qst_scr[...] = _query_stack(q_ref[...], NSA_SLOPES)

    n_io = lax.broadcasted_iota(jnp.int32, (ncp, tq), 0)
    cmask = _tile4(((n_io * CMP_STRIDE + (CMP_LEN - 1)) <= qpos) & (n_io < n_cmp))
    s = jnp.where(cmask, _nt(kc_scr[...], qst_scr[...]), NEG)
    e = jnp.where(cmask, jnp.exp(s - jnp.max(s, axis=0, keepdims=True)), 0.0)
    pc = e / jnp.maximum(jnp.sum(e, axis=0, keepdims=True), TINY)
    ocmp_scr[...] = _tn(vc_scr[...], pc.astype(BF16))
    pc_sum = pc[:, 0:tq]
    for h in range(1, N_HEADS):
        pc_sum = pc_sum + pc[:, h * tq:(h + 1) * tq]

    imp = _xdot_left(imp_ref[...], pc_sum)
    blk = lax.broadcasted_iota(jnp.int32, (n_sel, tq), 0)
    cur = qpos // SEL_BLOCK
    forced = (blk == 0) | (blk == cur) | (blk == cur - 1)
    vals = jnp.where(blk <= cur, jnp.where(forced, BIG, imp), NEG)
    chosen = _rank_select(vals, blk, n_sel, min(N_SEL, n_sel), 0)
    selb_scr[...] = jnp.where(chosen, 0.0, NEG)

    def sel_bias(kt):
        rows = [jnp.broadcast_to(selb_scr[pl.ds(kt * blocks_per_tile + i, 1), :], (SEL_BLOCK, tq))
                for i in range(blocks_per_tile)]
        return _tile4(jnp.concatenate(rows, axis=0))

    def tile_rows(kt):
        return pl.ds(pl.multiple_of(kt * TK, TK), TK)

    def step(k_scr, v_scr, rows, bias, first):
        _tile_step(lambda h: k_scr[rows, :], lambda h: v_scr[rows, :], qst_scr, bias,
                   m_scr, l_scr, acc_scr, first, tq, 2 * tq)

    diag = tile_rows(qi)
    step(ks_scr, vs_scr, diag, sel_bias(qi) + _tri_bias(tq, True), True)

    def sel_body(kt, carry):
        step(ks_scr, vs_scr, tile_rows(kt), sel_bias(kt), False)
        return carry

    lax.fori_loop(0, qi, sel_body, 0)
    gate_row = lambda j: jnp.concatenate([gates_t[3 * h + j:3 * h + j + 1, :] for h in range(N_HEADS)], axis=1)
    out = gate_row(0) * ocmp_scr[...] + gate_row(1) * (acc_scr[...] / l_scr[...])

    step(kw_scr, vw_scr, diag, _tri_bias(tq, True), True)

    @pl.when(qi >= 1)
    def _():
        step(kw_scr, vw_scr, tile_rows(qi - 1), None, False)

    @pl.when(qi >= WINDOW // TK)
    def _():
        step(kw_scr, vw_scr, tile_rows(qi - WINDOW // TK), _tri_bias(tq, False), False)

    out = out + gate_row(2) * (acc_scr[...] / l_scr[...])
    o_ref[...] = jnp.concatenate([out[:, h * tq:(h + 1) * tq] for h in range(N_HEADS)], axis=0).T


def _cmp_to_sel(n_cmp, n_sel):
    ratio, span = SEL_BLOCK // CMP_STRIDE, CMP_LEN // CMP_STRIDE
    j, m, n = np.meshgrid(np.arange(n_sel), np.arange(ratio), np.arange(span), indexing="ij")
    i = ratio * j + m - n
    ok = (i >= 0) & (i < n_cmp)
    mat = np.zeros((n_cmp, n_sel), np.float32)
    np.add.at(mat, (i[ok], j[ok]), 1.0)
    return mat


def nsa_prompt(xq, xkv, xwg, batch, seq, w1, pos, w2):
    nq = seq // TQ
    ncp = seq // CMP_STRIDE
    n_sel = seq // SEL_BLOCK
    imp_t = np.zeros((n_sel, ncp), np.float32)
    imp_t[:, :ncp - 1] = _cmp_to_sel(ncp - 1, n_sel).T
    full = lambda *shape: pl.BlockSpec(shape, lambda b, i: (0,) * len(shape))
    return pl.pallas_call(
        functools.partial(_nsa_prompt_body, seq=seq),
        grid=(batch, nq),
        in_specs=[pl.BlockSpec((TQ, GROUP_WIDTH), lambda b, i: (b * nq + i, 0)),
                  pl.BlockSpec((TQ, GROUP_WIDTH), lambda b, i: (b * nq + i, 0)),
                  pl.BlockSpec((seq, GROUP_WIDTH), lambda b, i: (b, 0)),
                  pl.BlockSpec((seq, GROUP_WIDTH), lambda b, i: (b, 0)),
                  full(CMP_LEN, LANES, 2 * CMP_HIDDEN), full(CMP_LEN, LANES),
                  full(2 * CMP_HIDDEN, LANES), full(n_sel, ncp)],
        out_specs=pl.BlockSpec((TQ, GROUP_WIDTH), lambda b, i: (b * nq + i, 0)),
        out_shape=jax.ShapeDtypeStruct((batch * seq, GROUP_WIDTH), F32),
        scratch_shapes=[pltpu.VMEM((seq + CMP_STRIDE, LANES), F32),
                        pltpu.VMEM((ncp, LANES), BF16), pltpu.VMEM((ncp, HEAD_DIM), BF16),
                        pltpu.VMEM((seq, LANES), BF16), pltpu.VMEM((seq, HEAD_DIM), BF16),
                        pltpu.VMEM((seq, LANES), BF16), pltpu.VMEM((seq, HEAD_DIM), BF16),
                        pltpu.VMEM((n_sel, TQ), F32),
                        pltpu.VMEM((N_HEADS * TQ, LANES), BF16), pltpu.VMEM((HEAD_DIM, N_HEADS * TQ), F32),
                        pltpu.VMEM((1, N_HEADS * TQ), F32), pltpu.VMEM((1, N_HEADS * TQ), F32),
                        pltpu.VMEM((HEAD_DIM, N_HEADS * TQ), F32)],
        compiler_params=_cparams(("parallel", "arbitrary")),
        name="nsa_prompt",
    )(xq, xwg, xkv, xwg, w1, pos, w2, jnp.asarray(imp_t))


def _moba_prompt_body(q_ref, kv_ref, o_ref, km_scr, ka_scr, va_scr, selb_scr, qst_scr,
                      m_scr, l_scr, acc_scr, *, seq):
    qi = pl.program_id(1)
    tq = q_ref.shape[0]
    nb = seq // MOBA_BLOCK

    @pl.when(qi == 0)
    def _():
        for n in range(nb):
            km_scr[n:n + 1, :] = jnp.mean(kv_ref[n * MOBA_BLOCK:(n + 1) * MOBA_BLOCK, 0:GROUP_WIDTH],
                                          axis=0, keepdims=True)

        kpos = lax.broadcasted_iota(jnp.int32, (seq, LANES), 0)
        for h in range(N_HEADS):
            pair = kv_ref[:, (h // 2) * LANES:(h // 2 + 1) * LANES]
            if h % 2:
                pair = pltpu.roll(pair, HEAD_DIM, 1)
            ka_scr[h] = _key_tile(pair, kpos)
            va_scr[h] = kv_ref[:, GROUP_WIDTH + h * HEAD_DIM:GROUP_WIDTH + (h + 1) * HEAD_DIM].astype(BF16)

    qpos = qi * tq + lax.broadcasted_iota(jnp.int32, (1, tq), 1)
    own = qpos // MOBA_BLOCK
    blk = lax.broadcasted_iota(jnp.int32, (nb, tq), 0)
    past = blk < own
    for h in range(N_HEADS):
        hs = slice(h * HEAD_DIM, (h + 1) * HEAD_DIM)
        gate = _hdot_nt(km_scr[:, hs], q_ref[:, hs])
        chosen = _rank_select(jnp.where(past, gate, NEG), blk, nb, min(MOBA_TOPK, nb), 0) & past
        selb_scr[h] = jnp.where(chosen, 0.0, NEG)

    qst_scr[...] = _query_stack(q_ref[...], MOBA_SLOPES)

    def step(rows, bias, first):
        _tile_step(lambda h: ka_scr[h, rows, :], lambda h: va_scr[h, rows, :], qst_scr, bias,
                   m_scr, l_scr, acc_scr, first, tq, tq)

    step(pl.ds(pl.multiple_of(qi * TK, TK), TK), _tri_bias(tq, True), True)

    def body(kt, carry):
        bias = jnp.concatenate([selb_scr[h, pl.ds(kt, 1), :] for h in range(N_HEADS)], axis=1)
        step(pl.ds(pl.multiple_of(kt * TK, TK), TK), bias, False)
        return carry

    lax.fori_loop(0, qi, body, 0)
    out = acc_scr[...] / l_scr[...]
    o_ref[...] = jnp.concatenate([out[:, h * tq:(h + 1) * tq] for h in range(N_HEADS)], axis=0).T


def moba_prompt(xq, xkv, batch, seq):
    nq = seq // TQ
    nb = seq // MOBA_BLOCK
    return pl.pallas_call(
        functools.partial(_moba_prompt_body, seq=seq),
        grid=(batch, nq),
        in_specs=[pl.BlockSpec((TQ, GROUP_WIDTH), lambda b, i: (b * nq + i, 0)),
                  pl.BlockSpec((seq, 2 * GROUP_WIDTH), lambda b, i: (b, 0))],
        out_specs=pl.BlockSpec((TQ, GROUP_WIDTH), lambda b, i: (b * nq + i, 0)),
        out_shape=jax.ShapeDtypeStruct((batch * seq, GROUP_WIDTH), F32),
        scratch_shapes=[pltpu.VMEM((nb, GROUP_WIDTH), F32),
                        pltpu.VMEM((N_HEADS, seq, LANES), BF16), pltpu.VMEM((N_HEADS, seq, HEAD_DIM), BF16),
                        pltpu.VMEM((N_HEADS, nb, TQ), F32),
                        pltpu.VMEM((N_HEADS * TQ, LANES), BF16),
                        pltpu.VMEM((1, N_HEADS * TQ), F32), pltpu.VMEM((1, N_HEADS * TQ), F32),
                        pltpu.VMEM((HEAD_DIM, N_HEADS * TQ), F32)],
        compiler_params=_cparams(("parallel", "arbitrary")),
        name="moba_prompt",
    )(xq, xkv)


def _softmax_rows(parts, masks):
    masked = [jnp.where(mk, s, NEG) for s, mk in zip(parts, masks)]
    m = masked[0].max(axis=1, keepdims=True)
    for s in masked[1:]:
        m = jnp.maximum(m, s.max(axis=1, keepdims=True))
    es = [jnp.where(mk, jnp.exp(s - m), 0.0) for s, mk in zip(masked, masks)]
    den = es[0].sum(axis=1, keepdims=True)
    for e in es[1:]:
        den = den + e.sum(axis=1, keepdims=True)
    inv = 1.0 / jnp.maximum(den, TINY)
    return [e * inv for e in es]


def _stack_heads(x, off):
    return jnp.concatenate([x[:, off + h * HEAD_DIM:off + (h + 1) * HEAD_DIM] for h in range(N_HEADS)], axis=0)


def _round_robin(gens):
    outs = [None] * len(gens)
    live = list(range(len(gens)))
    while live:
        for i in list(live):
            try:
                next(gens[i])
            except StopIteration as stop:
                outs[i] = stop.value
                live.remove(i)
    return outs


def _pad_rows(x, n):
    return jnp.concatenate([x, jnp.zeros((n - x.shape[0], x.shape[1]), x.dtype)], axis=0)


def _nsa_decode_body(*refs, n_pages, past_len, ns):
    x_refs = refs[1:4]
    all_pages = refs[4:4 + ns * n_pages]
    (win_ref, w1_ref, pos_ref, w2_ref, imp_ref, esel_ref, place_ref, _, o_ref, wout_ref,
     xs_scr) = refs[4 + ns * n_pages:]
    steps = x_refs[0].shape[0] // ns
    x_rows = lambda i: jnp.concatenate([r[i * steps:(i + 1) * steps, :] for r in x_refs], axis=1)
    ncp = past_len // CMP_STRIDE
    n_win = win_ref.shape[-1]
    lane = lax.broadcasted_iota(jnp.int32, (HEAD_DIM, n_win), 1)
    for i in range(ns):
        new_rows = x_refs[2][i * steps:(i + 1) * steps, 0:2 * HEAD_DIM]
        for c in range(2):
            placed = _hdot_tn(new_rows[:, c * HEAD_DIM:(c + 1) * HEAD_DIM], place_ref[...])
            shifted = pltpu.roll(win_ref[0, i, c], n_win - steps, 1)
            wout_ref[0, i, c] = jnp.where(lane < n_win - steps, shifted, placed)
    for i in range(ns):
        for p in range(n_pages):
            xs_scr[i, p * PAGE_SIZE:(p + 1) * PAGE_SIZE, :] = (
                all_pages[i * n_pages + p][0, 0, 0:2].reshape(LANES, PAGE_SIZE).T)
        xs_scr[i, past_len:past_len + CMP_STRIDE, :] = jnp.zeros((CMP_STRIDE, LANES), F32)
    kvc_all = _cmp_mlp([xs_scr.at[i] for i in range(ns)], pos_ref, w1_ref, w2_ref, ncp)
    outs = _round_robin([_nsa_decode_seq(
        x_rows(i), all_pages[i * n_pages:(i + 1) * n_pages], win_ref, i,
        kvc_all[i * ncp:(i + 1) * ncp], imp_ref, esel_ref, n_pages, past_len) for i in range(ns)])
    for i in range(ns):
        o_ref[i * steps:(i + 1) * steps, :] = outs[i]


def _nsa_decode_seq(x, page_refs, win_ref, wi, kvc, imp_ref, esel_ref, n_pages, past_len):
    steps = x.shape[0]
    rows = N_HEADS * steps
    ncp = past_len // CMP_STRIDE
    n_cmp = ncp - 1
    tpos = past_len + lax.broadcasted_iota(jnp.int32, (steps, 1), 0)
    qpos = jnp.concatenate([tpos] * N_HEADS, axis=0)
    slope = jnp.concatenate([jnp.full((steps, 1), s, F32) for s in NSA_SLOPES], axis=0)
    qs = _stack_heads(x, 0) * (HEAD_DIM ** -0.5)
    gates = jax.nn.sigmoid(x[:, 640:768])
    gcol = [jnp.concatenate([gates[:, 3 * h + j:3 * h + j + 1] for h in range(N_HEADS)], axis=0)
            for j in range(3)]
    new_io = lax.broadcasted_iota(jnp.int32, (rows, LANES), 1)
    new_pos = past_len + new_io
    new_ok = (new_io < steps) & (new_pos <= qpos)

    k_cmp, v_cmp = kvc[:, 0:HEAD_DIM], kvc[:, HEAD_DIM:2 * HEAD_DIM]
    n_io = lax.broadcasted_iota(jnp.int32, (rows, ncp), 1)
    cend = n_io * CMP_STRIDE + (CMP_LEN - 1)
    (pc,) = _softmax_rows([_bdot_nt(qs, k_cmp) + slope * cend.astype(F32)],
                          [(cend <= qpos) & (n_io < n_cmp)])
    o_cmp = _bdot(pc, v_cmp)
    yield
    pc_sum = pc[0:steps]
    for h in range(1, N_HEADS):
        pc_sum = pc_sum + pc[h * steps:(h + 1) * steps]

    n_sel = -(-(past_len + steps) // SEL_BLOCK)
    imp = _hdot(pc_sum, imp_ref[...])
    yield
    blk = lax.broadcasted_iota(jnp.int32, (steps, LANES), 1)
    cur = tpos // SEL_BLOCK
    forced = (blk == 0) | (blk == cur) | (blk == cur - 1)
    vals = jnp.where((blk <= cur) & (blk < n_sel), jnp.where(forced, BIG, imp), NEG)
    chosen = _rank_select(vals, blk, n_sel, min(N_SEL, n_sel), 1).astype(F32)
    key_sel = _bdot(chosen, esel_ref[...])
    key_sel = jnp.concatenate([key_sel] * N_HEADS, axis=0) > 0.5
    yield

    cat_pages = lambda c: jnp.concatenate([page_refs[p][0, 0, c].astype(BF16) for p in range(n_pages)], axis=1)
    s_past = _bdot(qs, cat_pages(2))
    kpos = lax.broadcasted_iota(jnp.int32, (rows, past_len), 1)
    k_new = _pad_rows(x[:, 384:448], LANES)
    v_new = _pad_rows(x[:, 448:512], LANES)
    s_new = _bdot_nt(qs, k_new) + slope * new_pos.astype(F32)
    yield
    p_past, p_new = _softmax_rows([s_past + slope * kpos.astype(F32), s_new],
                                  [key_sel[:, 0:past_len] & (kpos <= qpos),
                                   key_sel[:, past_len:past_len + LANES] & new_ok])
    o_sel = _bdot(p_new, v_new) + _bdot_nt(p_past, cat_pages(3))
    yield

    n_win = win_ref.shape[-1]
    wpos = (past_len - n_win) + lax.broadcasted_iota(jnp.int32, (rows, n_win), 1)
    kw_new = _pad_rows(x[:, 512:576], LANES)
    vw_new = _pad_rows(x[:, 576:640], LANES)
    s_w = _bdot(qs, win_ref[0, wi, 0]) + slope * wpos.astype(F32)
    s_wn = _bdot_nt(qs, kw_new) + slope * new_pos.astype(F32)
    yield
    dist = qpos - wpos
    p_w, p_wn = _softmax_rows([s_w, s_wn],
                              [(dist >= 0) & (dist < WINDOW), new_ok & (qpos - new_pos < WINDOW)])
    o_win = _bdot_nt(p_w, win_ref[0, wi, 1]) + _bdot(p_wn, vw_new)

    out = gcol[0] * o_cmp + gcol[1] * o_sel + gcol[2] * o_win
    return jnp.concatenate([out[h * steps:(h + 1) * steps] for h in range(N_HEADS)], axis=1)


def nsa_decode(xs, page_ids, cache_t, layer, win_t, win_next, w1, pos, w2, *, steps, past_len):
    nseq = xs[0].shape[0] // steps
    n_pages = past_len // PAGE_SIZE
    ncp = past_len // CMP_STRIDE
    n_sel = -(-(past_len + steps) // SEL_BLOCK)
    imp = np.zeros((ncp, LANES), np.float32)
    imp[:ncp - 1, :n_sel] = _cmp_to_sel(ncp - 1, n_sel)
    kblk = np.concatenate([np.arange(past_len) // SEL_BLOCK,
                           (past_len + np.arange(LANES)) // SEL_BLOCK])
    esel = (np.arange(LANES)[:, None] == kblk[None, :]).astype(np.float32)
    n_win = win_t.shape[-1]

    ns = math.gcd(nseq, DECODE_SEQS)

    def page_spec(j):
        return pl.BlockSpec((1, 1, 4, HEAD_DIM, PAGE_SIZE),
                            lambda b, pt, j=j: (pt[b * (ns * n_pages) + j], layer, 0, 0, 0))

    place = (np.arange(n_win)[None, :] == (n_win - steps + np.arange(steps))[:, None]).astype(np.float32)
    full = lambda *shape: pl.BlockSpec(shape, lambda b, pt: (0,) * len(shape))
    win_spec = pl.BlockSpec((1, ns, 2, HEAD_DIM, n_win), lambda b, pt: (layer, b, 0, 0, 0))
    inputs = (page_ids, *xs, *([cache_t] * (ns * n_pages)), win_t, w1, pos, w2, jnp.asarray(imp),
              jnp.asarray(esel), jnp.asarray(place), win_next)
    return pl.pallas_call(
        functools.partial(_nsa_decode_body, n_pages=n_pages, past_len=past_len, ns=ns),
        grid_spec=pltpu.PrefetchScalarGridSpec(
            num_scalar_prefetch=1,
            grid=(nseq // ns,),
            in_specs=[pl.BlockSpec((ns * steps, GROUP_WIDTH), lambda b, pt: (b, 0)) for _ in xs]
            + [page_spec(j) for j in range(ns * n_pages)]
            + [win_spec,
               full(CMP_LEN, LANES, 2 * CMP_HIDDEN), full(CMP_LEN, LANES),
               full(2 * CMP_HIDDEN, LANES), full(ncp, LANES), full(LANES, past_len + LANES),
               full(steps, n_win), pl.BlockSpec(memory_space=pl.ANY)],
            out_specs=[pl.BlockSpec((ns * steps, GROUP_WIDTH), lambda b, pt: (b, 0)), win_spec],
            scratch_shapes=[pltpu.VMEM((ns, past_len + CMP_STRIDE, LANES), F32)],
        ),
        out_shape=[jax.ShapeDtypeStruct((nseq * steps, GROUP_WIDTH), F32),
                   jax.ShapeDtypeStruct(win_next.shape, F32)],
        input_output_aliases={len(inputs) - 1: 1},
        compiler_params=_cparams(("parallel",)),
        name="nsa_decode",
    )(*inputs)


def _moba_decode_body(*refs, n_pages, past_len, ns):
    x_refs = refs[1:3]
    all_pages = refs[3:3 + ns * n_pages]
    emean_ref, eblk_ref, o_ref = refs[3 + ns * n_pages:]
    steps = x_refs[0].shape[0] // ns
    x_rows = lambda i: jnp.concatenate([r[i * steps:(i + 1) * steps, :] for r in x_refs], axis=1)
    outs = _round_robin([_moba_decode_seq(
        x_rows(i), all_pages[i * n_pages:(i + 1) * n_pages],
        emean_ref, eblk_ref, n_pages, past_len) for i in range(ns)])
    for i in range(ns):
        o_ref[i * steps:(i + 1) * steps, :] = outs[i]


def _moba_decode_seq(x, page_refs, emean_ref, eblk_ref, n_pages, past_len):
    steps = x.shape[0]
    rows = N_HEADS * steps
    tpos = past_len + lax.broadcasted_iota(jnp.int32, (steps, 1), 0)
    qpos = jnp.concatenate([tpos] * N_HEADS, axis=0)
    own = qpos // MOBA_BLOCK
    slope = jnp.concatenate([jnp.full((steps, 1), s, F32) for s in MOBA_SLOPES], axis=0)
    head_of_row = lax.broadcasted_iota(jnp.int32, (rows, GROUP_WIDTH), 0) // steps
    head_of_col = lax.broadcasted_iota(jnp.int32, (rows, GROUP_WIDTH), 1) // HEAD_DIM
    diag = head_of_row == head_of_col
    q_bd = jnp.where(diag, jnp.concatenate([x[:, 0:GROUP_WIDTH]] * N_HEADS, axis=0), 0.0)

    cat_pages = lambda c: jnp.concatenate(
        [page_refs[p][0, 0, c].reshape(GROUP_WIDTH, PAGE_SIZE).astype(BF16) for p in range(n_pages)], axis=1)
    kt_all = cat_pages(0)
    kmean_t = jnp.dot(kt_all, emean_ref[...], preferred_element_type=F32)
    yield
    gate = _hdot(q_bd, kmean_t)
    yield
    nb = -(-(past_len + steps) // MOBA_BLOCK)
    blk = lax.broadcasted_iota(jnp.int32, (rows, LANES), 1)
    past = (blk < own) & (blk < nb)
    vals = jnp.where(past, gate, NEG)
    chosen = (_rank_select(vals, blk, nb, min(MOBA_TOPK, nb), 1) & past).astype(F32)
    key_sel = _bdot(chosen, eblk_ref[...]) > 0.5
    yield

    q_sc = q_bd * (HEAD_DIM ** -0.5)
    s_past = _bdot(q_sc, kt_all)
    kpos = lax.broadcasted_iota(jnp.int32, (rows, past_len), 1)
    new_io = lax.broadcasted_iota(jnp.int32, (rows, LANES), 1)
    new_pos = past_len + new_io
    k_new = _pad_rows(x[:, 256:512], LANES)
    v_new = _pad_rows(x[:, 512:768], LANES)
    s_new = _bdot_nt(q_sc, k_new) + slope * new_pos.astype(F32)
    yield
    in_own_past = (kpos // MOBA_BLOCK) == own
    in_own_new = (new_pos // MOBA_BLOCK) == own
    p_past, p_new = _softmax_rows(
        [s_past + slope * kpos.astype(F32), s_new],
        [key_sel | in_own_past, (new_io < steps) & (new_pos <= qpos) & in_own_new])
    o_all = _bdot(p_new, v_new) + _bdot_nt(p_past, cat_pages(1))
    o_all = jnp.where(diag, o_all, 0.0)
    out = o_all[0:steps]
    for h in range(1, N_HEADS):
        out = out + o_all[h * steps:(h + 1) * steps]
    return out


def moba_decode(xs, page_ids, cache_t, layer, *, steps, past_len):
    nseq = xs[0].shape[0] // steps
    n_pages = past_len // PAGE_SIZE
    pages_per_blk = MOBA_BLOCK // PAGE_SIZE
    emean = np.zeros((n_pages, PAGE_SIZE, LANES), np.float32)
    for p in range(n_pages):
        emean[p, :, p // pages_per_blk] = 1.0 / MOBA_BLOCK
    eblk = (np.arange(LANES)[:, None] == (np.arange(past_len) // MOBA_BLOCK)[None, :]).astype(np.float32)

    ns = math.gcd(nseq, DECODE_SEQS)

    def page_spec(j):
        return pl.BlockSpec((1, 1, 2, N_HEADS, HEAD_DIM, PAGE_SIZE),
                            lambda b, pt, j=j: (pt[b * (ns * n_pages) + j], layer, 0, 0, 0, 0))

    full = lambda *shape: pl.BlockSpec(shape, lambda b, pt: (0,) * len(shape))
    return pl.pallas_call(
        functools.partial(_moba_decode_body, n_pages=n_pages, past_len=past_len, ns=ns),
        grid_spec=pltpu.PrefetchScalarGridSpec(
            num_scalar_prefetch=1,
            grid=(nseq // ns,),
            in_specs=[pl.BlockSpec((ns * steps, a.shape[1]), lambda b, pt: (b, 0)) for a in xs]
            + [page_spec(j) for j in range(ns * n_pages)]
            + [full(past_len, LANES), full(LANES, past_len)],
            out_specs=pl.BlockSpec((ns * steps, GROUP_WIDTH), lambda b, pt: (b, 0)),
        ),
        out_shape=jax.ShapeDtypeStruct((nseq * steps, GROUP_WIDTH), F32),
        compiler_params=_cparams(("parallel",)),
        name="moba_decode",
    )(page_ids, *xs, *([cache_t] * (ns * n_pages)), jnp.asarray(emean.reshape(past_len, LANES), BF16),
      jnp.asarray(eblk))


def _regroup_w_in(w):
    z = lambda n: jnp.zeros(w.shape[:2] + (n,), w.dtype)
    col = lambda a, b: w[:, :, a:b]
    gq, gk, gv, ga, gr = col(0, 128), col(128, 256), col(256, 512), col(512, 528), col(528, 784)
    ret = col(784, 1808)
    nq, nkv, ng = col(1808, 2064), col(2064, 2448), col(2448, 2460)
    moba = col(2460, 3228)
    return jnp.concatenate([gq, gk, gv, gr, ga, z(112), ret, nq, nkv, ng, z(116), moba], axis=2).astype(BF16)


def _cmp_weights(w1, pos, w2):
    w1r = w1.reshape(2, CMP_LEN, HEAD_DIM, CMP_HIDDEN)
    zero = jnp.zeros((CMP_LEN, HEAD_DIM, CMP_HIDDEN), w1.dtype)
    w1bd = jnp.concatenate([jnp.concatenate([w1r[0], zero], axis=2),
                            jnp.concatenate([zero, w1r[1]], axis=2)], axis=1)
    zero2 = jnp.zeros((CMP_HIDDEN, HEAD_DIM), w2.dtype)
    w2bd = jnp.concatenate([jnp.concatenate([w2[0], zero2], axis=1),
                            jnp.concatenate([zero2, w2[1]], axis=1)], axis=0)
    posf = jnp.concatenate([pos[0], pos[1]], axis=1)
    return w1bd.astype(BF16), posf, w2bd.astype(BF16)


def kernel(x_prompt, x_sample, cache_nsa, cache_moba, cache_nsa_win, state_gla, state_ret, page_table, w_in, gla_w_a2, gla_b_a, gla_norm, ret_norm, nsa_cmp_w1, nsa_cmp_w2, nsa_cmp_pos, w_out, ffn_w_up, ffn_w_down, ln_g, ln_b):
    bp, sp, d = x_prompt.shape
    bs, ss, _ = x_sample.shape
    past_len = page_table.shape[1] * PAGE_SIZE
    assert cache_nsa_win.shape[2] >= ss, "the sample group's window buffer must hold at least the new rows"

    nsa_t = jnp.transpose(cache_nsa, (0, 1, 3, 4, 2))
    moba_t = jnp.transpose(cache_moba, (0, 1, 3, 4, 5, 2))
    win_t = jnp.transpose(cache_nsa_win, (0, 1, 3, 4, 2))
    gla_t = jnp.transpose(state_gla, (0, 2, 3, 4, 1)).reshape(DEPTH, N_HEADS * GLA_DK, HEAD_DIM, bs)
    ret_t = jnp.transpose(state_ret, (0, 2, 3, 4, 1)).reshape(DEPTH, N_HEADS * HEAD_DIM, HEAD_DIM, bs)
    page_ids = page_table.reshape(-1)

    log_gamma = np.log1p(-np.power(2.0, -5.0 - np.arange(N_HEADS, dtype=np.float64))).astype(np.float32)
    la_ret = jnp.asarray(np.repeat(log_gamma, HEAD_DIM))
    win_next = jnp.zeros(win_t.shape, F32)

    xp = x_prompt.reshape(bp * sp, d)
    xs = x_sample.reshape(bs * ss, d)
    outs = {k: [] for k in ("nsa_p", "moba_p", "win_p", "gla_p", "ret_p", "nsa_s", "moba_s", "gla_s", "ret_s")}
    wu = ffn_w_up.astype(BF16)
    wd = ffn_w_down.astype(BF16)
    wi = _regroup_w_in(w_in)
    wo = w_out.astype(BF16)
    for l in range(DEPTH):
        g = ln_g[l].reshape(3, 1, d)
        b = ln_b[l].reshape(3, 1, d)
        wa = jnp.zeros((128, 128), F32).at[0:GLA_RANK, :].set(gla_w_a2[l])
        ba = gla_b_a[l].reshape(1, 128)
        gn = gla_norm[l].reshape(1, GROUP_WIDTH)
        rn = ret_norm[l].reshape(1, GROUP_WIDTH)
        w1bd, posf, w2bd = _cmp_weights(nsa_cmp_w1[l], nsa_cmp_pos[l], nsa_cmp_w2[l])

        xp = ffn_ln(xp, wu, wd, l, g[0], b[0])
        pg, pr, nq, nkv, nwg, mq, mkv = proj(xp, wi, l)
        o_gla, st_gla = gla_prompt(pg, bp, sp, wa, ba, gn)
        o_ret, st_ret = ret_prompt(pr, bp, sp, rn)
        o_nsa = nsa_prompt(nq, nkv, nwg, bp, sp, w1bd, posf, w2bd)
        o_moba = moba_prompt(mq, mkv, bp, sp)
        xp = out_ffn_ln(xp, (o_gla, o_ret, o_nsa, o_moba), wo, wu, wd, l, g[1], b[1], g[2], b[2])
        outs["nsa_p"].append(nkv.reshape(bp, sp, 4, HEAD_DIM))
        outs["moba_p"].append(mkv.reshape(bp, sp, 2, N_HEADS, HEAD_DIM))
        keep = min(WINDOW, sp)
        outs["win_p"].append(nwg.reshape(bp, sp, GROUP_WIDTH)[:, sp - keep:, 0:128].reshape(bp, keep, 2, HEAD_DIM))
        outs["gla_p"].append(st_gla)
        outs["ret_p"].append(st_ret)

        xs = ffn_ln(xs, wu, wd, l, g[0], b[0])
        pg, pr, nq, nkv, nwg, mq, mkv = proj(xs, wi, l)
        to_lanes = lambda a: jnp.transpose(a.reshape(bs, ss, a.shape[1]), (1, 2, 0))
        og_t, sg_t = lin_decode(to_lanes(pg), gla_t[l], "gla", wa.T,
                                ba.reshape(128, 1), jnp.zeros((128, 1), F32), gn.reshape(GROUP_WIDTH, 1))
        or_t, sr_t = lin_decode(to_lanes(pr), ret_t[l], "ret", jnp.zeros((128, 128), F32),
                                jnp.zeros((128, 1), F32), la_ret.reshape(256, 1), rn.reshape(GROUP_WIDTH, 1))
        from_lanes = lambda a: jnp.transpose(a, (2, 0, 1)).reshape(bs * ss, GROUP_WIDTH)
        o_nsa, win_next = nsa_decode((nq, nkv, nwg), page_ids, nsa_t, l, win_t, win_next, w1bd, posf, w2bd,
                                     steps=ss, past_len=past_len)
        o_moba = moba_decode((mq, mkv), page_ids, moba_t, l, steps=ss, past_len=past_len)
        xs = out_ffn_ln(xs, (from_lanes(og_t), from_lanes(or_t), o_nsa, o_moba), wo, wu, wd, l,
                        g[1], b[1], g[2], b[2])
        outs["nsa_s"].append(nkv.reshape(bs, ss, 4, HEAD_DIM))
        outs["moba_s"].append(mkv.reshape(bs, ss, 2, N_HEADS, HEAD_DIM))
        outs["gla_s"].append(jnp.transpose(sg_t.reshape(N_HEADS, GLA_DK, HEAD_DIM, bs), (3, 0, 1, 2)))
        outs["ret_s"].append(jnp.transpose(sr_t.reshape(N_HEADS, HEAD_DIM, HEAD_DIM, bs), (3, 0, 1, 2)))

    win_s = jnp.transpose(win_next, (0, 1, 4, 2, 3))
    return (xp.reshape(bp, sp, d), xs.reshape(bs, ss, d),
            jnp.stack(outs["nsa_p"], axis=1), jnp.stack(outs["moba_p"], axis=1),
            jnp.stack(outs["win_p"], axis=0), jnp.stack(outs["gla_p"], axis=0), jnp.stack(outs["ret_p"], axis=0),
            jnp.stack(outs["nsa_s"], axis=1), jnp.stack(outs["moba_s"], axis=1),
            win_s, jnp.stack(outs["gla_s"], axis=0), jnp.stack(outs["ret_s"], axis=0))
```

```python
import functools
import math

import numpy as np
import jax
import jax.numpy as jnp
from jax import lax
from jax.experimental import pallas as pl
from jax.experimental.pallas import tpu as pltpu

F32 = jnp.float32
BF16 = jnp.bfloat16
HI = lax.Precision.HIGHEST

D_MODEL = 1024
DEPTH = 4
PAGE_SIZE = 128
HEAD_DIM = 64
N_HEADS = 4
GROUP_WIDTH = 256
GLA_DK = 32
GLA_RANK = 16
GLA_TAU = 16.0
CMP_LEN = 32
CMP_STRIDE = 16
CMP_HIDDEN = 128
SEL_BLOCK = 64
N_SEL = 8
WINDOW = 512
MOBA_BLOCK = 256
MOBA_TOPK = 3
D_FF = 2816
ALPHA = (2 * DEPTH) ** 0.25
LN_EPS = 1e-5
NEG = -1e30
BIG = 1e30
TINY = 1e-30

LANES = 128
VMEM_LIMIT = 56 * 1024 * 1024

GLA_W = 896
RET_W = 1024
NSA_W = 768
MOBA_W = 768
PROJ_SPLITS = (GLA_W, RET_W, 256, 256, 256, 256, 512)


def _slopes():
    n = 2 * N_HEADS
    s = [2.0 ** (-8.0 * i / n) for i in range(1, n + 1)]
    return s[0::2], s[1::2]


NSA_SLOPES, MOBA_SLOPES = _slopes()


def _bdot(a, b):
    return jnp.dot(a.astype(BF16), b.astype(BF16), preferred_element_type=F32)


def _bdot_nt(a, b):
    return lax.dot_general(a.astype(BF16), b.astype(BF16), (((1,), (1,)), ((), ())),
                           preferred_element_type=F32)


def _bdot_tn(a, b):
    return lax.dot_general(a.astype(BF16), b.astype(BF16), (((0,), (0,)), ((), ())),
                           preferred_element_type=F32)


def _hdot(a, b):
    return jnp.dot(a, b, precision=HI, preferred_element_type=F32)


def _hdot_nt(a, b):
    return lax.dot_general(a, b, (((1,), (1,)), ((), ())), precision=HI, preferred_element_type=F32)


def _hdot_tn(a, b):
    return lax.dot_general(a, b, (((0,), (0,)), ((), ())), precision=HI, preferred_element_type=F32)


def _split3(a):
    hi = a.astype(BF16)
    r1 = a - hi.astype(F32)
    mid = r1.astype(BF16)
    lo = (r1 - mid.astype(F32)).astype(BF16)
    return hi, mid, lo


def _xdot(a, c, dims=(((1,), (0,)), ((), ()))):
    cb = c.astype(BF16)
    return sum(lax.dot_general(part, cb, dims, preferred_element_type=F32) for part in _split3(a))


def _xdot_left(c, a):
    cb = c.astype(BF16)
    return sum(jnp.dot(cb, part, preferred_element_type=F32) for part in _split3(a))


def _layer_norm(y, g, b):
    mu = jnp.mean(y, axis=-1, keepdims=True)
    d = y - mu
    var = jnp.mean(d * d, axis=-1, keepdims=True)
    return d * lax.rsqrt(var + LN_EPS) * g + b


def _cparams(sem):
    return pltpu.CompilerParams(dimension_semantics=sem, vmem_limit_bytes=VMEM_LIMIT)


MXU_TILE = 256


def _swiglu(xb, wup_ref, wdn_ref):
    f = wdn_ref.shape[0]
    cut = min(f, (f // 2 // MXU_TILE + 1) * MXU_TILE)
    out = None
    for a0, a1 in ((0, cut), (cut, f)):
        if a1 == a0:
            continue
        u = jnp.dot(xb, wup_ref[:, a0:a1], preferred_element_type=F32)
        gt = jnp.dot(xb, wup_ref[:, f + a0:f + a1], preferred_element_type=F32)
        a = (gt * jax.nn.sigmoid(gt) * u).astype(BF16)
        part = jnp.dot(a, wdn_ref[a0:a1, :], preferred_element_type=F32)
        out = part if out is None else out + part
    return out


def _ffn_ln_body(x_ref, wup_ref, wdn_ref, g_ref, b_ref, o_ref):
    x = x_ref[...]
    y = ALPHA * x + 0.5 * _swiglu(x.astype(BF16), wup_ref, wdn_ref)
    o_ref[...] = _layer_norm(y, g_ref[...], b_ref[...])


def ffn_ln(x, w_up, w_down, layer, g, b, *, tm=512):
    m, d = x.shape
    f = w_down.shape[2]
    tm = min(tm, m)
    return pl.pallas_call(
        _ffn_ln_body,
        grid=(m // tm,),
        in_specs=[
            pl.BlockSpec((tm, d), lambda i: (i, 0)),
            pl.BlockSpec((None, None, d, 2 * f), lambda i: (layer, 0, 0, 0)),
            pl.BlockSpec((None, None, f, d), lambda i: (layer, 0, 0, 0)),
            pl.BlockSpec((1, d), lambda i: (0, 0)),
            pl.BlockSpec((1, d), lambda i: (0, 0)),
        ],
        out_specs=pl.BlockSpec((tm, d), lambda i: (i, 0)),
        out_shape=jax.ShapeDtypeStruct((m, d), F32),
        compiler_params=_cparams(("parallel",)),
        name="ffn_ln",
    )(x, w_up, w_down, g, b)


def _proj_body(x_ref, w_ref, *o_refs):
    xb = x_ref[...].astype(BF16)
    off = 0
    for o_ref in o_refs:
        wdt = o_ref.shape[1]
        o_ref[...] = jnp.dot(xb, w_ref[:, off:off + wdt], preferred_element_type=F32)
        off += wdt


def proj(x, w, layer, *, tm=512):
    m, d = x.shape
    n = w.shape[2]
    tm = min(tm, m)
    return pl.pallas_call(
        _proj_body,
        grid=(m // tm,),
        in_specs=[pl.BlockSpec((tm, d), lambda i: (i, 0)),
                  pl.BlockSpec((None, d, n), lambda i: (layer, 0, 0))],
        out_specs=[pl.BlockSpec((tm, wdt), lambda i: (i, 0)) for wdt in PROJ_SPLITS],
        out_shape=[jax.ShapeDtypeStruct((m, wdt), F32) for wdt in PROJ_SPLITS],
        compiler_params=_cparams(("parallel",)),
        name="proj",
    )(x, w)


def _out_ffn_ln_body(x_ref, o0_ref, o1_ref, o2_ref, o3_ref, wo_ref, g1_ref, b1_ref,
                     wup_ref, wdn_ref, g2_ref, b2_ref, y_ref):
    mix = None
    for gi, o_ref in enumerate((o0_ref, o1_ref, o2_ref, o3_ref)):
        part = jnp.dot(o_ref[...].astype(BF16), wo_ref[gi * GROUP_WIDTH:(gi + 1) * GROUP_WIDTH, :],
                       preferred_element_type=F32)
        mix = part if mix is None else mix + part
    mid = _layer_norm(ALPHA * x_ref[...] + mix, g1_ref[...], b1_ref[...])
    y = ALPHA * mid + 0.5 * _swiglu(mid.astype(BF16), wup_ref, wdn_ref)
    y_ref[...] = _layer_norm(y, g2_ref[...], b2_ref[...])


def out_ffn_ln(x, outs, w_out, w_up, w_down, layer, g1, b1, g2, b2, *, tm=512):
    m, d = x.shape
    f = w_down.shape[2]
    tm = min(tm, m)
    row = lambda i: (i, 0)
    vec = pl.BlockSpec((1, d), lambda i: (0, 0))
    return pl.pallas_call(
        _out_ffn_ln_body,
        grid=(m // tm,),
        in_specs=[pl.BlockSpec((tm, d), row)]
        + [pl.BlockSpec((tm, GROUP_WIDTH), row) for _ in range(4)]
        + [pl.BlockSpec((None, d, d), lambda i: (layer, 0, 0)), vec, vec,
           pl.BlockSpec((None, None, d, 2 * f), lambda i: (layer, 1, 0, 0)),
           pl.BlockSpec((None, None, f, d), lambda i: (layer, 1, 0, 0)), vec, vec],
        out_specs=pl.BlockSpec((tm, d), row),
        out_shape=jax.ShapeDtypeStruct((m, d), F32),
        compiler_params=_cparams(("parallel",)),
        name="out_ffn_ln",
    )(x, *outs, w_out, g1, b1, w_up, w_down, g2, b2)


def _head_group_norm(o, jn, gain):
    mu = _xdot(o, jn)
    d = o - mu
    var = _xdot(d * d, jn)
    return d * lax.rsqrt(var + LN_EPS) * gain


GLA_CHUNK = 16
GLA_ROWS = 128
RET_CHUNK = 256


def _gla_prompt_body(x_ref, wa_ref, ba_ref, gain_ref, tri_ref, tot_ref, emask_ref, jm_ref, jn_ref,
                     o_ref, st_ref, s_scr):
    c, r = GLA_CHUNK, GLA_ROWS
    g = r // c
    kd = 128
    seq = x_ref.shape[0]
    s_scr[...] = jnp.zeros_like(s_scr)
    jm = jm_ref[...]
    si = lax.broadcasted_iota(jnp.int32, (g, c, c, kd), 1)
    ti = lax.broadcasted_iota(jnp.int32, (g, c, c, kd), 2)
    causal = ti >= si

    def step(i, carry):
        rows = pl.ds(pl.multiple_of(i * r, r), r)
        q = x_ref[rows, 0:128] * (GLA_DK ** -0.5)
        k = x_ref[rows, 128:256]
        v = x_ref[rows, 256:512]
        gate = x_ref[rows, 512:768]
        la = jax.nn.log_sigmoid(_hdot(x_ref[rows, 768:896], wa_ref[...]) + ba_ref[...]) / GLA_TAU
        bt = _hdot(tri_ref[...], la)
        btot = _hdot(tot_ref[...], la)
        bt4 = bt.reshape(g, c, kd)
        q4 = q.reshape(g, c, kd)
        k4 = k.reshape(g, c, kd)
        dlt = jnp.minimum(bt4[:, None, :, :] - bt4[:, :, None, :], 0.0)
        w = jnp.where(causal, q4[:, None, :, :] * k4[:, :, None, :] * jnp.exp(dlt), 0.0)
        z = _bdot(w.reshape(g * c * c, kd), jm).reshape(g, c, c, GROUP_WIDTH)
        o = jnp.sum(z * v.reshape(g, c, GROUP_WIDTH)[:, :, None, :], axis=1).reshape(r, GROUP_WIDTH)
        qd = q * jnp.exp(bt)
        kdn = k * jnp.exp(btot - bt)
        edec = jnp.exp(_hdot_tn(la, emask_ref[...]))
        upds = [jnp.where(jm > 0, _bdot_tn(kdn[j * c:(j + 1) * c], v[j * c:(j + 1) * c]), 0.0) for j in range(g)]
        s_cur = s_scr[...]
        starts = []
        for j in range(g):
            starts.append(s_cur.astype(BF16))
            s_cur = jnp.broadcast_to(edec[:, j:j + 1], (kd, GROUP_WIDTH)) * s_cur + upds[j]
        s_scr[...] = s_cur
        chunk_of_row = lax.broadcasted_iota(jnp.int32, (r, kd), 0) // c
        qd_bd = jnp.concatenate([jnp.where(chunk_of_row == j, qd, 0.0).astype(BF16) for j in range(g)], axis=1)
        o = o + jnp.dot(qd_bd, jnp.concatenate(starts, axis=0), preferred_element_type=F32)
        o_ref[rows, :] = _head_group_norm(o, jn_ref[...], gain_ref[...]) * (gate * jax.nn.sigmoid(gate))
        return carry

    lax.fori_loop(0, seq // r, step, 0)
    s_fin = s_scr[...]
    for h in range(N_HEADS):
        st_ref[0, h] = s_fin[h * GLA_DK:(h + 1) * GLA_DK, h * HEAD_DIM:(h + 1) * HEAD_DIM]


def _ret_prompt_body(x_ref, gain_ref, dstack_ref, qdec_ref, kdec_ref, sdec_ref, hm_ref, jm_ref, jn_ref,
                     o_ref, st_ref, s_scr):
    c = RET_CHUNK
    seq = x_ref.shape[0]
    s_scr[...] = jnp.zeros_like(s_scr)

    def step(i, carry):
        rows = pl.ds(pl.multiple_of(i * c, c), c)
        q = x_ref[rows, 0:256]
        k = x_ref[rows, 256:512] * (HEAD_DIM ** -0.5)
        v = x_ref[rows, 512:768]
        gate = x_ref[rows, 768:1024]
        hm = hm_ref[...]
        qs = jnp.concatenate([q] * N_HEADS, axis=0) * hm
        sc = _bdot_nt(qs, k) * dstack_ref[...]
        of = _bdot(sc, v) * hm
        o = of[0:c]
        for h in range(1, N_HEADS):
            o = o + of[h * c:(h + 1) * c]
        s_old = s_scr[...]
        o = o + _bdot(q * qdec_ref[...], s_old)
        upd = jnp.where(jm_ref[...] > 0, _bdot_tn(k * kdec_ref[...], v), 0.0)
        s_scr[...] = sdec_ref[...] * s_old + upd
        o_ref[rows, :] = _head_group_norm(o, jn_ref[...], gain_ref[...]) * (gate * jax.nn.sigmoid(gate))
        return carry

    lax.fori_loop(0, seq // c, step, 0)
    s_fin = s_scr[...]
    for h in range(N_HEADS):
        st_ref[0, h] = s_fin[h * HEAD_DIM:(h + 1) * HEAD_DIM, h * HEAD_DIM:(h + 1) * HEAD_DIM]


def _blockdiag(kd, dk):
    r = np.arange(kd)[:, None] // dk
    cidx = np.arange(GROUP_WIDTH)[None, :] // HEAD_DIM
    return (r == cidx).astype(np.float32)


def _lin_prompt_call(body, name, x, batch, seq, kd, dk, consts):
    full = lambda a: pl.BlockSpec(a.shape, lambda b: (0,) * a.ndim)
    return pl.pallas_call(
        body,
        grid=(batch,),
        in_specs=[pl.BlockSpec((seq, x.shape[1]), lambda b: (b, 0))] + [full(a) for a in consts],
        out_specs=[pl.BlockSpec((seq, GROUP_WIDTH), lambda b: (b, 0)),
                   pl.BlockSpec((1, N_HEADS, dk, HEAD_DIM), lambda b: (b, 0, 0, 0))],
        out_shape=[jax.ShapeDtypeStruct((batch * seq, GROUP_WIDTH), F32),
                   jax.ShapeDtypeStruct((batch, N_HEADS, dk, HEAD_DIM), F32)],
        scratch_shapes=[pltpu.VMEM((kd, GROUP_WIDTH), F32)],
        compiler_params=_cparams(("parallel",)),
        name=name,
    )(x, *consts)


def gla_prompt(x, batch, seq, wa, ba, gain):
    r, c = GLA_ROWS, GLA_CHUNK
    same = (np.arange(r)[:, None] // c) == (np.arange(r)[None, :] // c)
    tri = (same & (np.arange(r)[:, None] >= np.arange(r)[None, :])).astype(np.float32)
    emask = ((np.arange(r)[:, None] // c) == np.arange(LANES)[None, :]).astype(np.float32)
    consts = [wa, ba, gain, jnp.asarray(tri), jnp.asarray(same.astype(np.float32)), jnp.asarray(emask),
              jnp.asarray(_blockdiag(128, GLA_DK)), jnp.asarray(_blockdiag(GROUP_WIDTH, HEAD_DIM) / HEAD_DIM)]
    return _lin_prompt_call(_gla_prompt_body, "gla_prompt", x, batch, seq, 128, GLA_DK, consts)


def ret_prompt(x, batch, seq, gain):
    c = RET_CHUNK
    log_gamma = np.log1p(-np.power(2.0, -5.0 - np.arange(N_HEADS, dtype=np.float64)))
    t = np.arange(c)
    diff = t[:, None] - t[None, :]
    dstack = np.concatenate([np.where(diff >= 0, np.exp(lg * np.maximum(diff, 0)), 0.0) for lg in log_gamma], axis=0)
    per_lane = np.repeat(log_gamma, HEAD_DIM)
    qdec = np.exp(per_lane[None, :] * (t[:, None] + 1))
    kdec = np.exp(per_lane[None, :] * (c - 1 - t[:, None]))
    sdec = np.broadcast_to(np.exp(per_lane * c)[:, None], (GROUP_WIDTH, GROUP_WIDTH))
    hm = np.concatenate([np.broadcast_to(np.arange(GROUP_WIDTH)[None, :] // HEAD_DIM == h, (c, GROUP_WIDTH))
                         for h in range(N_HEADS)], axis=0)
    as_f32 = lambda a: jnp.asarray(np.asarray(a, np.float32))
    consts = [gain, as_f32(dstack), as_f32(qdec), as_f32(kdec), as_f32(sdec), as_f32(hm),
              jnp.asarray(_blockdiag(GROUP_WIDTH, HEAD_DIM)),
              jnp.asarray(_blockdiag(GROUP_WIDTH, HEAD_DIM) / HEAD_DIM)]
    return _lin_prompt_call(_ret_prompt_body, "ret_prompt", x, batch, seq, GROUP_WIDTH, HEAD_DIM, consts)


def _lin_decode_body(x_ref, s0_ref, wa_ref, ba_ref, la_ref, gain_ref, o_ref, s1_ref, a_scr, o_scr,
                     *, kind, kd, dk):
    steps = x_ref.shape[0]
    if kind == "gla":
        qo, ko, vo, go = 0, 128, 256, 512
        for t in range(steps):
            pre = _hdot(wa_ref[...], x_ref[t, 768:896, :]) + ba_ref[...]
            a_scr[t] = jnp.exp(jax.nn.log_sigmoid(pre) / GLA_TAU)
        qscale, kscale = GLA_DK ** -0.5, 1.0
    else:
        qo, ko, vo, go = 0, 256, 512, 768
        for t in range(steps):
            a_scr[t] = jnp.exp(jnp.broadcast_to(la_ref[...], a_scr.shape[1:]))
        qscale, kscale = 1.0, HEAD_DIM ** -0.5
    o_scr[...] = jnp.zeros_like(o_scr)

    def body(j8, carry):
        r0 = pl.multiple_of(j8 * 8, 8)
        h = r0 // dk
        vrows = pl.ds(pl.multiple_of(vo + h * HEAD_DIM, HEAD_DIM), HEAD_DIM)
        orows = pl.ds(pl.multiple_of(h * HEAD_DIM, HEAD_DIM), HEAD_DIM)
        a8 = [a_scr[t, pl.ds(r0, 8), :] for t in range(steps)]
        q8 = [x_ref[t, pl.ds(qo + r0, 8), :] * qscale for t in range(steps)]
        k8 = [x_ref[t, pl.ds(ko + r0, 8), :] * kscale for t in range(steps)]
        for jj in range(8):
            sj = s0_ref[r0 + jj]
            for t in range(steps):
                vt = x_ref[t, vrows, :]
                sj = a8[t][jj:jj + 1, :] * sj + k8[t][jj:jj + 1, :] * vt
                o_scr[t, orows, :] += q8[t][jj:jj + 1, :] * sj
            s1_ref[r0 + jj] = sj
        return carry

    lax.fori_loop(0, kd // 8, body, 0)
    for t in range(steps):
        for h in range(N_HEADS):
            blk = slice(h * HEAD_DIM, (h + 1) * HEAD_DIM)
            o = o_scr[t, blk, :]
            mu = jnp.mean(o, axis=0, keepdims=True)
            d = o - mu
            var = jnp.mean(d * d, axis=0, keepdims=True)
            gate = x_ref[t, go + h * HEAD_DIM:go + (h + 1) * HEAD_DIM, :]
            o_ref[t, blk, :] = d * lax.rsqrt(var + LN_EPS) * gain_ref[blk, :] * (gate * jax.nn.sigmoid(gate))


def lin_decode(xt, s0, kind, wa_t, ba_col, la_col, gain_col):
    kd, dk = (128, GLA_DK) if kind == "gla" else (256, HEAD_DIM)
    steps, _, nb = xt.shape
    return pl.pallas_call(
        functools.partial(_lin_decode_body, kind=kind, kd=kd, dk=dk),
        out_shape=[jax.ShapeDtypeStruct((steps, GROUP_WIDTH, nb), F32),
                   jax.ShapeDtypeStruct((kd, HEAD_DIM, nb), F32)],
        scratch_shapes=[pltpu.VMEM((steps, kd, nb), F32), pltpu.VMEM((steps, GROUP_WIDTH, nb), F32)],
        compiler_params=pltpu.CompilerParams(vmem_limit_bytes=VMEM_LIMIT),
        name="lin_decode_" + kind,
    )(xt, s0, wa_t, ba_col, la_col, gain_col)


def _cmp_mlp(xs_list, pos_ref, w1_ref, w2_ref, n_rows):
    xcat = jnp.concatenate(
        [jnp.concatenate([(xs[pl.ds(r, n_rows, stride=CMP_STRIDE), :] + pos_ref[r:r + 1, :]).astype(BF16)
                          for r in range(CMP_LEN)], axis=1) for xs in xs_list], axis=0)
    hid = jnp.dot(xcat, w1_ref[...].reshape(CMP_LEN * LANES, 2 * CMP_HIDDEN), preferred_element_type=F32)
    return _bdot(jax.nn.gelu(hid), w2_ref[...])


def _rank_select(vals, idx, n_rows, n_top, axis):
    cnt = jnp.zeros(vals.shape, F32)
    for j in range(n_rows):
        vj = vals[j:j + 1, :] if axis == 0 else vals[:, j:j + 1]
        before = (vj > vals) | ((vj == vals) & (j < idx))
        cnt = cnt + before.astype(F32)
    return cnt < n_top


DECODE_SEQS = 4
TQ = 256
TK = 256


POS_HI_LANE = HEAD_DIM
POS_LO_LANE = HEAD_DIM + 1


def _nt(a, b):
    return lax.dot_general(a, b, (((1,), (1,)), ((), ())), preferred_element_type=F32)


def _tn(a, b):
    return lax.dot_general(a, b, (((0,), (0,)), ((), ())), preferred_element_type=F32)


def _key_tile(x128, pos):
    lane = lax.broadcasted_iota(jnp.int32, x128.shape, 1)
    feat = jnp.where(lane == POS_HI_LANE, (pos // 64).astype(F32),
                     jnp.where(lane == POS_LO_LANE, (pos % 64).astype(F32), 0.0))
    return jnp.where(lane < HEAD_DIM, x128, feat).astype(BF16)


def _query_stack(q256, slopes):
    tq = q256.shape[0]
    lane = lax.broadcasted_iota(jnp.int32, (tq, LANES), 1)
    parts = []
    for h in range(N_HEADS):
        pair = q256[:, (h // 2) * LANES:(h // 2 + 1) * LANES]
        if h % 2:
            pair = pltpu.roll(pair, HEAD_DIM, 1)
        const = jnp.where(lane == POS_HI_LANE, 64.0 * slopes[h], jnp.where(lane == POS_LO_LANE, slopes[h], 0.0))
        parts.append(jnp.where(lane < HEAD_DIM, pair * (HEAD_DIM ** -0.5), const).astype(BF16))
    return jnp.concatenate(parts, axis=0)


def _tile_step(k_fn, v_fn, qst_scr, bias, m_scr, l_scr, acc_scr, first, tq, group):
    starts = list(range(0, N_HEADS * tq, group))
    scores = []
    for c0 in starts:
        s = _nt(k_fn(c0 // tq), qst_scr[c0:c0 + group, :])
        scores.append(s if bias is None else s + bias[:, c0:c0 + group])
    probs, stats = [], []
    for c0, s in zip(starts, scores):
        cols = slice(c0, c0 + group)
        if first:
            m_new = jnp.max(s, axis=0, keepdims=True)
            p = jnp.exp(s - m_new)
            stats.append((m_new, jnp.sum(p, axis=0, keepdims=True), None))
        else:
            m_old = m_scr[:, cols]
            m_new = jnp.maximum(m_old, jnp.max(s, axis=0, keepdims=True))
            p = jnp.exp(s - m_new)
            corr = jnp.exp(m_old - m_new)
            stats.append((m_new, corr * l_scr[:, cols] + jnp.sum(p, axis=0, keepdims=True), corr))
        probs.append(p.astype(BF16))
    for c0, p, (m_new, l_new, corr) in zip(starts, probs, stats):
        cols = slice(c0, c0 + group)
        pv = _tn(v_fn(c0 // tq), p)
        acc_scr[:, cols] = pv if first else corr * acc_scr[:, cols] + pv
        m_scr[:, cols] = m_new
        l_scr[:, cols] = l_new


def _tile4(x):
    return jnp.concatenate([x] * N_HEADS, axis=1)


def _tri_bias(tq, keep_lower):
    row = lax.broadcasted_iota(jnp.int32, (TK, tq), 0)
    col = lax.broadcasted_iota(jnp.int32, (TK, tq), 1)
    return _tile4(jnp.where((row <= col) if keep_lower else (row > col), 0.0, NEG))


def _nsa_prompt_body(q_ref, gate_ref, kv_ref, win_ref, w1_ref, pos_ref, w2_ref, imp_ref, o_ref,
                     xs_scr, kc_scr, vc_scr, ks_scr, vs_scr, kw_scr, vw_scr, selb_scr,
                     qst_scr, ocmp_scr, m_scr, l_scr, acc_scr, *, seq):
    qi = pl.program_id(1)
    tq = q_ref.shape[0]
    ncp = seq // CMP_STRIDE
    n_cmp = ncp - 1
    n_sel = seq // SEL_BLOCK
    blocks_per_tile = TK // SEL_BLOCK

    @pl.when(qi == 0)
    def _():
        xs_scr[0:seq, :] = kv_ref[:, 0:128]
        xs_scr[seq:seq + CMP_STRIDE, :] = jnp.zeros((CMP_STRIDE, LANES), F32)
        kvc = _cmp_mlp([xs_scr], pos_ref, w1_ref, w2_ref, ncp)
        cend = lax.broadcasted_iota(jnp.int32, (ncp, LANES), 0) * CMP_STRIDE + (CMP_LEN - 1)
        kc_scr[...] = _key_tile(kvc, cend)
        vc_scr[...] = kvc[:, HEAD_DIM:2 * HEAD_DIM].astype(BF16)
        kpos = lax.broadcasted_iota(jnp.int32, (seq, LANES), 0)
        ks_scr[...] = _key_tile(kv_ref[:, 128:256], kpos)
        vs_scr[...] = kv_ref[:, 192:256].astype(BF16)
        kw_scr[...] = _key_tile(win_ref[:, 0:128], kpos)
        vw_scr[...] = win_ref[:, 64:128].astype(BF16)

    qpos = qi * tq + lax.broadcasted_iota(jnp.int32, (1, tq), 1)
    gates_t = jax.nn.sigmoid(gate_ref[:, 128:256]).T
    qst_scr[...] = _query_stack(q_ref[...], NSA_SLOPES)

    n_io = lax.broadcasted_iota(jnp.int32, (ncp, tq), 0)
    cmask = _tile4(((n_io * CMP_STRIDE + (CMP_LEN - 1)) <= qpos) & (n_io < n_cmp))
    s = jnp.where(cmask, _nt(kc_scr[...], qst_scr[...]), NEG)
    e = jnp.where(cmask, jnp.exp(s - jnp.max(s, axis=0, keepdims=True)), 0.0)
    pc = e / jnp.maximum(jnp.sum(e, axis=0, keepdims=True), TINY)
    ocmp_scr[...] = _tn(vc_scr[...], pc.astype(BF16))
    pc_sum = pc[:, 0:tq]
    for h in range(1, N_HEADS):
        pc_sum = pc_sum + pc[:, h * tq:(h + 1) * tq]

    imp = _xdot_left(imp_ref[...], pc_sum)
    blk = lax.broadcasted_iota(jnp.int32, (n_sel, tq), 0)
    cur = qpos // SEL_BLOCK
    forced = (blk == 0) | (blk == cur) | (blk == cur - 1)
    vals = jnp.where(blk <= cur, jnp.where(forced, BIG, imp), NEG)
    chosen = _rank_select(vals, blk, n_sel, min(N_SEL, n_sel), 0)
    selb_scr[...] = jnp.where(chosen, 0.0, NEG)

    def sel_bias(kt):
        rows = [jnp.broadcast_to(selb_scr[pl.ds(kt * blocks_per_tile + i, 1), :], (SEL_BLOCK, tq))
                for i in range(blocks_per_tile)]
        return _tile4(jnp.concatenate(rows, axis=0))

    def tile_rows(kt):
        return pl.ds(pl.multiple_of(kt * TK, TK), TK)

    def step(k_scr, v_scr, rows, bias, first):
        _tile_step(lambda h: k_scr[rows, :], lambda h: v_scr[rows, :], qst_scr, bias,
                   m_scr, l_scr, acc_scr, first, tq, 2 * tq)

    diag = tile_rows(qi)
    step(ks_scr, vs_scr, diag, sel_bias(qi) + _tri_bias(tq, True), True)

    def sel_body(kt, carry):
        step(ks_scr, vs_scr, tile_rows(kt), sel_bias(kt), False)
        return carry

    lax.fori_loop(0, qi, sel_body, 0)
    gate_row = lambda j: jnp.concatenate([gates_t[3 * h + j:3 * h + j + 1, :] for h in range(N_HEADS)], axis=1)
    out = gate_row(0) * ocmp_scr[...] + gate_row(1) * (acc_scr[...] / l_scr[...])

    step(kw_scr, vw_scr, diag, _tri_bias(tq, True), True)

    @pl.when(qi >= 1)
    def _():
        step(kw_scr, vw_scr, tile_rows(qi - 1), None, False)

    @pl.when(qi >= WINDOW // TK)
    def _():
        step(kw_scr, vw_scr, tile_rows(qi - WINDOW // TK), _tri_bias(tq, False), False)

    out = out + gate_row(2) * (acc_scr[...] / l_scr[...])
    o_ref[...] = jnp.concatenate([out[:, h * tq:(h + 1) * tq] for h in range(N_HEADS)], axis=0).T


def _cmp_to_sel(n_cmp, n_sel):
    ratio, span = SEL_BLOCK // CMP_STRIDE, CMP_LEN // CMP_STRIDE
    j, m, n = np.meshgrid(np.arange(n_sel), np.arange(ratio), np.arange(span), indexing="ij")
    i = ratio * j + m - n
    ok = (i >= 0) & (i < n_cmp)
    mat = np.zeros((n_cmp, n_sel), np.float32)
    np.add.at(mat, (i[ok], j[ok]), 1.0)
    return mat


def nsa_prompt(xq, xkv, xwg, batch, seq, w1, pos, w2):
    nq = seq // TQ
    ncp = seq // CMP_STRIDE
    n_sel = seq // SEL_BLOCK
    imp_t = np.zeros((n_sel, ncp), np.float32)
    imp_t[:, :ncp - 1] = _cmp_to_sel(ncp - 1, n_sel).T
    full = lambda *shape: pl.BlockSpec(shape, lambda b, i: (0,) * len(shape))
    return pl.pallas_call(
        functools.partial(_nsa_prompt_body, seq=seq),
        grid=(batch, nq),
        in_specs=[pl.BlockSpec((TQ, GROUP_WIDTH), lambda b, i: (b * nq + i, 0)),
                  pl.BlockSpec((TQ, GROUP_WIDTH), lambda b, i: (b * nq + i, 0)),
                  pl.BlockSpec((seq, GROUP_WIDTH), lambda b, i: (b, 0)),
                  pl.BlockSpec((seq, GROUP_WIDTH), lambda b, i: (b, 0)),
                  full(CMP_LEN, LANES, 2 * CMP_HIDDEN), full(CMP_LEN, LANES),
                  full(2 * CMP_HIDDEN, LANES), full(n_sel, ncp)],
        out_specs=pl.BlockSpec((TQ, GROUP_WIDTH), lambda b, i: (b * nq + i, 0)),
        out_shape=jax.ShapeDtypeStruct((batch * seq, GROUP_WIDTH), F32),
        scratch_shapes=[pltpu.VMEM((seq + CMP_STRIDE, LANES), F32),
                        pltpu.VMEM((ncp, LANES), BF16), pltpu.VMEM((ncp, HEAD_DIM), BF16),
                        pltpu.VMEM((seq, LANES), BF16), pltpu.VMEM((seq, HEAD_DIM), BF16),
                        pltpu.VMEM((seq, LANES), BF16), pltpu.VMEM((seq, HEAD_DIM), BF16),
                        pltpu.VMEM((n_sel, TQ), F32),
                        pltpu.VMEM((N_HEADS * TQ, LANES), BF16), pltpu.VMEM((HEAD_DIM, N_HEADS * TQ), F32),
                        pltpu.VMEM((1, N_HEADS * TQ), F32), pltpu.VMEM((1, N_HEADS * TQ), F32),
                        pltpu.VMEM((HEAD_DIM, N_HEADS * TQ), F32)],
        compiler_params=_cparams(("parallel", "arbitrary")),
        name="nsa_prompt",
    )(xq, xwg, xkv, xwg, w1, pos, w2, jnp.asarray(imp_t))


def _moba_prompt_body(q_ref, kv_ref, o_ref, km_scr, ka_scr, va_scr, selb_scr, qst_scr,
                      m_scr, l_scr, acc_scr, *, seq):
    qi = pl.program_id(1)
    tq = q_ref.shape[0]
    nb = seq // MOBA_BLOCK

    @pl.when(qi == 0)
    def _():
        for n in range(nb):
            km_scr[n:n + 1, :] = jnp.mean(kv_ref[n * MOBA_BLOCK:(n + 1) * MOBA_BLOCK, 0:GROUP_WIDTH],
                                          axis=0, keepdims=True)

        kpos = lax.broadcasted_iota(jnp.int32, (seq, LANES), 0)
        for h in range(N_HEADS):
            pair = kv_ref[:, (h // 2) * LANES:(h // 2 + 1) * LANES]
            if h % 2:
                pair = pltpu.roll(pair, HEAD_DIM, 1)
            ka_scr[h] = _key_tile(pair, kpos)
            va_scr[h] = kv_ref[:, GROUP_WIDTH + h * HEAD_DIM:GROUP_WIDTH + (h + 1) * HEAD_DIM].astype(BF16)

    qpos = qi * tq + lax.broadcasted_iota(jnp.int32, (1, tq), 1)
    own = qpos // MOBA_BLOCK
    blk = lax.broadcasted_iota(jnp.int32, (nb, tq), 0)
    past = blk < own
    for h in range(N_HEADS):
        hs = slice(h * HEAD_DIM, (h + 1) * HEAD_DIM)
        gate = _hdot_nt(km_scr[:, hs], q_ref[:, hs])
        chosen = _rank_select(jnp.where(past, gate, NEG), blk, nb, min(MOBA_TOPK, nb), 0) & past
        selb_scr[h] = jnp.where(chosen, 0.0, NEG)

    qst_scr[...] = _query_stack(q_ref[...], MOBA_SLOPES)

    def step(rows, bias, first):
        _tile_step(lambda h: ka_scr[h, rows, :], lambda h: va_scr[h, rows, :], qst_scr, bias,
                   m_scr, l_scr, acc_scr, first, tq, tq)

    step(pl.ds(pl.multiple_of(qi * TK, TK), TK), _tri_bias(tq, True), True)

    def body(kt, carry):
        bias = jnp.concatenate([selb_scr[h, pl.ds(kt, 1), :] for h in range(N_HEADS)], axis=1)
        step(pl.ds(pl.multiple_of(kt * TK, TK), TK), bias, False)
        return carry

    lax.fori_loop(0, qi, body, 0)
    out = acc_scr[...] / l_scr[...]
    o_ref[...] = jnp.concatenate([out[:, h * tq:(h + 1) * tq] for h in range(N_HEADS)], axis=0).T


def moba_prompt(xq, xkv, batch, seq):
    nq = seq // TQ
    nb = seq // MOBA_BLOCK
    return pl.pallas_call(
        functools.partial(_moba_prompt_body, seq=seq),
        grid=(batch, nq),
        in_specs=[pl.BlockSpec((TQ, GROUP_WIDTH), lambda b, i: (b * nq + i, 0)),
                  pl.BlockSpec((seq, 2 * GROUP_WIDTH), lambda b, i: (b, 0))],
        out_specs=pl.BlockSpec((TQ, GROUP_WIDTH), lambda b, i: (b * nq + i, 0)),
        out_shape=jax.ShapeDtypeStruct((batch * seq, GROUP_WIDTH), F32),
        scratch_shapes=[pltpu.VMEM((nb, GROUP_WIDTH), F32),
                        pltpu.VMEM((N_HEADS, seq, LANES), BF16), pltpu.VMEM((N_HEADS, seq, HEAD_DIM), BF16),
                        pltpu.VMEM((N_HEADS, nb, TQ), F32),
                        pltpu.VMEM((N_HEADS * TQ, LANES), BF16),
                        pltpu.VMEM((1, N_HEADS * TQ), F32), pltpu.VMEM((1, N_HEADS * TQ), F32),
                        pltpu.VMEM((HEAD_DIM, N_HEADS * TQ), F32)],
        compiler_params=_cparams(("parallel", "arbitrary")),
        name="moba_prompt",
    )(xq, xkv)


def _softmax_rows(parts, masks):
    masked = [jnp.where(mk, s, NEG) for s, mk in zip(parts, masks)]
    m = masked[0].max(axis=1, keepdims=True)
    for s in masked[1:]:
        m = jnp.maximum(m, s.max(axis=1, keepdims=True))
    es = [jnp.where(mk, jnp.exp(s - m), 0.0) for s, mk in zip(masked, masks)]
    den = es[0].sum(axis=1, keepdims=True)
    for e in es[1:]:
        den = den + e.sum(axis=1, keepdims=True)
    inv = 1.0 / jnp.maximum(den, TINY)
    return [e * inv for e in es]


def _stack_heads(x, off):
    return jnp.concatenate([x[:, off + h * HEAD_DIM:off + (h + 1) * HEAD_DIM] for h in range(N_HEADS)], axis=0)


def _round_robin(gens):
    outs = [None] * len(gens)
    live = list(range(len(gens)))
    while live:
        for i in list(live):
            try:
                next(gens[i])
            except StopIteration as stop:
                outs[i] = stop.value
                live.remove(i)
    return outs


def _pad_rows(x, n):
    return jnp.concatenate([x, jnp.zeros((n - x.shape[0], x.shape[1]), x.dtype)], axis=0)


def _nsa_decode_body(*refs, n_pages, past_len, ns):
    x_refs = refs[1:4]
    all_pages = refs[4:4 + ns * n_pages]
    (win_ref, w1_ref, pos_ref, w2_ref, imp_ref, esel_ref, place_ref, _, o_ref, wout_ref,
     xs_scr) = refs[4 + ns * n_pages:]
    steps = x_refs[0].shape[0] // ns
    x_rows = lambda i: jnp.concatenate([r[i * steps:(i + 1) * steps, :] for r in x_refs], axis=1)
    ncp = past_len // CMP_STRIDE
    n_win = win_ref.shape[-1]
    lane = lax.broadcasted_iota(jnp.int32, (HEAD_DIM, n_win), 1)
    for i in range(ns):
        new_rows = x_refs[2][i * steps:(i + 1) * steps, 0:2 * HEAD_DIM]
        for c in range(2):
            placed = _hdot_tn(new_rows[:, c * HEAD_DIM:(c + 1) * HEAD_DIM], place_ref[...])
            shifted = pltpu.roll(win_ref[0, i, c], n_win - steps, 1)
            wout_ref[0, i, c] = jnp.where(lane < n_win - steps, shifted, placed)
    for i in range(ns):
        for p in range(n_pages):
            xs_scr[i, p * PAGE_SIZE:(p + 1) * PAGE_SIZE, :] = (
                all_pages[i * n_pages + p][0, 0, 0:2].reshape(LANES, PAGE_SIZE).T)
        xs_scr[i, past_len:past_len + CMP_STRIDE, :] = jnp.zeros((CMP_STRIDE, LANES), F32)
    kvc_all = _cmp_mlp([xs_scr.at[i] for i in range(ns)], pos_ref, w1_ref, w2_ref, ncp)
    outs = _round_robin([_nsa_decode_seq(
        x_rows(i), all_pages[i * n_pages:(i + 1) * n_pages], win_ref, i,
        kvc_all[i * ncp:(i + 1) * ncp], imp_ref, esel_ref, n_pages, past_len) for i in range(ns)])
    for i in range(ns):
        o_ref[i * steps:(i + 1) * steps, :] = outs[i]


def _nsa_decode_seq(x, page_refs, win_ref, wi, kvc, imp_ref, esel_ref, n_pages, past_len):
    steps = x.shape[0]
    rows = N_HEADS * steps
    ncp = past_len // CMP_STRIDE
    n_cmp = ncp - 1
    tpos = past_len + lax.broadcasted_iota(jnp.int32, (steps, 1), 0)
    qpos = jnp.concatenate([tpos] * N_HEADS, axis=0)
    slope = jnp.concatenate([jnp.full((steps, 1), s, F32) for s in NSA_SLOPES], axis=0)
    qs = _stack_heads(x, 0) * (HEAD_DIM ** -0.5)
    gates = jax.nn.sigmoid(x[:, 640:768])
    gcol = [jnp.concatenate([gates[:, 3 * h + j:3 * h + j + 1] for h in range(N_HEADS)], axis=0)
            for j in range(3)]
    new_io = lax.broadcasted_iota(jnp.int32, (rows, LANES), 1)
    new_pos = past_len + new_io
    new_ok = (new_io < steps) & (new_pos <= qpos)

    k_cmp, v_cmp = kvc[:, 0:HEAD_DIM], kvc[:, HEAD_DIM:2 * HEAD_DIM]
    n_io = lax.broadcasted_iota(jnp.int32, (rows, ncp), 1)
    cend = n_io * CMP_STRIDE + (CMP_LEN - 1)
    (pc,) = _softmax_rows([_bdot_nt(qs, k_cmp) + slope * cend.astype(F32)],
                          [(cend <= qpos) & (n_io < n_cmp)])
    o_cmp = _bdot(pc, v_cmp)
    yield
    pc_sum = pc[0:steps]
    for h in range(1, N_HEADS):
        pc_sum = pc_sum + pc[h * steps:(h + 1) * steps]

    n_sel = -(-(past_len + steps) // SEL_BLOCK)
    imp = _hdot(pc_sum, imp_ref[...])
    yield
    blk = lax.broadcasted_iota(jnp.int32, (steps, LANES), 1)
    cur = tpos // SEL_BLOCK
    forced = (blk == 0) | (blk == cur) | (blk == cur - 1)
    vals = jnp.where((blk <= cur) & (blk < n_sel), jnp.where(forced, BIG, imp), NEG)
    chosen = _rank_select(vals, blk, n_sel, min(N_SEL, n_sel), 1).astype(F32)
    key_sel = _bdot(chosen, esel_ref[...])
    key_sel = jnp.concatenate([key_sel] * N_HEADS, axis=0) > 0.5
    yield

    cat_pages = lambda c: jnp.concatenate([page_refs[p][0, 0, c].astype(BF16) for p in range(n_pages)], axis=1)
    s_past = _bdot(qs, cat_pages(2))
    kpos = lax.broadcasted_iota(jnp.int32, (rows, past_len), 1)
    k_new = _pad_rows(x[:, 384:448], LANES)
    v_new = _pad_rows(x[:, 448:512], LANES)
    s_new = _bdot_nt(qs, k_new) + slope * new_pos.astype(F32)
    yield
    p_past, p_new = _softmax_rows([s_past + slope * kpos.astype(F32), s_new],
                                  [key_sel[:, 0:past_len] & (kpos <= qpos),
                                   key_sel[:, past_len:past_len + LANES] & new_ok])
    o_sel = _bdot(p_new, v_new) + _bdot_nt(p_past, cat_pages(3))
    yield

    n_win = win_ref.shape[-1]
    wpos = (past_len - n_win) + lax.broadcasted_iota(jnp.int32, (rows, n_win), 1)
    kw_new = _pad_rows(x[:, 512:576], LANES)
    vw_new = _pad_rows(x[:, 576:640], LANES)
    s_w = _bdot(qs, win_ref[0, wi, 0]) + slope * wpos.astype(F32)
    s_wn = _bdot_nt(qs, kw_new) + slope * new_pos.astype(F32)
    yield
    dist = qpos - wpos
    p_w, p_wn = _softmax_rows([s_w, s_wn],
                              [(dist >= 0) & (dist < WINDOW), new_ok & (qpos - new_pos < WINDOW)])
    o_win = _bdot_nt(p_w, win_ref[0, wi, 1]) + _bdot(p_wn, vw_new)

    out = gcol[0] * o_cmp + gcol[1] * o_sel + gcol[2] * o_win
    return jnp.concatenate([out[h * steps:(h + 1) * steps] for h in range(N_HEADS)], axis=1)


def nsa_decode(xs, page_ids, cache_t, layer, win_t, win_next, w1, pos, w2, *, steps, past_len):
    nseq = xs[0].shape[0] // steps
    n_pages = past_len // PAGE_SIZE
    ncp = past_len // CMP_STRIDE
    n_sel = -(-(past_len + steps) // SEL_BLOCK)
    imp = np.zeros((ncp, LANES), np.float32)
    imp[:ncp - 1, :n_sel] = _cmp_to_sel(ncp - 1, n_sel)
    kblk = np.concatenate([np.arange(past_len) // SEL_BLOCK,
                           (past_len + np.arange(LANES)) // SEL_BLOCK])
    esel = (np.arange(LANES)[:, None] == kblk[None, :]).astype(np.float32)
    n_win = win_t.shape[-1]

    ns = math.gcd(nseq, DECODE_SEQS)

    def page_spec(j):
        return pl.BlockSpec((1, 1, 4, HEAD_DIM, PAGE_SIZE),
                            lambda b, pt, j=j: (pt[b * (ns * n_pages) + j], layer, 0, 0, 0))

    place = (np.arange(n_win)[None, :] == (n_win - steps + np.arange(steps))[:, None]).astype(np.float32)
    full = lambda *shape: pl.BlockSpec(shape, lambda b, pt: (0,) * len(shape))
    win_spec = pl.BlockSpec((1, ns, 2, HEAD_DIM, n_win), lambda b, pt: (layer, b, 0, 0, 0))
    inputs = (page_ids, *xs, *([cache_t] * (ns * n_pages)), win_t, w1, pos, w2, jnp.asarray(imp),
              jnp.asarray(esel), jnp.asarray(place), win_next)
    return pl.pallas_call(
        functools.partial(_nsa_decode_body, n_pages=n_pages, past_len=past_len, ns=ns),
        grid_spec=pltpu.PrefetchScalarGridSpec(
            num_scalar_prefetch=1,
            grid=(nseq // ns,),
            in_specs=[pl.BlockSpec((ns * steps, GROUP_WIDTH), lambda b, pt: (b, 0)) for _ in xs]
            + [page_spec(j) for j in range(ns * n_pages)]
            + [win_spec,
               full(CMP_LEN, LANES, 2 * CMP_HIDDEN), full(CMP_LEN, LANES),
               full(2 * CMP_HIDDEN, LANES), full(ncp, LANES), full(LANES, past_len + LANES),
               full(steps, n_win), pl.BlockSpec(memory_space=pl.ANY)],
            out_specs=[pl.BlockSpec((ns * steps, GROUP_WIDTH), lambda b, pt: (b, 0)), win_spec],
            scratch_shapes=[pltpu.VMEM((ns, past_len + CMP_STRIDE, LANES), F32)],
        ),
        out_shape=[jax.ShapeDtypeStruct((nseq * steps, GROUP_WIDTH), F32),
                   jax.ShapeDtypeStruct(win_next.shape, F32)],
        input_output_aliases={len(inputs) - 1: 1},
        compiler_params=_cparams(("parallel",)),
        name="nsa_decode",
    )(*inputs)


def _moba_decode_body(*refs, n_pages, past_len, ns):
    x_refs = refs[1:3]
    all_pages = refs[3:3 + ns * n_pages]
    emean_ref, eblk_ref, o_ref = refs[3 + ns * n_pages:]
    steps = x_refs[0].shape[0] // ns
    x_rows = lambda i: jnp.concatenate([r[i * steps:(i + 1) * steps, :] for r in x_refs], axis=1)
    outs = _round_robin([_moba_decode_seq(
        x_rows(i), all_pages[i * n_pages:(i + 1) * n_pages],
        emean_ref, eblk_ref, n_pages, past_len) for i in range(ns)])
    for i in range(ns):
        o_ref[i * steps:(i + 1) * steps, :] = outs[i]


def _moba_decode_seq(x, page_refs, emean_ref, eblk_ref, n_pages, past_len):
    steps = x.shape[0]
    rows = N_HEADS * steps
    tpos = past_len + lax.broadcasted_iota(jnp.int32, (steps, 1), 0)
    qpos = jnp.concatenate([tpos] * N_HEADS, axis=0)
    own = qpos // MOBA_BLOCK
    slope = jnp.concatenate([jnp.full((steps, 1), s, F32) for s in MOBA_SLOPES], axis=0)
    head_of_row = lax.broadcasted_iota(jnp.int32, (rows, GROUP_WIDTH), 0) // steps
    head_of_col = lax.broadcasted_iota(jnp.int32, (rows, GROUP_WIDTH), 1) // HEAD_DIM
    diag = head_of_row == head_of_col
    q_bd = jnp.where(diag, jnp.concatenate([x[:, 0:GROUP_WIDTH]] * N_HEADS, axis=0), 0.0)

    cat_pages = lambda c: jnp.concatenate(
        [page_refs[p][0, 0, c].reshape(GROUP_WIDTH, PAGE_SIZE).astype(BF16) for p in range(n_pages)], axis=1)
    kt_all = cat_pages(0)
    kmean_t = jnp.dot(kt_all, emean_ref[...], preferred_element_type=F32)
    yield
    gate = _hdot(q_bd, kmean_t)
    yield
    nb = -(-(past_len + steps) // MOBA_BLOCK)
    blk = lax.broadcasted_iota(jnp.int32, (rows, LANES), 1)
    past = (blk < own) & (blk < nb)
    vals = jnp.where(past, gate, NEG)
    chosen = (_rank_select(vals, blk, nb, min(MOBA_TOPK, nb), 1) & past).astype(F32)
    key_sel = _bdot(chosen, eblk_ref[...]) > 0.5
    yield

    q_sc = q_bd * (HEAD_DIM ** -0.5)
    s_past = _bdot(q_sc, kt_all)
    kpos = lax.broadcasted_iota(jnp.int32, (rows, past_len), 1)
    new_io = lax.broadcasted_iota(jnp.int32, (rows, LANES), 1)
    new_pos = past_len + new_io
    k_new = _pad_rows(x[:, 256:512], LANES)
    v_new = _pad_rows(x[:, 512:768], LANES)
    s_new = _bdot_nt(q_sc, k_new) + slope * new_pos.astype(F32)
    yield
    in_own_past = (kpos // MOBA_BLOCK) == own
    in_own_new = (new_pos // MOBA_BLOCK) == own
    p_past, p_new = _softmax_rows(
        [s_past + slope * kpos.astype(F32), s_new],
        [key_sel | in_own_past, (new_io < steps) & (new_pos <= qpos) & in_own_new])
    o_all = _bdot(p_new, v_new) + _bdot_nt(p_past, cat_pages(1))
    o_all = jnp.where(diag, o_all, 0.0)
    out = o_all[0:steps]
    for h in range(1, N_HEADS):
        out = out + o_all[h * steps:(h + 1) * steps]
    return out


def moba_decode(xs, page_ids, cache_t, layer, *, steps, past_len):
    nseq = xs[0].shape[0] // steps
    n_pages = past_len // PAGE_SIZE
    pages_per_blk = MOBA_BLOCK // PAGE_SIZE
    emean = np.zeros((n_pages, PAGE_SIZE, LANES), np.float32)
    for p in range(n_pages):
        emean[p, :, p // pages_per_blk] = 1.0 / MOBA_BLOCK
    eblk = (np.arange(LANES)[:, None] == (np.arange(past_len) // MOBA_BLOCK)[None, :]).astype(np.float32)

    ns = math.gcd(nseq, DECODE_SEQS)

    def page_spec(j):
        return pl.BlockSpec((1, 1, 2, N_HEADS, HEAD_DIM, PAGE_SIZE),
                            lambda b, pt, j=j: (pt[b * (ns * n_pages) + j], layer, 0, 0, 0, 0))

    full = lambda *shape: pl.BlockSpec(shape, lambda b, pt: (0,) * len(shape))
    return pl.pallas_call(
        functools.partial(_moba_decode_body, n_pages=n_pages, past_len=past_len, ns=ns),
        grid_spec=pltpu.PrefetchScalarGridSpec(
            num_scalar_prefetch=1,
            grid=(nseq // ns,),
            in_specs=[pl.BlockSpec((ns * steps, a.shape[1]), lambda b, pt: (b, 0)) for a in xs]
            + [page_spec(j) for j in range(ns * n_pages)]
            + [full(past_len, LANES), full(LANES, past_len)],
            out_specs=pl.BlockSpec((ns * steps, GROUP_WIDTH), lambda b, pt: (b, 0)),
        ),
        out_shape=jax.ShapeDtypeStruct((nseq * steps, GROUP_WIDTH), F32),
        compiler_params=_cparams(("parallel",)),
        name="moba_decode",
    )(page_ids, *xs, *([cache_t] * (ns * n_pages)), jnp.asarray(emean.reshape(past_len, LANES), BF16),
      jnp.asarray(eblk))


def _regroup_w_in(w):
    z = lambda n: jnp.zeros(w.shape[:2] + (n,), w.dtype)
    col = lambda a, b: w[:, :, a:b]
    gq, gk, gv, ga, gr = col(0, 128), col(128, 256), col(256, 512), col(512, 528), col(528, 784)
    ret = col(784, 1808)
    nq, nkv, ng = col(1808, 2064), col(2064, 2448), col(2448, 2460)
    moba = col(2460, 3228)
    return jnp.concatenate([gq, gk, gv, gr, ga, z(112), ret, nq, nkv, ng, z(116), moba], axis=2).astype(BF16)


def _cmp_weights(w1, pos, w2):
    w1r = w1.reshape(2, CMP_LEN, HEAD_DIM, CMP_HIDDEN)
    zero = jnp.zeros((CMP_LEN, HEAD_DIM, CMP_HIDDEN), w1.dtype)
    w1bd = jnp.concatenate([jnp.concatenate([w1r[0], zero], axis=2),
                            jnp.concatenate([zero, w1r[1]], axis=2)], axis=1)
    zero2 = jnp.zeros((CMP_HIDDEN, HEAD_DIM), w2.dtype)
    w2bd = jnp.concatenate([jnp.concatenate([w2[0], zero2], axis=1),
                            jnp.concatenate([zero2, w2[1]], axis=1)], axis=0)
    posf = jnp.concatenate([pos[0], pos[1]], axis=1)
    return w1bd.astype(BF16), posf, w2bd.astype(BF16)


def kernel(x_prompt, x_sample, cache_nsa, cache_moba, cache_nsa_win, state_gla, state_ret, page_table, w_in, gla_w_a2, gla_b_a, gla_norm, ret_norm, nsa_cmp_w1, nsa_cmp_w2, nsa_cmp_pos, w_out, ffn_w_up, ffn_w_down, ln_g, ln_b):
    bp, sp, d = x_prompt.shape
    bs, ss, _ = x_sample.shape
    past_len = page_table.shape[1] * PAGE_SIZE
    assert cache_nsa_win.shape[2] >= ss, "the sample group's window buffer must hold at least the new rows"

    nsa_t = jnp.transpose(cache_nsa, (0, 1, 3, 4, 2))
    moba_t = jnp.transpose(cache_moba, (0, 1, 3, 4, 5, 2))
    win_t = jnp.transpose(cache_nsa_win, (0, 1, 3, 4, 2))
    gla_t = jnp.transpose(state_gla, (0, 2, 3, 4, 1)).reshape(DEPTH, N_HEADS * GLA_DK, HEAD_DIM, bs)
    ret_t = jnp.transpose(state_ret, (0, 2, 3, 4, 1)).reshape(DEPTH, N_HEADS * HEAD_DIM, HEAD_DIM, bs)
    page_ids = page_table.reshape(-1)

    log_gamma = np.log1p(-np.power(2.0, -5.0 - np.arange(N_HEADS, dtype=np.float64))).astype(np.float32)
    la_ret = jnp.asarray(np.repeat(log_gamma, HEAD_DIM))
    win_next = jnp.zeros(win_t.shape, F32)

    xp = x_prompt.reshape(bp * sp, d)
    xs = x_sample.reshape(bs * ss, d)
    outs = {k: [] for k in ("nsa_p", "moba_p", "win_p", "gla_p", "ret_p", "nsa_s", "moba_s", "gla_s", "ret_s")}
    wu = ffn_w_up.astype(BF16)
    wd = ffn_w_down.astype(BF16)
    wi = _regroup_w_in(w_in)
    wo = w_out.astype(BF16)
    for l in range(DEPTH):
        g = ln_g[l].reshape(3, 1, d)
        b = ln_b[l].reshape(3, 1, d)
        wa = jnp.zeros((128, 128), F32).at[0:GLA_RANK, :].set(gla_w_a2[l])
        ba = gla_b_a[l].reshape(1, 128)
        gn = gla_norm[l].reshape(1, GROUP_WIDTH)
        rn = ret_norm[l].reshape(1, GROUP_WIDTH)
        w1bd, posf, w2bd = _cmp_weights(nsa_cmp_w1[l], nsa_cmp_pos[l], nsa_cmp_w2[l])

        xp = ffn_ln(xp, wu, wd, l, g[0], b[0])
        pg, pr, nq, nkv, nwg, mq, mkv = proj(xp, wi, l)
        o_gla, st_gla = gla_prompt(pg, bp, sp, wa, ba, gn)
        o_ret, st_ret = ret_prompt(pr, bp, sp, rn)
        o_nsa = nsa_prompt(nq, nkv, nwg, bp, sp, w1bd, posf, w2bd)
        o_moba = moba_prompt(mq, mkv, bp, sp)
        xp = out_ffn_ln(xp, (o_gla, o_ret, o_nsa, o_moba), wo, wu, wd, l, g[1], b[1], g[2], b[2])
        outs["nsa_p"].append(nkv.reshape(bp, sp, 4, HEAD_DIM))
        outs["moba_p"].append(mkv.reshape(bp, sp, 2, N_HEADS, HEAD_DIM))
        keep = min(WINDOW, sp)
        outs["win_p"].append(nwg.reshape(bp, sp, GROUP_WIDTH)[:, sp - keep:, 0:128].reshape(bp, keep, 2, HEAD_DIM))
        outs["gla_p"].append(st_gla)
        outs["ret_p"].append(st_ret)

        xs = ffn_ln(xs, wu, wd, l, g[0], b[0])
        pg, pr, nq, nkv, nwg, mq, mkv = proj(xs, wi, l)
        to_lanes = lambda a: jnp.transpose(a.reshape(bs, ss, a.shape[1]), (1, 2, 0))
        og_t, sg_t = lin_decode(to_lanes(pg), gla_t[l], "gla", wa.T,
                                ba.reshape(128, 1), jnp.zeros((128, 1), F32), gn.reshape(GROUP_WIDTH, 1))
        or_t, sr_t = lin_decode(to_lanes(pr), ret_t[l], "ret", jnp.zeros((128, 128), F32),
                                jnp.zeros((128, 1), F32), la_ret.reshape(256, 1), rn.reshape(GROUP_WIDTH, 1))
        from_lanes = lambda a: jnp.transpose(a, (2, 0, 1)).reshape(bs * ss, GROUP_WIDTH)
        o_nsa, win_next = nsa_decode((nq, nkv, nwg), page_ids, nsa_t, l, win_t, win_next, w1bd, posf, w2bd,
                                     steps=ss, past_len=past_len)
        o_moba = moba_decode((mq, mkv), page_ids, moba_t, l, steps=ss, past_len=past_len)
        xs = out_ffn_ln(xs, (from_lanes(og_t), from_lanes(or_t), o_nsa, o_moba), wo, wu, wd, l,
                        g[1], b[1], g[2], b[2])
        outs["nsa_s"].append(nkv.reshape(bs, ss, 4, HEAD_DIM))
        outs["moba_s"].append(mkv.reshape(bs, ss, 2, N_HEADS, HEAD_DIM))
        outs["gla_s"].append(jnp.transpose(sg_t.reshape(N_HEADS, GLA_DK, HEAD_DIM, bs), (3, 0, 1, 2)))
        outs["ret_s"].append(jnp.transpose(sr_t.reshape(N_HEADS, HEAD_DIM, HEAD_DIM, bs), (3, 0, 1, 2)))

    win_s = jnp.transpose(win_next, (0, 1, 4, 2, 3))
    return (xp.reshape(bp, sp, d), xs.reshape(bs, ss, d),
            jnp.stack(outs["nsa_p"], axis=1), jnp.stack(outs["moba_p"], axis=1),
            jnp.stack(outs["win_p"], axis=0), jnp.stack(outs["gla_p"], axis=0), jnp.stack(outs["ret_p"], axis=0),
            jnp.stack(outs["nsa_s"], axis=1), jnp.stack(outs["moba_s"], axis=1),
            win_s, jnp.stack(outs["gla_s"], axis=0), jnp.stack(outs["ret_s"], axis=0))
```

```python
import functools
import math

import numpy as np
import jax
import jax.numpy as jnp
from jax import lax
from jax.experimental import pallas as pl
from jax.experimental.pallas import tpu as pltpu

F32 = jnp.float32
BF16 = jnp.bfloat16
HI = lax.Precision.HIGHEST

D_MODEL = 1024
DEPTH = 4
PAGE_SIZE = 128
HEAD_DIM = 64
N_HEADS = 4
GROUP_WIDTH = 256
GLA_DK = 32
GLA_RANK = 16
GLA_TAU = 16.0
CMP_LEN = 32
CMP_STRIDE = 16
CMP_HIDDEN = 128
SEL_BLOCK = 64
N_SEL = 8
WINDOW = 512
MOBA_BLOCK = 256
MOBA_TOPK = 3
D_FF = 2816
ALPHA = (2 * DEPTH) ** 0.25
LN_EPS = 1e-5
NEG = -1e30
BIG = 1e30
TINY = 1e-30

LANES = 128
VMEM_LIMIT = 56 * 1024 * 1024

GLA_W = 896
RET_W = 1024
NSA_W = 768
MOBA_W = 768
PROJ_SPLITS = (GLA_W, RET_W, 256, 256, 256, 256, 512)


def _slopes():
    n = 2 * N_HEADS
    s = [2.0 ** (-8.0 * i / n) for i in range(1, n + 1)]
    return s[0::2], s[1::2]


NSA_SLOPES, MOBA_SLOPES = _slopes()


def _bdot(a, b):
    return jnp.dot(a.astype(BF16), b.astype(BF16), preferred_element_type=F32)


def _bdot_nt(a, b):
    return lax.dot_general(a.astype(BF16), b.astype(BF16), (((1,), (1,)), ((), ())),
                           preferred_element_type=F32)


def _bdot_tn(a, b):
    return lax.dot_general(a.astype(BF16), b.astype(BF16), (((0,), (0,)), ((), ())),
                           preferred_element_type=F32)


def _hdot(a, b):
    return jnp.dot(a, b, precision=HI, preferred_element_type=F32)


def _hdot_nt(a, b):
    return lax.dot_general(a, b, (((1,), (1,)), ((), ())), precision=HI, preferred_element_type=F32)


def _hdot_tn(a, b):
    return lax.dot_general(a, b, (((0,), (0,)), ((), ())), precision=HI, preferred_element_type=F32)


def _split3(a):
    hi = a.astype(BF16)
    r1 = a - hi.astype(F32)
    mid = r1.astype(BF16)
    lo = (r1 - mid.astype(F32)).astype(BF16)
    return hi, mid, lo


def _xdot(a, c, dims=(((1,), (0,)), ((), ()))):
    cb = c.astype(BF16)
    return sum(lax.dot_general(part, cb, dims, preferred_element_type=F32) for part in _split3(a))


def _xdot_left(c, a):
    cb = c.astype(BF16)
    return sum(jnp.dot(cb, part, preferred_element_type=F32) for part in _split3(a))


def _layer_norm(y, g, b):
    mu = jnp.mean(y, axis=-1, keepdims=True)
    d = y - mu
    var = jnp.mean(d * d, axis=-1, keepdims=True)
    return d * lax.rsqrt(var + LN_EPS) * g + b


def _cparams(sem):
    return pltpu.CompilerParams(dimension_semantics=sem, vmem_limit_bytes=VMEM_LIMIT)


MXU_TILE = 256


def _swiglu(xb, wup_ref, wdn_ref):
    f = wdn_ref.shape[0]
    cut = min(f, (f // 2 // MXU_TILE + 1) * MXU_TILE)
    out = None
    for a0, a1 in ((0, cut), (cut, f)):
        if a1 == a0:
            continue
        u = jnp.dot(xb, wup_ref[:, a0:a1], preferred_element_type=F32)
        gt = jnp.dot(xb, wup_ref[:, f + a0:f + a1], preferred_element_type=F32)
        a = (gt * jax.nn.sigmoid(gt) * u).astype(BF16)
        part = jnp.dot(a, wdn_ref[a0:a1, :], preferred_element_type=F32)
        out = part if out is None else out + part
    return out


def _ffn_ln_body(x_ref, wup_ref, wdn_ref, g_ref, b_ref, o_ref):
    x = x_ref[...]
    y = ALPHA * x + 0.5 * _swiglu(x.astype(BF16), wup_ref, wdn_ref)
    o_ref[...] = _layer_norm(y, g_ref[...], b_ref[...])


def ffn_ln(x, w_up, w_down, layer, g, b, *, tm=512):
    m, d = x.shape
    f = w_down.shape[2]
    tm = min(tm, m)
    return pl.pallas_call(
        _ffn_ln_body,
        grid=(m // tm,),
        in_specs=[
            pl.BlockSpec((tm, d), lambda i: (i, 0)),
            pl.BlockSpec((None, None, d, 2 * f), lambda i: (layer, 0, 0, 0)),
            pl.BlockSpec((None, None, f, d), lambda i: (layer, 0, 0, 0)),
            pl.BlockSpec((1, d), lambda i: (0, 0)),
            pl.BlockSpec((1, d), lambda i: (0, 0)),
        ],
        out_specs=pl.BlockSpec((tm, d), lambda i: (i, 0)),
        out_shape=jax.ShapeDtypeStruct((m, d), F32),
        compiler_params=_cparams(("parallel",)),
        name="ffn_ln",
    )(x, w_up, w_down, g, b)


def _proj_body(x_ref, w_ref, *o_refs):
    xb = x_ref[...].astype(BF16)
    off = 0
    for o_ref in o_refs:
        wdt = o_ref.shape[1]
        o_ref[...] = jnp.dot(xb, w_ref[:, off:off + wdt], preferred_element_type=F32)
        off += wdt


def proj(x, w, layer, *, tm=512):
    m, d = x.shape
    n = w.shape[2]
    tm = min(tm, m)
    return pl.pallas_call(
        _proj_body,
        grid=(m // tm,),
        in_specs=[pl.BlockSpec((tm, d), lambda i: (i, 0)),
                  pl.BlockSpec((None, d, n), lambda i: (layer, 0, 0))],
        out_specs=[pl.BlockSpec((tm, wdt), lambda i: (i, 0)) for wdt in PROJ_SPLITS],
        out_shape=[jax.ShapeDtypeStruct((m, wdt), F32) for wdt in PROJ_SPLITS],
        compiler_params=_cparams(("parallel",)),
        name="proj",
    )(x, w)


def _out_ffn_ln_body(x_ref, o0_ref, o1_ref, o2_ref, o3_ref, wo_ref, g1_ref, b1_ref,
                     wup_ref, wdn_ref, g2_ref, b2_ref, y_ref):
    mix = None
    for gi, o_ref in enumerate((o0_ref, o1_ref, o2_ref, o3_ref)):
        part = jnp.dot(o_ref[...].astype(BF16), wo_ref[gi * GROUP_WIDTH:(gi + 1) * GROUP_WIDTH, :],
                       preferred_element_type=F32)
        mix = part if mix is None else mix + part
    mid = _layer_norm(ALPHA * x_ref[...] + mix, g1_ref[...], b1_ref[...])
    y = ALPHA * mid + 0.5 * _swiglu(mid.astype(BF16), wup_ref, wdn_ref)
    y_ref[...] = _layer_norm(y, g2_ref[...], b2_ref[...])


def out_ffn_ln(x, outs, w_out, w_up, w_down, layer, g1, b1, g2, b2, *, tm=512):
    m, d = x.shape
    f = w_down.shape[2]
    tm = min(tm, m)
    row = lambda i: (i, 0)
    vec = pl.BlockSpec((1, d), lambda i: (0, 0))
    return pl.pallas_call(
        _out_ffn_ln_body,
        grid=(m // tm,),
        in_specs=[pl.BlockSpec((tm, d), row)]
        + [pl.BlockSpec((tm, GROUP_WIDTH), row) for _ in range(4)]
        + [pl.BlockSpec((None, d, d), lambda i: (layer, 0, 0)), vec, vec,
           pl.BlockSpec((None, None, d, 2 * f), lambda i: (layer, 1, 0, 0)),
           pl.BlockSpec((None, None, f, d), lambda i: (layer, 1, 0, 0)), vec, vec],
        out_specs=pl.BlockSpec((tm, d), row),
        out_shape=jax.ShapeDtypeStruct((m, d), F32),
        compiler_params=_cparams(("parallel",)),
        name="out_ffn_ln",
    )(x, *outs, w_out, g1, b1, w_up, w_down, g2, b2)


def _head_group_norm(o, jn, gain):
    mu = _xdot(o, jn)
    d = o - mu
    var = _xdot(d * d, jn)
    return d * lax.rsqrt(var + LN_EPS) * gain


GLA_CHUNK = 16
GLA_ROWS = 256
RET_CHUNK = 256


def _gla_prompt_body(x_ref, wa_ref, ba_ref, gain_ref, tri_ref, tot_ref, emask_ref, jm_ref, jn_ref,
                     o_ref, st_ref, s_scr):
    c, r = GLA_CHUNK, GLA_ROWS
    g = r // c
    kd = 128
    seq = x_ref.shape[0]
    s_scr[...] = jnp.zeros_like(s_scr)
    jm = jm_ref[...]
    si = lax.broadcasted_iota(jnp.int32, (g, c, c, kd), 1)
    ti = lax.broadcasted_iota(jnp.int32, (g, c, c, kd), 2)
    causal = ti >= si

    def step(i, carry):
        rows = pl.ds(pl.multiple_of(i * r, r), r)
        q = x_ref[rows, 0:128] * (GLA_DK ** -0.5)
        k = x_ref[rows, 128:256]
        v = x_ref[rows, 256:512]
        gate = x_ref[rows, 512:768]
        la = jax.nn.log_sigmoid(_hdot(x_ref[rows, 768:896], wa_ref[...]) + ba_ref[...]) / GLA_TAU
        bt = _hdot(tri_ref[...], la)
        btot = _hdot(tot_ref[...], la)
        bt4 = bt.reshape(g, c, kd)
        q4 = q.reshape(g, c, kd)
        k4 = k.reshape(g, c, kd)
        dlt = jnp.minimum(bt4[:, None, :, :] - bt4[:, :, None, :], 0.0)
        w = jnp.where(causal, q4[:, None, :, :] * k4[:, :, None, :] * jnp.exp(dlt), 0.0)
        z = _bdot(w.reshape(g * c * c, kd), jm).reshape(g, c, c, GROUP_WIDTH)
        o = jnp.sum(z * v.reshape(g, c, GROUP_WIDTH)[:, :, None, :], axis=1).reshape(r, GROUP_WIDTH)
        qd = q * jnp.exp(bt)
        kdn = k * jnp.exp(btot - bt)
        edec = jnp.exp(_hdot_tn(la, emask_ref[...]))
        upds = [jnp.where(jm > 0, _bdot_tn(kdn[j * c:(j + 1) * c], v[j * c:(j + 1) * c]), 0.0) for j in range(g)]
        s_cur = s_scr[...]
        starts = []
        for j in range(g):
            starts.append(s_cur.astype(BF16))
            s_cur = jnp.broadcast_to(edec[:, j:j + 1], (kd, GROUP_WIDTH)) * s_cur + upds[j]
        s_scr[...] = s_cur
        chunk_of_row = lax.broadcasted_iota(jnp.int32, (r, kd), 0) // c
        qd_bd = jnp.concatenate([jnp.where(chunk_of_row == j, qd, 0.0).astype(BF16) for j in range(g)], axis=1)
        o = o + jnp.dot(qd_bd, jnp.concatenate(starts, axis=0), preferred_element_type=F32)
        o_ref[rows, :] = _head_group_norm(o, jn_ref[...], gain_ref[...]) * (gate * jax.nn.sigmoid(gate))
        return carry

    lax.fori_loop(0, seq // r, step, 0)
    s_fin = s_scr[...]
    for h in range(N_HEADS):
        st_ref[0, h] = s_fin[h * GLA_DK:(h + 1) * GLA_DK, h * HEAD_DIM:(h + 1) * HEAD_DIM]


def _ret_prompt_body(x_ref, gain_ref, dstack_ref, qdec_ref, kdec_ref, sdec_ref, hm_ref, jm_ref, jn_ref,
                     o_ref, st_ref, s_scr):
    c = RET_CHUNK
    seq = x_ref.shape[0]
    s_scr[...] = jnp.zeros_like(s_scr)

    def step(i, carry):
        rows = pl.ds(pl.multiple_of(i * c, c), c)
        q = x_ref[rows, 0:256]
        k = x_ref[rows, 256:512] * (HEAD_DIM ** -0.5)
        v = x_ref[rows, 512:768]
        gate = x_ref[rows, 768:1024]
        hm = hm_ref[...]
        qs = jnp.concatenate([q] * N_HEADS, axis=0) * hm
        sc = _bdot_nt(qs, k) * dstack_ref[...]
        of = _bdot(sc, v) * hm
        o = of[0:c]
        for h in range(1, N_HEADS):
            o = o + of[h * c:(h + 1) * c]
        s_old = s_scr[...]
        o = o + _bdot(q * qdec_ref[...], s_old)
        upd = jnp.where(jm_ref[...] > 0, _bdot_tn(k * kdec_ref[...], v), 0.0)
        s_scr[...] = sdec_ref[...] * s_old + upd
        o_ref[rows, :] = _head_group_norm(o, jn_ref[...], gain_ref[...]) * (gate * jax.nn.sigmoid(gate))
        return carry

    lax.fori_loop(0, seq // c, step, 0)
    s_fin = s_scr[...]
    for h in range(N_HEADS):
        st_ref[0, h] = s_fin[h * HEAD_DIM:(h + 1) * HEAD_DIM, h * HEAD_DIM:(h + 1) * HEAD_DIM]


def _blockdiag(kd, dk):
    r = np.arange(kd)[:, None] // dk
    cidx = np.arange(GROUP_WIDTH)[None, :] // HEAD_DIM
    return (r == cidx).astype(np.float32)


def _lin_prompt_call(body, name, x, batch, seq, kd, dk, consts):
    full = lambda a: pl.BlockSpec(a.shape, lambda b: (0,) * a.ndim)
    return pl.pallas_call(
        body,
        grid=(batch,),
        in_specs=[pl.BlockSpec((seq, x.shape[1]), lambda b: (b, 0))] + [full(a) for a in consts],
        out_specs=[pl.BlockSpec((seq, GROUP_WIDTH), lambda b: (b, 0)),
                   pl.BlockSpec((1, N_HEADS, dk, HEAD_DIM), lambda b: (b, 0, 0, 0))],
        out_shape=[jax.ShapeDtypeStruct((batch * seq, GROUP_WIDTH), F32),
                   jax.ShapeDtypeStruct((batch, N_HEADS, dk, HEAD_DIM), F32)],
        scratch_shapes=[pltpu.VMEM((kd, GROUP_WIDTH), F32)],
        compiler_params=_cparams(("parallel",)),
        name=name,
    )(x, *consts)


def gla_prompt(x, batch, seq, wa, ba, gain):
    r, c = GLA_ROWS, GLA_CHUNK
    same = (np.arange(r)[:, None] // c) == (np.arange(r)[None, :] // c)
    tri = (same & (np.arange(r)[:, None] >= np.arange(r)[None, :])).astype(np.float32)
    emask = ((np.arange(r)[:, None] // c) == np.arange(LANES)[None, :]).astype(np.float32)
    consts = [wa, ba, gain, jnp.asarray(tri), jnp.asarray(same.astype(np.float32)), jnp.asarray(emask),
              jnp.asarray(_blockdiag(128, GLA_DK)), jnp.asarray(_blockdiag(GROUP_WIDTH, HEAD_DIM) / HEAD_DIM)]
    return _lin_prompt_call(_gla_prompt_body, "gla_prompt", x, batch, seq, 128, GLA_DK, consts)


def ret_prompt(x, batch, seq, gain):
    c = RET_CHUNK
    log_gamma = np.log1p(-np.power(2.0, -5.0 - np.arange(N_HEADS, dtype=np.float64)))
    t = np.arange(c)
    diff = t[:, None] - t[None, :]
    dstack = np.concatenate([np.where(diff >= 0, np.exp(lg * np.maximum(diff, 0)), 0.0) for lg in log_gamma], axis=0)
    per_lane = np.repeat(log_gamma, HEAD_DIM)
    qdec = np.exp(per_lane[None, :] * (t[:, None] + 1))
    kdec = np.exp(per_lane[None, :] * (c - 1 - t[:, None]))
    sdec = np.broadcast_to(np.exp(per_lane * c)[:, None], (GROUP_WIDTH, GROUP_WIDTH))
    hm = np.concatenate([np.broadcast_to(np.arange(GROUP_WIDTH)[None, :] // HEAD_DIM == h, (c, GROUP_WIDTH))
                         for h in range(N_HEADS)], axis=0)
    as_f32 = lambda a: jnp.asarray(np.asarray(a, np.float32))
    consts = [gain, as_f32(dstack), as_f32(qdec), as_f32(kdec), as_f32(sdec), as_f32(hm),
              jnp.asarray(_blockdiag(GROUP_WIDTH, HEAD_DIM)),
              jnp.asarray(_blockdiag(GROUP_WIDTH, HEAD_DIM) / HEAD_DIM)]
    return _lin_prompt_call(_ret_prompt_body, "ret_prompt", x, batch, seq, GROUP_WIDTH, HEAD_DIM, consts)


def _lin_decode_body(x_ref, s0_ref, wa_ref, ba_ref, la_ref, gain_ref, o_ref, s1_ref, a_scr, o_scr,
                     *, kind, kd, dk):
    steps = x_ref.shape[0]
    if kind == "gla":
        qo, ko, vo, go = 0, 128, 256, 512
        for t in range(steps):
            pre = _hdot(wa_ref[...], x_ref[t, 768:896, :]) + ba_ref[...]
            a_scr[t] = jnp.exp(jax.nn.log_sigmoid(pre) / GLA_TAU)
        qscale, kscale = GLA_DK ** -0.5, 1.0
    else:
        qo, ko, vo, go = 0, 256, 512, 768
        for t in range(steps):
            a_scr[t] = jnp.exp(jnp.broadcast_to(la_ref[...], a_scr.shape[1:]))
        qscale, kscale = 1.0, HEAD_DIM ** -0.5
    o_scr[...] = jnp.zeros_like(o_scr)

    def body(j8, carry):
        r0 = pl.multiple_of(j8 * 8, 8)
        h = r0 // dk
        vrows = pl.ds(pl.multiple_of(vo + h * HEAD_DIM, HEAD_DIM), HEAD_DIM)
        orows = pl.ds(pl.multiple_of(h * HEAD_DIM, HEAD_DIM), HEAD_DIM)
        a8 = [a_scr[t, pl.ds(r0, 8), :] for t in range(steps)]
        q8 = [x_ref[t, pl.ds(qo + r0, 8), :] * qscale for t in range(steps)]
        k8 = [x_ref[t, pl.ds(ko + r0, 8), :] * kscale for t in range(steps)]
        for jj in range(8):
            sj = s0_ref[r0 + jj]
            for t in range(steps):
                vt = x_ref[t, vrows, :]
                sj = a8[t][jj:jj + 1, :] * sj + k8[t][jj:jj + 1, :] * vt
                o_scr[t, orows, :] += q8[t][jj:jj + 1, :] * sj
            s1_ref[r0 + jj] = sj
        return carry

    lax.fori_loop(0, kd // 8, body, 0)
    for t in range(steps):
        for h in range(N_HEADS):
            blk = slice(h * HEAD_DIM, (h + 1) * HEAD_DIM)
            o = o_scr[t, blk, :]
            mu = jnp.mean(o, axis=0, keepdims=True)
            d = o - mu
            var = jnp.mean(d * d, axis=0, keepdims=True)
            gate = x_ref[t, go + h * HEAD_DIM:go + (h + 1) * HEAD_DIM, :]
            o_ref[t, blk, :] = d * lax.rsqrt(var + LN_EPS) * gain_ref[blk, :] * (gate * jax.nn.sigmoid(gate))


def lin_decode(xt, s0, kind, wa_t, ba_col, la_col, gain_col):
    kd, dk = (128, GLA_DK) if kind == "gla" else (256, HEAD_DIM)
    steps, _, nb = xt.shape
    return pl.pallas_call(
        functools.partial(_lin_decode_body, kind=kind, kd=kd, dk=dk),
        out_shape=[jax.ShapeDtypeStruct((steps, GROUP_WIDTH, nb), F32),
                   jax.ShapeDtypeStruct((kd, HEAD_DIM, nb), F32)],
        scratch_shapes=[pltpu.VMEM((steps, kd, nb), F32), pltpu.VMEM((steps, GROUP_WIDTH, nb), F32)],
        compiler_params=pltpu.CompilerParams(vmem_limit_bytes=VMEM_LIMIT),
        name="lin_decode_" + kind,
    )(xt, s0, wa_t, ba_col, la_col, gain_col)


def _cmp_mlp(xs_list, pos_ref, w1_ref, w2_ref, n_rows):
    xcat = jnp.concatenate(
        [jnp.concatenate([(xs[pl.ds(r, n_rows, stride=CMP_STRIDE), :] + pos_ref[r:r + 1, :]).astype(BF16)
                          for r in range(CMP_LEN)], axis=1) for xs in xs_list], axis=0)
    hid = jnp.dot(xcat, w1_ref[...].reshape(CMP_LEN * LANES, 2 * CMP_HIDDEN), preferred_element_type=F32)
    return _bdot(jax.nn.gelu(hid), w2_ref[...])


def _rank_select(vals, idx, n_rows, n_top, axis):
    cnt = jnp.zeros(vals.shape, F32)
    for j in range(n_rows):
        vj = vals[j:j + 1, :] if axis == 0 else vals[:, j:j + 1]
        before = (vj > vals) | ((vj == vals) & (j < idx))
        cnt = cnt + before.astype(F32)
    return cnt < n_top


DECODE_SEQS = 4
TQ = 256
TK = 256


POS_HI_LANE = HEAD_DIM
POS_LO_LANE = HEAD_DIM + 1


def _nt(a, b):
    return lax.dot_general(a, b, (((1,), (1,)), ((), ())), preferred_element_type=F32)


def _tn(a, b):
    return lax.dot_general(a, b, (((0,), (0,)), ((), ())), preferred_element_type=F32)


def _key_tile(x128, pos):
    lane = lax.broadcasted_iota(jnp.int32, x128.shape, 1)
    feat = jnp.where(lane == POS_HI_LANE, (pos // 64).astype(F32),
                     jnp.where(lane == POS_LO_LANE, (pos % 64).astype(F32), 0.0))
    return jnp.where(lane < HEAD_DIM, x128, feat).astype(BF16)


def _query_stack(q256, slopes):
    tq = q256.shape[0]
    lane = lax.broadcasted_iota(jnp.int32, (tq, LANES), 1)
    parts = []
    for h in range(N_HEADS):
        pair = q256[:, (h // 2) * LANES:(h // 2 + 1) * LANES]
        if h % 2:
            pair = pltpu.roll(pair, HEAD_DIM, 1)
        const = jnp.where(lane == POS_HI_LANE, 64.0 * slopes[h], jnp.where(lane == POS_LO_LANE, slopes[h], 0.0))
        parts.append(jnp.where(lane < HEAD_DIM, pair * (HEAD_DIM ** -0.5), const).astype(BF16))
    return jnp.concatenate(parts, axis=0)


def _tile_step(k_fn, v_fn, qst_scr, bias, m_scr, l_scr, acc_scr, first, tq, group):
    starts = list(range(0, N_HEADS * tq, group))
    scores = []
    for c0 in starts:
        s = _nt(k_fn(c0 // tq), qst_scr[c0:c0 + group, :])
        scores.append(s if bias is None else s + bias[:, c0:c0 + group])
    probs, stats = [], []
    for c0, s in zip(starts, scores):
        cols = slice(c0, c0 + group)
        if first:
            m_new = jnp.max(s, axis=0, keepdims=True)
            p = jnp.exp(s - m_new)
            stats.append((m_new, jnp.sum(p, axis=0, keepdims=True), None))
        else:
            m_old = m_scr[:, cols]
            m_new = jnp.maximum(m_old, jnp.max(s, axis=0, keepdims=True))
            p = jnp.exp(s - m_new)
            corr = jnp.exp(m_old - m_new)
            stats.append((m_new, corr * l_scr[:, cols] + jnp.sum(p, axis=0, keepdims=True), corr))
        probs.append(p.astype(BF16))
    for c0, p, (m_new, l_new, corr) in zip(starts, probs, stats):
        cols = slice(c0, c0 + group)
        pv = _tn(v_fn(c0 // tq), p)
        acc_scr[:, cols] = pv if first else corr * acc_scr[:, cols] + pv
        m_scr[:, cols] = m_new
        l_scr[:, cols] = l_new


def _tile4(x):
    return jnp.concatenate([x] * N_HEADS, axis=1)


def _tri_bias(tq, keep_lower):
    row = lax.broadcasted_iota(jnp.int32, (TK, tq), 0)
    col = lax.broadcasted_iota(jnp.int32, (TK, tq), 1)
    return _tile4(jnp.where((row <= col) if keep_lower else (row > col), 0.0, NEG))


def _nsa_prompt_body(q_ref, gate_ref, kv_ref, win_ref, w1_ref, pos_ref, w2_ref, imp_ref, o_ref,
                     xs_scr, kc_scr, vc_scr, ks_scr, vs_scr, kw_scr, vw_scr, selb_scr,
                     qst_scr, ocmp_scr, m_scr, l_scr, acc_scr, *, seq):
    qi = pl.program_id(1)
    tq = q_ref.shape[0]
    ncp = seq // CMP_STRIDE
    n_cmp = ncp - 1
    n_sel = seq // SEL_BLOCK
    blocks_per_tile = TK // SEL_BLOCK

    @pl.when(qi == 0)
    def _():
        xs_scr[0:seq, :] = kv_ref[:, 0:128]
        xs_scr[seq:seq + CMP_STRIDE, :] = jnp.zeros((CMP_STRIDE, LANES), F32)
        kvc = _cmp_mlp([xs_scr], pos_ref, w1_ref, w2_ref, ncp)
        cend = lax.broadcasted_iota(jnp.int32, (ncp, LANES), 0) * CMP_STRIDE + (CMP_LEN - 1)
        kc_scr[...] = _key_tile(kvc, cend)
        vc_scr[...] = kvc[:, HEAD_DIM:2 * HEAD_DIM].astype(BF16)
        kpos = lax.broadcasted_iota(jnp.int32, (seq, LANES), 0)
        ks_scr[...] = _key_tile(kv_ref[:, 128:256], kpos)
        vs_scr[...] = kv_ref[:, 192:256].astype(BF16)
        kw_scr[...] = _key_tile(win_ref[:, 0:128], kpos)
        vw_scr[...] = win_ref[:, 64:128].astype(BF16)

    qpos = qi * tq + lax.broadcasted_iota(jnp.int32, (1, tq), 1)
    gates_t = jax.nn.sigmoid(gate_ref[:, 128:256]).T
    qst_scr[...] = _query_stack(q_ref[...], NSA_SLOPES)

    n_io = lax.broadcasted_iota(jnp.int32, (ncp, tq), 0)
    cmask = _tile4(((n_io * CMP_STRIDE + (CMP_LEN - 1)) <= qpos) & (n_io < n_cmp))
    s = jnp.where(cmask, _nt(kc_scr[...], qst_scr[...]), NEG)
    e = jnp.where(cmask, jnp.exp(s - jnp.max(s, axis=0, keepdims=True)), 0.0)
    pc = e / jnp.maximum(jnp.sum(e, axis=0, keepdims=True), TINY)
    ocmp_scr[...] = _tn(vc_scr[...], pc.astype(BF16))
    pc_sum = pc[:, 0:tq]
    for h in range(1, N_HEADS):
        pc_sum = pc_sum + pc[:, h * tq:(h + 1) * tq]

    imp = _xdot_left(imp_ref[...], pc_sum)
    blk = lax.broadcasted_iota(jnp.int32, (n_sel, tq), 0)
    cur = qpos // SEL_BLOCK
    forced = (blk == 0) | (blk == cur) | (blk == cur - 1)
    vals = jnp.where(blk <= cur, jnp.where(forced, BIG, imp), NEG)
    chosen = _rank_select(vals, blk, n_sel, min(N_SEL, n_sel), 0)
    selb_scr[...] = jnp.where(chosen, 0.0, NEG)

    def sel_bias(kt):
        rows = [jnp.broadcast_to(selb_scr[pl.ds(kt * blocks_per_tile + i, 1), :], (SEL_BLOCK, tq))
                for i in range(blocks_per_tile)]
        return _tile4(jnp.concatenate(rows, axis=0))

    def tile_rows(kt):
        return pl.ds(pl.multiple_of(kt * TK, TK), TK)

    def step(k_scr, v_scr, rows, bias, first):
        _tile_step(lambda h: k_scr[rows, :], lambda h: v_scr[rows, :], qst_scr, bias,
                   m_scr, l_scr, acc_scr, first, tq, 2 * tq)

    diag = tile_rows(qi)
    step(ks_scr, vs_scr, diag, sel_bias(qi) + _tri_bias(tq, True), True)

    def sel_body(kt, carry):
        step(ks_scr, vs_scr, tile_rows(kt), sel_bias(kt), False)
        return carry

    lax.fori_loop(0, qi, sel_body, 0)
    gate_row = lambda j: jnp.concatenate([gates_t[3 * h + j:3 * h + j + 1, :] for h in range(N_HEADS)], axis=1)
    out = gate_row(0) * ocmp_scr[...] + gate_row(1) * (acc_scr[...] / l_scr[...])

    step(kw_scr, vw_scr, diag, _tri_bias(tq, True), True)

    @pl.when(qi >= 1)
    def _():
        step(kw_scr, vw_scr, tile_rows(qi - 1), None, False)

    @pl.when(qi >= WINDOW // TK)
    def _():
        step(kw_scr, vw_scr, tile_rows(qi - WINDOW // TK), _tri_bias(tq, False), False)

    out = out + gate_row(2) * (acc_scr[...] / l_scr[...])
    o_ref[...] = jnp.concatenate([out[:, h * tq:(h + 1) * tq] for h in range(N_HEADS)], axis=0).T


def _cmp_to_sel(n_cmp, n_sel):
    ratio, span = SEL_BLOCK // CMP_STRIDE, CMP_LEN // CMP_STRIDE
    j, m, n = np.meshgrid(np.arange(n_sel), np.arange(ratio), np.arange(span), indexing="ij")
    i = ratio * j + m - n
    ok = (i >= 0) & (i < n_cmp)
    mat = np.zeros((n_cmp, n_sel), np.float32)
    np.add.at(mat, (i[ok], j[ok]), 1.0)
    return mat


def nsa_prompt(xq, xkv, xwg, batch, seq, w1, pos, w2):
    nq = seq // TQ
    ncp = seq // CMP_STRIDE
    n_sel = seq // SEL_BLOCK
    imp_t = np.zeros((n_sel, ncp), np.float32)
    imp_t[:, :ncp - 1] = _cmp_to_sel(ncp - 1, n_sel).T
    full = lambda *shape: pl.BlockSpec(shape, lambda b, i: (0,) * len(shape))
    return pl.pallas_call(
        functools.partial(_nsa_prompt_body, seq=seq),
        grid=(batch, nq),
        in_specs=[pl.BlockSpec((TQ, GROUP_WIDTH), lambda b, i: (b * nq + i, 0)),
                  pl.BlockSpec((TQ, GROUP_WIDTH), lambda b, i: (b * nq + i, 0)),
                  pl.BlockSpec((seq, GROUP_WIDTH), lambda b, i: (b, 0)),
                  pl.BlockSpec((seq, GROUP_WIDTH), lambda b, i: (b, 0)),
                  full(CMP_LEN, LANES, 2 * CMP_HIDDEN), full(CMP_LEN, LANES),
                  full(2 * CMP_HIDDEN, LANES), full(n_sel, ncp)],
        out_specs=pl.BlockSpec((TQ, GROUP_WIDTH), lambda b, i: (b * nq + i, 0)),
        out_shape=jax.ShapeDtypeStruct((batch * seq, GROUP_WIDTH), F32),
        scratch_shapes=[pltpu.VMEM((seq + CMP_STRIDE, LANES), F32),
                        pltpu.VMEM((ncp, LANES), BF16), pltpu.VMEM((ncp, HEAD_DIM), BF16),
                        pltpu.VMEM((seq, LANES), BF16), pltpu.VMEM((seq, HEAD_DIM), BF16),
                        pltpu.VMEM((seq, LANES), BF16), pltpu.VMEM((seq, HEAD_DIM), BF16),
                        pltpu.VMEM((n_sel, TQ), F32),
                        pltpu.VMEM((N_HEADS * TQ, LANES), BF16), pltpu.VMEM((HEAD_DIM, N_HEADS * TQ), F32),
                        pltpu.VMEM((1, N_HEADS * TQ), F32), pltpu.VMEM((1, N_HEADS * TQ), F32),
                        pltpu.VMEM((HEAD_DIM, N_HEADS * TQ), F32)],
        compiler_params=_cparams(("parallel", "arbitrary")),
        name="nsa_prompt",
    )(xq, xwg, xkv, xwg, w1, pos, w2, jnp.asarray(imp_t))


def _moba_prompt_body(q_ref, kv_ref, o_ref, km_scr, ka_scr, va_scr, selb_scr, qst_scr,
                      m_scr, l_scr, acc_scr, *, seq):
    qi = pl.program_id(1)
    tq = q_ref.shape[0]
    nb = seq // MOBA_BLOCK

    @pl.when(qi == 0)
    def _():
        for n in range(nb):
            km_scr[n:n + 1, :] = jnp.mean(kv_ref[n * MOBA_BLOCK:(n + 1) * MOBA_BLOCK, 0:GROUP_WIDTH],
                                          axis=0, keepdims=True)

        kpos = lax.broadcasted_iota(jnp.int32, (seq, LANES), 0)
        for h in range(N_HEADS):
            pair = kv_ref[:, (h // 2) * LANES:(h // 2 + 1) * LANES]
            if h % 2:
                pair = pltpu.roll(pair, HEAD_DIM, 1)
            ka_scr[h] = _key_tile(pair, kpos)
            va_scr[h] = kv_ref[:, GROUP_WIDTH + h * HEAD_DIM:GROUP_WIDTH + (h + 1) * HEAD_DIM].astype(BF16)

    qpos = qi * tq + lax.broadcasted_iota(jnp.int32, (1, tq), 1)
    own = qpos // MOBA_BLOCK
    blk = lax.broadcasted_iota(jnp.int32, (nb, tq), 0)
    past = blk < own
    for h in range(N_HEADS):
        hs = slice(h * HEAD_DIM, (h + 1) * HEAD_DIM)
        gate = _hdot_nt(km_scr[:, hs], q_ref[:, hs])
        chosen = _rank_select(jnp.where(past, gate, NEG), blk, nb, min(MOBA_TOPK, nb), 0) & past
        selb_scr[h] = jnp.where(chosen, 0.0, NEG)

    qst_scr[...] = _query_stack(q_ref[...], MOBA_SLOPES)

    def step(rows, bias, first):
        _tile_step(lambda h: ka_scr[h, rows, :], lambda h: va_scr[h, rows, :], qst_scr, bias,
                   m_scr, l_scr, acc_scr, first, tq, tq)

    step(pl.ds(pl.multiple_of(qi * TK, TK), TK), _tri_bias(tq, True), True)

    def body(kt, carry):
        bias = jnp.concatenate([selb_scr[h, pl.ds(kt, 1), :] for h in range(N_HEADS)], axis=1)
        step(pl.ds(pl.multiple_of(kt * TK, TK), TK), bias, False)
        return carry

    lax.fori_loop(0, qi, body, 0)
    out = acc_scr[...] / l_scr[...]
    o_ref[...] = jnp.concatenate([out[:, h * tq:(h + 1) * tq] for h in range(N_HEADS)], axis=0).T


def moba_prompt(xq, xkv, batch, seq):
    nq = seq // TQ
    nb = seq // MOBA_BLOCK
    return pl.pallas_call(
        functools.partial(_moba_prompt_body, seq=seq),
        grid=(batch, nq),
        in_specs=[pl.BlockSpec((TQ, GROUP_WIDTH), lambda b, i: (b * nq + i, 0)),
                  pl.BlockSpec((seq, 2 * GROUP_WIDTH), lambda b, i: (b, 0))],
        out_specs=pl.BlockSpec((TQ, GROUP_WIDTH), lambda b, i: (b * nq + i, 0)),
        out_shape=jax.ShapeDtypeStruct((batch * seq, GROUP_WIDTH), F32),
        scratch_shapes=[pltpu.VMEM((nb, GROUP_WIDTH), F32),
                        pltpu.VMEM((N_HEADS, seq, LANES), BF16), pltpu.VMEM((N_HEADS, seq, HEAD_DIM), BF16),
                        pltpu.VMEM((N_HEADS, nb, TQ), F32),
                        pltpu.VMEM((N_HEADS * TQ, LANES), BF16),
                        pltpu.VMEM((1, N_HEADS * TQ), F32), pltpu.VMEM((1, N_HEADS * TQ), F32),
                        pltpu.VMEM((HEAD_DIM, N_HEADS * TQ), F32)],
        compiler_params=_cparams(("parallel", "arbitrary")),
        name="moba_prompt",
    )(xq, xkv)


def _softmax_rows(parts, masks):
    masked = [jnp.where(mk, s, NEG) for s, mk in zip(parts, masks)]
    m = masked[0].max(axis=1, keepdims=True)
    for s in masked[1:]:
        m = jnp.maximum(m, s.max(axis=1, keepdims=True))
    es = [jnp.where(mk, jnp.exp(s - m), 0.0) for s, mk in zip(masked, masks)]
    den = es[0].sum(axis=1, keepdims=True)
    for e in es[1:]:
        den = den + e.sum(axis=1, keepdims=True)
    inv = 1.0 / jnp.maximum(den, TINY)
    return [e * inv for e in es]


def _stack_heads(x, off):
    return jnp.concatenate([x[:, off + h * HEAD_DIM:off + (h + 1) * HEAD_DIM] for h in range(N_HEADS)], axis=0)


def _round_robin(gens):
    outs = [None] * len(gens)
    live = list(range(len(gens)))
    while live:
        for i in list(live):
            try:
                next(gens[i])
            except StopIteration as stop:
                outs[i] = stop.value
                live.remove(i)
    return outs


def _pad_rows(x, n):
    return jnp.concatenate([x, jnp.zeros((n - x.shape[0], x.shape[1]), x.dtype)], axis=0)


def _nsa_decode_body(*refs, n_pages, past_len, ns):
    x_refs = refs[1:4]
    all_pages = refs[4:4 + ns * n_pages]
    (win_ref, w1_ref, pos_ref, w2_ref, imp_ref, esel_ref, place_ref, _, o_ref, wout_ref,
     xs_scr) = refs[4 + ns * n_pages:]
    steps = x_refs[0].shape[0] // ns
    x_rows = lambda i: jnp.concatenate([r[i * steps:(i + 1) * steps, :] for r in x_refs], axis=1)
    ncp = past_len // CMP_STRIDE
    n_win = win_ref.shape[-1]
    lane = lax.broadcasted_iota(jnp.int32, (HEAD_DIM, n_win), 1)
    for i in range(ns):
        new_rows = x_refs[2][i * steps:(i + 1) * steps, 0:2 * HEAD_DIM]
        for c in range(2):
            placed = _hdot_tn(new_rows[:, c * HEAD_DIM:(c + 1) * HEAD_DIM], place_ref[...])
            shifted = pltpu.roll(win_ref[0, i, c], n_win - steps, 1)
            wout_ref[0, i, c] = jnp.where(lane < n_win - steps, shifted, placed)
    for i in range(ns):
        for p in range(n_pages):
            xs_scr[i, p * PAGE_SIZE:(p + 1) * PAGE_SIZE, :] = (
                all_pages[i * n_pages + p][0, 0, 0:2].reshape(LANES, PAGE_SIZE).T)
        xs_scr[i, past_len:past_len + CMP_STRIDE, :] = jnp.zeros((CMP_STRIDE, LANES), F32)
    kvc_all = _cmp_mlp([xs_scr.at[i] for i in range(ns)], pos_ref, w1_ref, w2_ref, ncp)
    outs = _round_robin([_nsa_decode_seq(
        x_rows(i), all_pages[i * n_pages:(i + 1) * n_pages], win_ref, i,
        kvc_all[i * ncp:(i + 1) * ncp], imp_ref, esel_ref, n_pages, past_len) for i in range(ns)])
    for i in range(ns):
        o_ref[i * steps:(i + 1) * steps, :] = outs[i]


def _nsa_decode_seq(x, page_refs, win_ref, wi, kvc, imp_ref, esel_ref, n_pages, past_len):
    steps = x.shape[0]
    rows = N_HEADS * steps
    ncp = past_len // CMP_STRIDE
    n_cmp = ncp - 1
    tpos = past_len + lax.broadcasted_iota(jnp.int32, (steps, 1), 0)
    qpos = jnp.concatenate([tpos] * N_HEADS, axis=0)
    slope = jnp.concatenate([jnp.full((steps, 1), s, F32) for s in NSA_SLOPES], axis=0)
    qs = _stack_heads(x, 0) * (HEAD_DIM ** -0.5)
    gates = jax.nn.sigmoid(x[:, 640:768])
    gcol = [jnp.concatenate([gates[:, 3 * h + j:3 * h + j + 1] for h in range(N_HEADS)], axis=0)
            for j in range(3)]
    new_io = lax.broadcasted_iota(jnp.int32, (rows, LANES), 1)
    new_pos = past_len + new_io
    new_ok = (new_io < steps) & (new_pos <= qpos)

    k_cmp, v_cmp = kvc[:, 0:HEAD_DIM], kvc[:, HEAD_DIM:2 * HEAD_DIM]
    n_io = lax.broadcasted_iota(jnp.int32, (rows, ncp), 1)
    cend = n_io * CMP_STRIDE + (CMP_LEN - 1)
    (pc,) = _softmax_rows([_bdot_nt(qs, k_cmp) + slope * cend.astype(F32)],
                          [(cend <= qpos) & (n_io < n_cmp)])
    o_cmp = _bdot(pc, v_cmp)
    yield
    pc_sum = pc[0:steps]
    for h in range(1, N_HEADS):
        pc_sum = pc_sum + pc[h * steps:(h + 1) * steps]

    n_sel = -(-(past_len + steps) // SEL_BLOCK)
    imp = _hdot(pc_sum, imp_ref[...])
    yield
    blk = lax.broadcasted_iota(jnp.int32, (steps, LANES), 1)
    cur = tpos // SEL_BLOCK
    forced = (blk == 0) | (blk == cur) | (blk == cur - 1)
    vals = jnp.where((blk <= cur) & (blk < n_sel), jnp.where(forced, BIG, imp), NEG)
    chosen = _rank_select(vals, blk, n_sel, min(N_SEL, n_sel), 1).astype(F32)
    key_sel = _bdot(chosen, esel_ref[...])
    key_sel = jnp.concatenate([key_sel] * N_HEADS, axis=0) > 0.5
    yield

    cat_pages = lambda c: jnp.concatenate([page_refs[p][0, 0, c].astype(BF16) for p in range(n_pages)], axis=1)
    s_past = _bdot(qs, cat_pages(2))
    kpos = lax.broadcasted_iota(jnp.int32, (rows, past_len), 1)
    k_new = _pad_rows(x[:, 384:448], LANES)
    v_new = _pad_rows(x[:, 448:512], LANES)
    s_new = _bdot_nt(qs, k_new) + slope * new_pos.astype(F32)
    yield
    p_past, p_new = _softmax_rows([s_past + slope * kpos.astype(F32), s_new],
                                  [key_sel[:, 0:past_len] & (kpos <= qpos),
                                   key_sel[:, past_len:past_len + LANES] & new_ok])
    o_sel = _bdot(p_new, v_new) + _bdot_nt(p_past, cat_pages(3))
    yield

    n_win = win_ref.shape[-1]
    wpos = (past_len - n_win) + lax.broadcasted_iota(jnp.int32, (rows, n_win), 1)
    kw_new = _pad_rows(x[:, 512:576], LANES)
    vw_new = _pad_rows(x[:, 576:640], LANES)
    s_w = _bdot(qs, win_ref[0, wi, 0]) + slope * wpos.astype(F32)
    s_wn = _bdot_nt(qs, kw_new) + slope * new_pos.astype(F32)
    yield
    dist = qpos - wpos
    p_w, p_wn = _softmax_rows([s_w, s_wn],
                              [(dist >= 0) & (dist < WINDOW), new_ok & (qpos - new_pos < WINDOW)])
    o_win = _bdot_nt(p_w, win_ref[0, wi, 1]) + _bdot(p_wn, vw_new)

    out = gcol[0] * o_cmp + gcol[1] * o_sel + gcol[2] * o_win
    return jnp.concatenate([out[h * steps:(h + 1) * steps] for h in range(N_HEADS)], axis=1)


def nsa_decode(xs, page_ids, cache_t, layer, win_t, win_next, w1, pos, w2, *, steps, past_len):
    nseq = xs[0].shape[0] // steps
    n_pages = past_len // PAGE_SIZE
    ncp = past_len // CMP_STRIDE
    n_sel = -(-(past_len + steps) // SEL_BLOCK)
    imp = np.zeros((ncp, LANES), np.float32)
    imp[:ncp - 1, :n_sel] = _cmp_to_sel(ncp - 1, n_sel)
    kblk = np.concatenate([np.arange(past_len) // SEL_BLOCK,
                           (past_len + np.arange(LANES)) // SEL_BLOCK])
    esel = (np.arange(LANES)[:, None] == kblk[None, :]).astype(np.float32)
    n_win = win_t.shape[-1]

    ns = math.gcd(nseq, DECODE_SEQS)

    def page_spec(j):
        return pl.BlockSpec((1, 1, 4, HEAD_DIM, PAGE_SIZE),
                            lambda b, pt, j=j: (pt[b * (ns * n_pages) + j], layer, 0, 0, 0))

    place = (np.arange(n_win)[None, :] == (n_win - steps + np.arange(steps))[:, None]).astype(np.float32)
    full = lambda *shape: pl.BlockSpec(shape, lambda b, pt: (0,) * len(shape))
    win_spec = pl.BlockSpec((1, ns, 2, HEAD_DIM, n_win), lambda b, pt: (layer, b, 0, 0, 0))
    inputs = (page_ids, *xs, *([cache_t] * (ns * n_pages)), win_t, w1, pos, w2, jnp.asarray(imp),
              jnp.asarray(esel), jnp.asarray(place), win_next)
    return pl.pallas_call(
        functools.partial(_nsa_decode_body, n_pages=n_pages, past_len=past_len, ns=ns),
        grid_spec=pltpu.PrefetchScalarGridSpec(
            num_scalar_prefetch=1,
            grid=(nseq // ns,),
            in_specs=[pl.BlockSpec((ns * steps, GROUP_WIDTH), lambda b, pt: (b, 0)) for _ in xs]
            + [page_spec(j) for j in range(ns * n_pages)]
            + [win_spec,
               full(CMP_LEN, LANES, 2 * CMP_HIDDEN), full(CMP_LEN, LANES),
               full(2 * CMP_HIDDEN, LANES), full(ncp, LANES), full(LANES, past_len + LANES),
               full(steps, n_win), pl.BlockSpec(memory_space=pl.ANY)],
            out_specs=[pl.BlockSpec((ns * steps, GROUP_WIDTH), lambda b, pt: (b, 0)), win_spec],
            scratch_shapes=[pltpu.VMEM((ns, past_len + CMP_STRIDE, LANES), F32)],
        ),
        out_shape=[jax.ShapeDtypeStruct((nseq * steps, GROUP_WIDTH), F32),
                   jax.ShapeDtypeStruct(win_next.shape, F32)],
        input_output_aliases={len(inputs) - 1: 1},
        compiler_params=_cparams(("parallel",)),
        name="nsa_decode",
    )(*inputs)


def _moba_decode_body(*refs, n_pages, past_len, ns):
    x_refs = refs[1:3]
    all_pages = refs[3:3 + ns * n_pages]
    emean_ref, eblk_ref, o_ref = refs[3 + ns * n_pages:]
    steps = x_refs[0].shape[0] // ns
    x_rows = lambda i: jnp.concatenate([r[i * steps:(i + 1) * steps, :] for r in x_refs], axis=1)
    outs = _round_robin([_moba_decode_seq(
        x_rows(i), all_pages[i * n_pages:(i + 1) * n_pages],
        emean_ref, eblk_ref, n_pages, past_len) for i in range(ns)])
    for i in range(ns):
        o_ref[i * steps:(i + 1) * steps, :] = outs[i]


def _moba_decode_seq(x, page_refs, emean_ref, eblk_ref, n_pages, past_len):
    steps = x.shape[0]
    rows = N_HEADS * steps
    tpos = past_len + lax.broadcasted_iota(jnp.int32, (steps, 1), 0)
    qpos = jnp.concatenate([tpos] * N_HEADS, axis=0)
    own = qpos // MOBA_BLOCK
    slope = jnp.concatenate([jnp.full((steps, 1), s, F32) for s in MOBA_SLOPES], axis=0)
    head_of_row = lax.broadcasted_iota(jnp.int32, (rows, GROUP_WIDTH), 0) // steps
    head_of_col = lax.broadcasted_iota(jnp.int32, (rows, GROUP_WIDTH), 1) // HEAD_DIM
    diag = head_of_row == head_of_col
    q_bd = jnp.where(diag, jnp.concatenate([x[:, 0:GROUP_WIDTH]] * N_HEADS, axis=0), 0.0)

    cat_pages = lambda c: jnp.concatenate(
        [page_refs[p][0, 0, c].reshape(GROUP_WIDTH, PAGE_SIZE).astype(BF16) for p in range(n_pages)], axis=1)
    kt_all = cat_pages(0)
    kmean_t = jnp.dot(kt_all, emean_ref[...], preferred_element_type=F32)
    yield
    gate = _hdot(q_bd, kmean_t)
    yield
    nb = -(-(past_len + steps) // MOBA_BLOCK)
    blk = lax.broadcasted_iota(jnp.int32, (rows, LANES), 1)
    past = (blk < own) & (blk < nb)
    vals = jnp.where(past, gate, NEG)
    chosen = (_rank_select(vals, blk, nb, min(MOBA_TOPK, nb), 1) & past).astype(F32)
    key_sel = _bdot(chosen, eblk_ref[...]) > 0.5
    yield

    q_sc = q_bd * (HEAD_DIM ** -0.5)
    s_past = _bdot(q_sc, kt_all)
    kpos = lax.broadcasted_iota(jnp.int32, (rows, past_len), 1)
    new_io = lax.broadcasted_iota(jnp.int32, (rows, LANES), 1)
    new_pos = past_len + new_io
    k_new = _pad_rows(x[:, 256:512], LANES)
    v_new = _pad_rows(x[:, 512:768], LANES)
    s_new = _bdot_nt(q_sc, k_new) + slope * new_pos.astype(F32)
    yield
    in_own_past = (kpos // MOBA_BLOCK) == own
    in_own_new = (new_pos // MOBA_BLOCK) == own
    p_past, p_new = _softmax_rows(
        [s_past + slope * kpos.astype(F32), s_new],
        [key_sel | in_own_past, (new_io < steps) & (new_pos <= qpos) & in_own_new])
    o_all = _bdot(p_new, v_new) + _bdot_nt(p_past, cat_pages(1))
    o_all = jnp.where(diag, o_all, 0.0)
    out = o_all[0:steps]
    for h in range(1, N_HEADS):
        out = out + o_all[h * steps:(h + 1) * steps]
    return out


def moba_decode(xs, page_ids, cache_t, layer, *, steps, past_len):
    nseq = xs[0].shape[0] // steps
    n_pages = past_len // PAGE_SIZE
    pages_per_blk = MOBA_BLOCK // PAGE_SIZE
    emean = np.zeros((n_pages, PAGE_SIZE, LANES), np.float32)
    for p in range(n_pages):
        emean[p, :, p // pages_per_blk] = 1.0 / MOBA_BLOCK
    eblk = (np.arange(LANES)[:, None] == (np.arange(past_len) // MOBA_BLOCK)[None, :]).astype(np.float32)

    ns = math.gcd(nseq, DECODE_SEQS)

    def page_spec(j):
        return pl.BlockSpec((1, 1, 2, N_HEADS, HEAD_DIM, PAGE_SIZE),
                            lambda b, pt, j=j: (pt[b * (ns * n_pages) + j], layer, 0, 0, 0, 0))

    full = lambda *shape: pl.BlockSpec(shape, lambda b, pt: (0,) * len(shape))
    return pl.pallas_call(
        functools.partial(_moba_decode_body, n_pages=n_pages, past_len=past_len, ns=ns),
        grid_spec=pltpu.PrefetchScalarGridSpec(
            num_scalar_prefetch=1,
            grid=(nseq // ns,),
            in_specs=[pl.BlockSpec((ns * steps, a.shape[1]), lambda b, pt: (b, 0)) for a in xs]
            + [page_spec(j) for j in range(ns * n_pages)]
            + [full(past_len, LANES), full(LANES, past_len)],
            out_specs=pl.BlockSpec((ns * steps, GROUP_WIDTH), lambda b, pt: (b, 0)),
        ),
        out_shape=jax.ShapeDtypeStruct((nseq * steps, GROUP_WIDTH), F32),
        compiler_params=_cparams(("parallel",)),
        name="moba_decode",
    )(page_ids, *xs, *([cache_t] * (ns * n_pages)), jnp.asarray(emean.reshape(past_len, LANES), BF16),
      jnp.asarray(eblk))


def _regroup_w_in(w):
    z = lambda n: jnp.zeros(w.shape[:2] + (n,), w.dtype)
    col = lambda a, b: w[:, :, a:b]
    gq, gk, gv, ga, gr = col(0, 128), col(128, 256), col(256, 512), col(512, 528), col(528, 784)
    ret = col(784, 1808)
    nq, nkv, ng = col(1808, 2064), col(2064, 2448), col(2448, 2460)
    moba = col(2460, 3228)
    return jnp.concatenate([gq, gk, gv, gr, ga, z(112), ret, nq, nkv, ng, z(116), moba], axis=2).astype(BF16)


def _cmp_weights(w1, pos, w2):
    w1r = w1.reshape(2, CMP_LEN, HEAD_DIM, CMP_HIDDEN)
    zero = jnp.zeros((CMP_LEN, HEAD_DIM, CMP_HIDDEN), w1.dtype)
    w1bd = jnp.concatenate([jnp.concatenate([w1r[0], zero], axis=2),
                            jnp.concatenate([zero, w1r[1]], axis=2)], axis=1)
    zero2 = jnp.zeros((CMP_HIDDEN, HEAD_DIM), w2.dtype)
    w2bd = jnp.concatenate([jnp.concatenate([w2[0], zero2], axis=1),
                            jnp.concatenate([zero2, w2[1]], axis=1)], axis=0)
    posf = jnp.concatenate([pos[0], pos[1]], axis=1)
    return w1bd.astype(BF16), posf, w2bd.astype(BF16)


def kernel(x_prompt, x_sample, cache_nsa, cache_moba, cache_nsa_win, state_gla, state_ret, page_table, w_in, gla_w_a2, gla_b_a, gla_norm, ret_norm, nsa_cmp_w1, nsa_cmp_w2, nsa_cmp_pos, w_out, ffn_w_up, ffn_w_down, ln_g, ln_b):
    bp, sp, d = x_prompt.shape
    bs, ss, _ = x_sample.shape
    past_len = page_table.shape[1] * PAGE_SIZE
    assert cache_nsa_win.shape[2] >= ss, "the sample group's window buffer must hold at least the new rows"

    nsa_t = jnp.transpose(cache_nsa, (0, 1, 3, 4, 2))
    moba_t = jnp.transpose(cache_moba, (0, 1, 3, 4, 5, 2))
    win_t = jnp.transpose(cache_nsa_win, (0, 1, 3, 4, 2))
    gla_t = jnp.transpose(state_gla, (0, 2, 3, 4, 1)).reshape(DEPTH, N_HEADS * GLA_DK, HEAD_DIM, bs)
    ret_t = jnp.transpose(state_ret, (0, 2, 3, 4, 1)).reshape(DEPTH, N_HEADS * HEAD_DIM, HEAD_DIM, bs)
    page_ids = page_table.reshape(-1)

    log_gamma = np.log1p(-np.power(2.0, -5.0 - np.arange(N_HEADS, dtype=np.float64))).astype(np.float32)
    la_ret = jnp.asarray(np.repeat(log_gamma, HEAD_DIM))
    win_next = jnp.zeros(win_t.shape, F32)

    xp = x_prompt.reshape(bp * sp, d)
    xs = x_sample.reshape(bs * ss, d)
    outs = {k: [] for k in ("nsa_p", "moba_p", "win_p", "gla_p", "ret_p", "nsa_s", "moba_s", "gla_s", "ret_s")}
    wu = ffn_w_up.astype(BF16)
    wd = ffn_w_down.astype(BF16)
    wi = _regroup_w_in(w_in)
    wo = w_out.astype(BF16)
    for l in range(DEPTH):
        g = ln_g[l].reshape(3, 1, d)
        b = ln_b[l].reshape(3, 1, d)
        wa = jnp.zeros((128, 128), F32).at[0:GLA_RANK, :].set(gla_w_a2[l])
        ba = gla_b_a[l].reshape(1, 128)
        gn = gla_norm[l].reshape(1, GROUP_WIDTH)
        rn = ret_norm[l].reshape(1, GROUP_WIDTH)
        w1bd, posf, w2bd = _cmp_weights(nsa_cmp_w1[l], nsa_cmp_pos[l], nsa_cmp_w2[l])

        xp = ffn_ln(xp, wu, wd, l, g[0], b[0])
        pg, pr, nq, nkv, nwg, mq, mkv = proj(xp, wi, l)
        o_gla, st_gla = gla_prompt(pg, bp, sp, wa, ba, gn)
        o_ret, st_ret = ret_prompt(pr, bp, sp, rn)
        o_nsa = nsa_prompt(nq, nkv, nwg, bp, sp, w1bd, posf, w2bd)
        o_moba = moba_prompt(mq, mkv, bp, sp)
        xp = out_ffn_ln(xp, (o_gla, o_ret, o_nsa, o_moba), wo, wu, wd, l, g[1], b[1], g[2], b[2])
        outs["nsa_p"].append(nkv.reshape(bp, sp, 4, HEAD_DIM))
        outs["moba_p"].append(mkv.reshape(bp, sp, 2, N_HEADS, HEAD_DIM))
        keep = min(WINDOW, sp)
        outs["win_p"].append(nwg.reshape(bp, sp, GROUP_WIDTH)[:, sp - keep:, 0:128].reshape(bp, keep, 2, HEAD_DIM))
        outs["gla_p"].append(st_gla)
        outs["ret_p"].append(st_ret)

        xs = ffn_ln(xs, wu, wd, l, g[0], b[0])
        pg, pr, nq, nkv, nwg, mq, mkv = proj(xs, wi, l)
        to_lanes = lambda a: jnp.transpose(a.reshape(bs, ss, a.shape[1]), (1, 2, 0))
        og_t, sg_t = lin_decode(to_lanes(pg), gla_t[l], "gla", wa.T,
                                ba.reshape(128, 1), jnp.zeros((128, 1), F32), gn.reshape(GROUP_WIDTH, 1))
        or_t, sr_t = lin_decode(to_lanes(pr), ret_t[l], "ret", jnp.zeros((128, 128), F32),
                                jnp.zeros((128, 1), F32), la_ret.reshape(256, 1), rn.reshape(GROUP_WIDTH, 1))
        from_lanes = lambda a: jnp.transpose(a, (2, 0, 1)).reshape(bs * ss, GROUP_WIDTH)
        o_nsa, win_next = nsa_decode((nq, nkv, nwg), page_ids, nsa_t, l, win_t, win_next, w1bd, posf, w2bd,
                                     steps=ss, past_len=past_len)
        o_moba = moba_decode((mq, mkv), page_ids, moba_t, l, steps=ss, past_len=past_len)
        xs = out_ffn_ln(xs, (from_lanes(og_t), from_lanes(or_t), o_nsa, o_moba), wo, wu, wd, l,
                        g[1], b[1], g[2], b[2])
        outs["nsa_s"].append(nkv.reshape(bs, ss, 4, HEAD_DIM))
        outs["moba_s"].append(mkv.reshape(bs, ss, 2, N_HEADS, HEAD_DIM))
        outs["gla_s"].append(jnp.transpose(sg_t.reshape(N_HEADS, GLA_DK, HEAD_DIM, bs), (3, 0, 1, 2)))
        outs["ret_s"].append(jnp.transpose(sr_t.reshape(N_HEADS, HEAD_DIM, HEAD_DIM, bs), (3, 0, 1, 2)))

    win_s = jnp.transpose(win_next, (0, 1, 4, 2, 3))
    return (xp.reshape(bp, sp, d), xs.reshape(bs, ss, d),
            jnp.stack(outs["nsa_p"], axis=1), jnp.stack(outs["moba_p"], axis=1),
            jnp.stack(outs["win_p"], axis=0), jnp.stack(outs["gla_p"], axis=0), jnp.stack(outs["ret_p"], axis=0),
            jnp.stack(outs["nsa_s"], axis=1), jnp.stack(outs["moba_s"], axis=1),
            win_s, jnp.stack(outs["gla_s"], axis=0), jnp.stack(outs["ret_s"], axis=0))
```
